```python
import jax, jax.numpy as jnp
from jax import lax
import numpy as np

D_MODEL = 1024
BATCH = 8
SEQ = 4096
DEPTH = 2

HEAD_DIM = 64
D_MIX = D_MODEL
CONV_WIDTH = D_MIX // 2
N_CONV_GROUPS = CONV_WIDTH // HEAD_DIM
ATTN_WIDTH = D_MIX - CONV_WIDTH
N_Q_HEADS = ATTN_WIDTH // HEAD_DIM
N_KV_GROUPS = 2
Q_PER_KV = N_Q_HEADS // N_KV_GROUPS
KV_WIDTH = N_KV_GROUPS * HEAD_DIM
N_GATES = 3
CMP_LEN = 32
CMP_STRIDE = 16
CMP_HIDDEN = 256
SLC_LEN = 64
SLC_TOPK = 16
WINDOW = 512
Q_CHUNK = 32
ROPE_THETA = 500000.0
ROPE_DIM = HEAD_DIM // 4
CONV_K = 3
FFN_CONV_K = 3
D_FF = 2816
ALPHA = (2.0 * DEPTH) ** 0.25
BETA = (8.0 * DEPTH) ** -0.25
NEG = -1e30
BIG = 1e9

kernel_name = 'hymba_shortconv_nsa_convffn_deepnorm'


def _in_split_points():
    sizes = [CONV_WIDTH] * 3 + [ATTN_WIDTH] + [KV_WIDTH] * 6 + [N_Q_HEADS * N_GATES]
    return [int(v) for v in np.cumsum(sizes)[:-1]], int(sum(sizes))


def _layer_norm(x, g, b, eps=1e-5):
    x32 = x.astype(jnp.float32)
    mu = jnp.mean(x32, -1, keepdims=True)
    var = jnp.mean(jnp.square(x32 - mu), -1, keepdims=True)
    return ((x32 - mu) * lax.rsqrt(var + eps) * g + b).astype(x.dtype)


def _head_rmsnorm(y, g, eps=1e-6):
    B, T, W = y.shape
    y32 = y.astype(jnp.float32).reshape(B, T, W // HEAD_DIM, HEAD_DIM)
    y32 = y32 * lax.rsqrt(jnp.mean(jnp.square(y32), -1, keepdims=True) + eps)
    return (y32.reshape(B, T, W) * g).astype(y.dtype)


def _causal_dwconv(z, w):
    K = w.shape[0]
    T = z.shape[1]
    zp = jnp.pad(z, ((0, 0), (K - 1, 0), (0, 0)))
    return sum(zp[:, k:k + T] * w[k] for k in range(K))


def _rope_tables(T):
    half = ROPE_DIM // 2
    inv_freq = ROPE_THETA ** (-(np.arange(half, dtype=np.float32) * 2.0 / ROPE_DIM))
    ang = jnp.arange(T, dtype=jnp.float32)[:, None] * jnp.asarray(inv_freq, jnp.float32)[None]
    return jnp.cos(ang)[:, None, :], jnp.sin(ang)[:, None, :]


def _partial_rope(x, cos, sin):
    half = ROPE_DIM // 2
    x1, x2, xp = x[..., :half], x[..., half:ROPE_DIM], x[..., ROPE_DIM:]
    rot = jnp.concatenate([x1 * cos - x2 * sin, x2 * cos + x1 * sin], -1).astype(x.dtype)
    return jnp.concatenate([rot, xp], -1)


def _masked_softmax(s, mask):
    s32 = jnp.where(mask, s.astype(jnp.float32), NEG)
    return jnp.where(mask, jax.nn.softmax(s32, axis=-1), 0.0)


def _slc_aggregation(T):
    nc = T // CMP_STRIDE - CMP_LEN // CMP_STRIDE + 1
    ns = T // SLC_LEN
    sc = np.arange(nc)[:, None] * CMP_STRIDE
    ss = np.arange(ns)[None, :] * SLC_LEN
    ov = np.clip(np.minimum(sc + CMP_LEN, ss + SLC_LEN) - np.maximum(sc, ss), 0, None)
    return jnp.asarray(ov / CMP_LEN, dtype=jnp.float32)


def _compress(a, pe, w1, b1, w2, b2):
    B, G, T, dk = a.shape
    r = CMP_LEN // CMP_STRIDE
    nsub = T // CMP_STRIDE
    nc = nsub - r + 1
    sub = a.reshape(B, G, nsub, CMP_STRIDE, dk)
    blocks = jnp.concatenate([sub[:, :, j:j + nc] for j in range(r)], axis=3)
    blocks = (blocks + pe).reshape(B, G, nc, CMP_LEN * dk)
    return jax.nn.gelu(blocks @ w1 + b1) @ w2 + b2


def _nsa(q, kc, vc, ks, vs, kw, vw, gates, pe, w1, b1, w2, b2):
    B, T, H, dk = q.shape
    G, R = N_KV_GROUPS, Q_PER_KV
    scale = dk ** -0.5
    qg = q.reshape(B, T, G, R, dk).transpose(0, 2, 3, 1, 4)
    to_g = lambda a: a.transpose(0, 2, 1, 3)
    k_cmp = _compress(to_g(kc), pe[0], w1[0], b1[0], w2[0], b2[0])
    v_cmp = _compress(to_g(vc), pe[1], w1[1], b1[1], w2[1], b2[1])
    nc = k_cmp.shape[2]
    cmp_end = jnp.arange(nc) * CMP_STRIDE + CMP_LEN - 1
    agg = _slc_aggregation(T)
    ns = T // SLC_LEN
    topk = min(SLC_TOPK, ns)
    kb_s = to_g(ks).reshape(B, G, ns, SLC_LEN, dk)
    vb_s = to_g(vs).reshape(B, G, ns, SLC_LEN, dk)
    pad = ((0, 0), (0, 0), (WINDOW, 0), (0, 0))
    kw_p = jnp.pad(to_g(kw), pad)
    vw_p = jnp.pad(to_g(vw), pad)
    blk = jnp.arange(ns)
    gather = jax.vmap(jax.vmap(lambda kb, ix: kb[ix]))

    def chunk(s):
        t = s + jnp.arange(Q_CHUNK)
        qc = lax.dynamic_slice_in_dim(qg, s, Q_CHUNK, axis=3) * scale
        sc = jnp.einsum('bgrqd,bgcd->bgrqc', qc, k_cmp)
        p_cmp = _masked_softmax(sc, cmp_end[None, :] <= t[:, None])
        o_cmp = jnp.einsum('bgrqc,bgcd->bgrqd', p_cmp, v_cmp)
        imp = jnp.einsum('bgrqc,cs->bgqs', p_cmp, agg)
        tb = t // SLC_LEN
        forced = (blk[None] == 0) | (blk[None] == tb[:, None]) | (blk[None] == tb[:, None] - 1)
        score = jnp.where(forced, BIG, imp)
        score = jnp.where(blk[None] <= tb[:, None], score, -BIG)
        _, idx = lax.top_k(score, topk)
        ks_sel = gather(kb_s, idx)
        vs_sel = gather(vb_s, idx).reshape(B, G, Q_CHUNK, topk * SLC_LEN, dk)
        ss = jnp.einsum('bgrqd,bgqnld->bgrqnl', qc, ks_sel).reshape(B, G, R, Q_CHUNK, topk * SLC_LEN)
        key_pos = idx[..., None] * SLC_LEN + jnp.arange(SLC_LEN)
        ms = (key_pos <= t[:, None, None]).reshape(B, G, 1, Q_CHUNK, topk * SLC_LEN)
        o_slc = jnp.einsum('bgrqm,bgqmd->bgrqd', _masked_softmax(ss, ms), vs_sel)
        kwin = lax.dynamic_slice_in_dim(kw_p, s, Q_CHUNK + WINDOW, axis=2)
        vwin = lax.dynamic_slice_in_dim(vw_p, s, Q_CHUNK + WINDOW, axis=2)
        wpos = s - WINDOW + jnp.arange(Q_CHUNK + WINDOW)
        mw = (wpos[None] <= t[:, None]) & (wpos[None] > t[:, None] - WINDOW) & (wpos[None] >= 0)
        sw = jnp.einsum('bgrqd,bgkd->bgrqk', qc, kwin)
        o_win = jnp.einsum('bgrqk,bgkd->bgrqd', _masked_softmax(sw, mw), vwin)
        return o_cmp, o_slc, o_win

    starts = jnp.arange(T // Q_CHUNK) * Q_CHUNK
    o_c, o_s, o_w = lax.map(chunk, starts)
    unchunk = lambda o: o.transpose(1, 0, 4, 2, 3, 5).reshape(B, T, H, dk)
    g = jax.nn.sigmoid(gates.astype(jnp.float32))
    out = g[..., 0:1] * unchunk(o_c) + g[..., 1:2] * unchunk(o_s) + g[..., 2:3] * unchunk(o_w)
    return out.astype(q.dtype).reshape(B, T, H * dk)


def setup_inputs(seed: int = 0) -> dict:
    key = jax.random.key(seed)
    ks = jax.random.split(key, 20)
    _, in_cols = _in_split_points()
    nrm = lambda k, shape, s: jax.random.normal(k, shape, jnp.float32) * s
    L = DEPTH
    return {
        'x': nrm(ks[0], (BATCH, SEQ, D_MODEL), 1.0),
        'w_in': nrm(ks[1], (L, D_MODEL, in_cols), D_MODEL ** -0.5),
        'short_conv_w': nrm(ks[2], (L, CONV_K, CONV_WIDTH), CONV_K ** -0.5),
        'cmp_pe': nrm(ks[3], (L, 2, CMP_LEN, HEAD_DIM), 0.02),
        'cmp_w1': nrm(ks[4], (L, 2, CMP_LEN * HEAD_DIM, CMP_HIDDEN), (CMP_LEN * HEAD_DIM) ** -0.5),
        'cmp_b1': nrm(ks[5], (L, 2, CMP_HIDDEN), 0.01),
        'cmp_w2': nrm(ks[6], (L, 2, CMP_HIDDEN, HEAD_DIM), CMP_HIDDEN ** -0.5),
        'cmp_b2': nrm(ks[7], (L, 2, HEAD_DIM), 0.01),
        'head_norm_g': 1.0 + nrm(ks[8], (L, D_MIX), 0.02),
        'w_out': nrm(ks[9], (L, D_MIX, D_MODEL), BETA * D_MIX ** -0.5),
        'ln1_g': 1.0 + nrm(ks[10], (L, D_MODEL), 0.02),
        'ln1_b': nrm(ks[11], (L, D_MODEL), 0.01),
        'w_up': nrm(ks[12], (L, D_MODEL, 2 * D_FF), D_MODEL ** -0.5),
        'ffn_conv_w': nrm(ks[13], (L, FFN_CONV_K, 2 * D_FF), FFN_CONV_K ** -0.5),
        'ffn_conv_b': nrm(ks[14], (L, 2 * D_FF), 0.01),
        'w_down': nrm(ks[15], (L, D_FF, D_MODEL), BETA * D_FF ** -0.5),
        'ln2_g': 1.0 + nrm(ks[16], (L, D_MODEL), 0.02),
        'ln2_b': nrm(ks[17], (L, D_MODEL), 0.01),
    }


def reference(x, w_in, short_conv_w, cmp_pe, cmp_w1, cmp_b1, cmp_w2, cmp_b2, head_norm_g,
              w_out, ln1_g, ln1_b, w_up, ffn_conv_w, ffn_conv_b, w_down, ln2_g, ln2_b):
    B, T, _ = x.shape
    split_points, _ = _in_split_points()
    cos, sin = _rope_tables(T)
    for i in range(DEPTH):
        proj = x @ w_in[i]
        (b_g, c_g, h, q, kc, vc, k_s, v_s, k_w, v_w, gates) = jnp.split(proj, split_points, axis=-1)
        conv_out = b_g * _causal_dwconv(c_g * h, short_conv_w[i])
        heads = lambda a, n: a.reshape(B, T, n, HEAD_DIM)
        q = _partial_rope(heads(q, N_Q_HEADS), cos, sin)
        kc = _partial_rope(heads(kc, N_KV_GROUPS), cos, sin)
        k_s = _partial_rope(heads(k_s, N_KV_GROUPS), cos, sin)
        k_w = _partial_rope(heads(k_w, N_KV_GROUPS), cos, sin)
        attn_out = _nsa(q, kc, heads(vc, N_KV_GROUPS), k_s, heads(v_s, N_KV_GROUPS), k_w,
                        heads(v_w, N_KV_GROUPS), gates.reshape(B, T, N_Q_HEADS, N_GATES),
                        cmp_pe[i], cmp_w1[i], cmp_b1[i], cmp_w2[i], cmp_b2[i])
        mixed = _head_rmsnorm(jnp.concatenate([conv_out, attn_out], -1), head_norm_g[i])
        x = _layer_norm(ALPHA * x + mixed @ w_out[i], ln1_g[i], ln1_b[i])
        u = _causal_dwconv(x @ w_up[i], ffn_conv_w[i]) + ffn_conv_b[i]
        val, gate = u[..., :D_FF], u[..., D_FF:]
        x = _layer_norm(ALPHA * x + (jax.nn.silu(gate) * val) @ w_down[i], ln2_g[i], ln2_b[i])
    return x
```

```python
import functools

import numpy as np
import jax
import jax.numpy as jnp
from jax import lax
from jax.experimental import pallas as pl
from jax.experimental.pallas import tpu as pltpu

D_MODEL = 1024
DEPTH = 2
HEAD_DIM = 64
CONV_WIDTH = 512
ATTN_WIDTH = 512
N_Q_HEADS = 8
N_KV_GROUPS = 2
Q_PER_KV = 4
KV_WIDTH = N_KV_GROUPS * HEAD_DIM
N_GATES = 3
CMP_LEN = 32
CMP_STRIDE = 16
CMP_HIDDEN = 256
SLC_LEN = 64
SLC_TOPK = 16
WINDOW = 512
ROPE_THETA = 500000.0
ROPE_DIM = 16
D_FF = 2816
ALPHA = (2.0 * DEPTH) ** 0.25
NEG = -1e30
BIG = 1e9

F32 = jnp.float32
MXU_DTYPE = jnp.bfloat16
LANES = 128
SUBLANES = 8
VMEM_LIMIT = 56 * 1024 * 1024

_OFF_B, _OFF_C, _OFF_H, _OFF_Q = 0, 512, 1024, 1536
_OFF_KC, _OFF_VC, _OFF_KS, _OFF_VS, _OFF_KW, _OFF_VW, _OFF_GATE = 2048, 2176, 2304, 2432, 2560, 2688, 2816
_IN_COLS = 2840
_IN_COLS_PAD = 2944

_NT = (((1,), (1,)), ((), ()))


def _dot(a, b):
    return jnp.dot(a, b, preferred_element_type=F32)


def _dot_nt(a, b):
    return lax.dot_general(a, b, _NT, preferred_element_type=F32)


def _layer_norm(y, g, b):
    mu = jnp.mean(y, axis=-1, keepdims=True)
    d = y - mu
    var = jnp.mean(d * d, axis=-1, keepdims=True)
    return d * lax.rsqrt(var + 1e-5) * g + b


def _group_rmsnorm(seg, g):
    ms = jnp.mean(seg * seg, axis=-1, keepdims=True)
    return seg * lax.rsqrt(ms + 1e-6) * g


def _inproj_kernel(x_ref, w_ref, cw_ref, cos_ref, sa_ref, sb_ref, hg_ref,
                   conv_ref, q_ref, cv_ref, ks_ref, vs_ref, kw_ref, vw_ref, gate_ref, zbuf):
    tm = x_ref.shape[0]
    xb = x_ref[...].astype(MXU_DTYPE)

    def mm(c0, width):
        return _dot(xb, w_ref[:, c0:c0 + width])

    @pl.when(pl.program_id(1) == 0)
    def _():
        zbuf[0:SUBLANES, :] = jnp.zeros((SUBLANES, CONV_WIDTH), F32)

    z = mm(_OFF_C, CONV_WIDTH) * mm(_OFF_H, CONV_WIDTH)
    zbuf[SUBLANES:SUBLANES + tm, :] = z
    conv = (cw_ref[2:3, :] * z + cw_ref[1:2, :] * zbuf[SUBLANES - 1:SUBLANES - 1 + tm, :]
            + cw_ref[0:1, :] * zbuf[SUBLANES - 2:SUBLANES - 2 + tm, :])
    zbuf[0:SUBLANES, :] = zbuf[tm:tm + SUBLANES, :]
    y = mm(_OFF_B, CONV_WIDTH) * conv
    for j in range(CONV_WIDTH // HEAD_DIM):
        sl = slice(j * HEAD_DIM, (j + 1) * HEAD_DIM)
        conv_ref[:, sl] = _group_rmsnorm(y[:, sl], hg_ref[:, sl]).astype(conv_ref.dtype)

    def rope(v):
        return (v * cos_ref[...] + pltpu.roll(v, LANES - ROPE_DIM // 2, 1) * sa_ref[...]
                + pltpu.roll(v, ROPE_DIM // 2, 1) * sb_ref[...])

    def split_heads(v, ref, lead):
        for g in range(N_KV_GROUPS):
            ref[lead + (g,)] = v[:, g * HEAD_DIM:(g + 1) * HEAD_DIM].astype(ref.dtype)

    scale = HEAD_DIM ** -0.5
    for c in range(ATTN_WIDTH // LANES):
        r = rope(mm(_OFF_Q + c * LANES, LANES)) * scale
        q_ref[2 * c] = r[:, :HEAD_DIM].astype(q_ref.dtype)
        q_ref[2 * c + 1] = r[:, HEAD_DIM:].astype(q_ref.dtype)
    split_heads(rope(mm(_OFF_KC, KV_WIDTH)), cv_ref, (0,))
    split_heads(mm(_OFF_VC, KV_WIDTH), cv_ref, (1,))
    split_heads(rope(mm(_OFF_KS, KV_WIDTH)), ks_ref, ())
    split_heads(mm(_OFF_VS, KV_WIDTH), vs_ref, ())
    split_heads(rope(mm(_OFF_KW, KV_WIDTH)), kw_ref, ())
    split_heads(mm(_OFF_VW, KV_WIDTH), vw_ref, ())
    sg = jax.nn.sigmoid(mm(_OFF_GATE, LANES))
    per_group = Q_PER_KV * N_GATES
    for g in range(N_KV_GROUPS):
        gate_ref[g] = sg[:, g * per_group:(g + 1) * per_group]


def _inproj(x, w_in_p, conv_w, cos_t, sa_t, sb_t, hg_conv, tm):
    B, T, D = x.shape
    kv_shape = jax.ShapeDtypeStruct((B, N_KV_GROUPS, T, HEAD_DIM), MXU_DTYPE)
    kv_spec = pl.BlockSpec((None, N_KV_GROUPS, tm, HEAD_DIM), lambda b, i: (b, 0, i, 0))
    tab_spec = pl.BlockSpec((tm, LANES), lambda b, i: (i, 0))
    const2 = lambda b, i: (0, 0)
    per_group = Q_PER_KV * N_GATES
    return pl.pallas_call(
        _inproj_kernel,
        grid=(B, T // tm),
        in_specs=[
            pl.BlockSpec((None, tm, D), lambda b, i: (b, i, 0)),
            pl.BlockSpec((D, _IN_COLS_PAD), const2),
            pl.BlockSpec((3, CONV_WIDTH), const2),
            tab_spec, tab_spec, tab_spec,
            pl.BlockSpec((1, CONV_WIDTH), const2),
        ],
        out_specs=[
            pl.BlockSpec((None, tm, CONV_WIDTH), lambda b, i: (b, i, 0)),
            pl.BlockSpec((None, N_Q_HEADS, tm, HEAD_DIM), lambda b, i: (b, 0, i, 0)),
            pl.BlockSpec((2, None, N_KV_GROUPS, tm, HEAD_DIM), lambda b, i: (0, b, 0, i, 0)),
            kv_spec, kv_spec, kv_spec, kv_spec,
            pl.BlockSpec((None, N_KV_GROUPS, tm, per_group), lambda b, i: (b, 0, i, 0)),
        ],
        out_shape=[
            jax.ShapeDtypeStruct((B, T, CONV_WIDTH), MXU_DTYPE),
            jax.ShapeDtypeStruct((B, N_Q_HEADS, T, HEAD_DIM), MXU_DTYPE),
            jax.ShapeDtypeStruct((2, B, N_KV_GROUPS, T, HEAD_DIM), MXU_DTYPE),
            kv_shape, kv_shape, kv_shape, kv_shape,
            jax.ShapeDtypeStruct((B, N_KV_GROUPS, T, per_group), F32),
        ],
        scratch_shapes=[pltpu.VMEM((tm + 2 * SUBLANES, CONV_WIDTH), F32)],
        compiler_params=pltpu.CompilerParams(
            dimension_semantics=("arbitrary", "arbitrary"), vmem_limit_bytes=VMEM_LIMIT),
    )(x, w_in_p, conv_w, cos_t, sa_t, sb_t, hg_conv)


def _compress_kernel(sub_ref, pe_ref, w1_ref, b1_ref, w2_ref, b2_ref, out_ref, bbuf):
    nsub = sub_ref.shape[0]
    half = CMP_STRIDE * HEAD_DIM
    sub = sub_ref[...]
    top = _dot(sub, w1_ref[0:half, :])
    bbuf[0:nsub, :] = _dot(sub, w1_ref[half:2 * half, :])
    bbuf[nsub:nsub + SUBLANES, :] = jnp.zeros((SUBLANES, CMP_HIDDEN), F32)
    pe_rows = jnp.broadcast_to(pe_ref[...], (SUBLANES, 2 * half)).astype(MXU_DTYPE)
    const = _dot(pe_rows, w1_ref[...])[0:1, :] + b1_ref[...]
    h = top + bbuf[1:nsub + 1, :] + const
    act = jax.nn.gelu(h)
    out_ref[...] = (_dot(act.astype(MXU_DTYPE), w2_ref[...]) + b2_ref[...]).astype(out_ref.dtype)


def _compress(sub, pe, w1, b1, w2, b2):
    _, BG, nsub, width = sub.shape
    sel = lambda k, n: (k, 0, 0)
    return pl.pallas_call(
        _compress_kernel,
        grid=(2, BG),
        in_specs=[
            pl.BlockSpec((None, None, nsub, width), lambda k, n: (k, n, 0, 0)),
            pl.BlockSpec((None, 1, 2 * width), sel),
            pl.BlockSpec((None, 2 * width, CMP_HIDDEN), sel),
            pl.BlockSpec((None, 1, CMP_HIDDEN), sel),
            pl.BlockSpec((None, CMP_HIDDEN, HEAD_DIM), sel),
            pl.BlockSpec((None, 1, HEAD_DIM), sel),
        ],
        out_specs=pl.BlockSpec((None, None, nsub, HEAD_DIM), lambda k, n: (k, n, 0, 0)),
        out_shape=jax.ShapeDtypeStruct((2, BG, nsub, HEAD_DIM), MXU_DTYPE),
        scratch_shapes=[pltpu.VMEM((nsub + SUBLANES, CMP_HIDDEN), F32)],
        compiler_params=pltpu.CompilerParams(
            dimension_semantics=("arbitrary", "arbitrary"), vmem_limit_bytes=VMEM_LIMIT),
    )(sub, pe, w1, b1, w2, b2)


def _topk_block_mask(score, topk):
    ns, tq = score.shape
    n_slabs = ns // SUBLANES
    slabs = [score[SUBLANES * j:SUBLANES * (j + 1), :] for j in range(n_slabs)]
    ranks = [jnp.zeros((SUBLANES, tq), F32) for _ in range(n_slabs)]
    local = lax.broadcasted_iota(jnp.int32, (SUBLANES, tq), 0)
    for i in range(ns):
        row = jnp.broadcast_to(score[i:i + 1, :], (SUBLANES, tq))
        for j in range(n_slabs):
            ge = jnp.where(row >= slabs[j], 1.0, 0.0)
            gt = jnp.where(row > slabs[j], 1.0, 0.0)
            if i < SUBLANES * j:
                ahead = ge
            elif i >= SUBLANES * (j + 1):
                ahead = gt
            else:
                ahead = jnp.where(local > i - SUBLANES * j, ge, gt)
            ranks[j] = ranks[j] + ahead
    rank = jnp.concatenate(ranks, axis=0)
    return jnp.where(rank < topk, 1.0, 0.0)


def _nsa_kernel(q_ref, kcmp_ref, vcmp_ref, ks_ref, vs_ref, kw_ref, vw_ref, gate_ref, aggt_ref, hg_ref,
                o_ref, *, tq, tk):
    R = Q_PER_KV
    T = ks_ref.shape[0]
    ns = T // SLC_LEN
    ncp = kcmp_ref.shape[0]
    q0 = pl.program_id(2) * tq
    q4 = q_ref[...].reshape(R * tq, HEAD_DIM)
    t_col = q0 + lax.broadcasted_iota(jnp.int32, (tq, 1), 0)

    s = _dot_nt(q4, kcmp_ref[...]).reshape(R, tq, ncp)
    cmp_end = lax.broadcasted_iota(jnp.int32, (tq, ncp), 1) * CMP_STRIDE + (CMP_LEN - 1)
    cmask = (cmp_end <= t_col)[None]
    s = jnp.where(cmask, s, NEG)
    p = jnp.where(cmask, jnp.exp(s - jnp.max(s, axis=-1, keepdims=True)), 0.0)
    l = jnp.sum(p, axis=-1, keepdims=True)
    p = p * jnp.where(l > 0.0, 1.0 / jnp.maximum(l, 1e-30), 0.0)
    o_cmp = _dot(p.reshape(R * tq, ncp).astype(MXU_DTYPE), vcmp_ref[...]).reshape(R, tq, HEAD_DIM)

    psum = p[0] + p[1] + p[2] + p[3]
    p_hi = psum.astype(MXU_DTYPE)
    p_lo = (psum - p_hi.astype(F32)).astype(MXU_DTYPE)
    imp = _dot_nt(aggt_ref[...], p_hi) + _dot_nt(aggt_ref[...], p_lo)
    blk = lax.broadcasted_iota(jnp.int32, (ns, tq), 0)
    tb = (q0 + lax.broadcasted_iota(jnp.int32, (ns, tq), 1)) // SLC_LEN
    forced = (blk == 0) | (blk == tb) | (blk == tb - 1)
    score = jnp.where(forced, BIG, imp)
    score = jnp.where(blk <= tb, score, -BIG)
    sel = _topk_block_mask(score, min(SLC_TOPK, ns)).T.astype(MXU_DTYPE)

    def sel_step(kt, carry):
        m, l, acc = carry
        k0 = pl.multiple_of(kt * tk, tk)
        s = _dot_nt(q4, ks_ref[pl.ds(k0, tk), :]).reshape(R, tq, tk)
        expand = jnp.where(
            lax.broadcasted_iota(jnp.int32, (ns, tk), 0)
            == (k0 + lax.broadcasted_iota(jnp.int32, (ns, tk), 1)) // SLC_LEN, 1.0, 0.0).astype(MXU_DTYPE)
        chosen = _dot(sel, expand)
        kpos = k0 + lax.broadcasted_iota(jnp.int32, (tq, tk), 1)
        bias = jnp.where((chosen > 0.5) & (kpos <= t_col), 0.0, NEG)
        s = s + bias[None]
        m_new = jnp.maximum(m, jnp.max(s, axis=-1, keepdims=True))
        alpha = jnp.exp(m - m_new)
        p = jnp.exp(s - m_new)
        l = alpha * l + jnp.sum(p, axis=-1, keepdims=True)
        pv = _dot(p.reshape(R * tq, tk).astype(MXU_DTYPE), vs_ref[pl.ds(k0, tk), :])
        return m_new, l, alpha * acc + pv.reshape(R, tq, HEAD_DIM)

    n_kt = (q0 + tq + tk - 1) // tk
    m, l, acc = lax.fori_loop(
        0, n_kt, sel_step,
        (jnp.full((R, tq, 1), NEG, F32), jnp.zeros((R, tq, 1), F32), jnp.zeros((R, tq, HEAD_DIM), F32)))
    o_slc = acc / l

    wl = WINDOW + tq
    ws = pl.multiple_of(jnp.maximum(q0 - WINDOW, 0), tq)
    s = _dot_nt(q4, kw_ref[pl.ds(ws, wl), :]).reshape(R, tq, wl)
    wpos = ws + lax.broadcasted_iota(jnp.int32, (tq, wl), 1)
    bias = jnp.where((wpos <= t_col) & (wpos > t_col - WINDOW), 0.0, NEG)
    s = s + bias[None]
    p = jnp.exp(s - jnp.max(s, axis=-1, keepdims=True))
    l = jnp.sum(p, axis=-1, keepdims=True)
    o_win = _dot(p.reshape(R * tq, wl).astype(MXU_DTYPE), vw_ref[pl.ds(ws, wl), :]).reshape(R, tq, HEAD_DIM) / l

    g = gate_ref[...]
    for r in range(R):
        c = N_GATES * r
        o = g[:, c:c + 1] * o_cmp[r] + g[:, c + 1:c + 2] * o_slc[r] + g[:, c + 2:c + 3] * o_win[r]
        sl = slice(r * HEAD_DIM, (r + 1) * HEAD_DIM)
        o_ref[:, sl] = _group_rmsnorm(o, hg_ref[:, sl]).astype(o_ref.dtype)


def _nsa(q, cmp_kv, ks, vs, kw, vw, gates, agg_t, hg_attn, tq, tk):
    B, _, T, _ = q.shape
    G, R = N_KV_GROUPS, Q_PER_KV
    ncp = cmp_kv.shape[2]
    ns = T // SLC_LEN
    assert T % tq == 0 and T % tk == 0 and WINDOW % tq == 0 and T >= WINDOW + tq
    kv_spec = pl.BlockSpec((None, None, T, HEAD_DIM), lambda b, g, i: (b, g, 0, 0))
    return pl.pallas_call(
        functools.partial(_nsa_kernel, tq=tq, tk=tk),
        grid=(B, G, T // tq),
        in_specs=[
            pl.BlockSpec((None, R, tq, HEAD_DIM), lambda b, g, i: (b, g, i, 0)),
            pl.BlockSpec((None, None, ncp, HEAD_DIM), lambda b, g, i: (0, b * G + g, 0, 0)),
            pl.BlockSpec((None, None, ncp, HEAD_DIM), lambda b, g, i: (1, b * G + g, 0, 0)),
            kv_spec, kv_spec, kv_spec, kv_spec,
            pl.BlockSpec((None, None, tq, R * N_GATES), lambda b, g, i: (b, g, i, 0)),
            pl.BlockSpec((ns, ncp), lambda b, g, i: (0, 0)),
            pl.BlockSpec((None, 1, R * HEAD_DIM), lambda b, g, i: (g, 0, 0)),
        ],
        out_specs=pl.BlockSpec((None, tq, R * HEAD_DIM), lambda b, g, i: (b, i, g)),
        out_shape=jax.ShapeDtypeStruct((B, T, ATTN_WIDTH), MXU_DTYPE),
        compiler_params=pltpu.CompilerParams(
            dimension_semantics=("arbitrary", "arbitrary", "arbitrary"), vmem_limit_bytes=VMEM_LIMIT),
    )(q, cmp_kv, cmp_kv, ks, vs, kw, vw, gates, agg_t, hg_attn)


def _outproj_kernel(conv_ref, attn_ref, x_ref, w_ref, g_ref, b_ref, o_ref):
    mix = _dot(conv_ref[...], w_ref[0:CONV_WIDTH, :]) + _dot(attn_ref[...], w_ref[CONV_WIDTH:, :])
    o_ref[...] = _layer_norm(ALPHA * x_ref[...] + mix, g_ref[...], b_ref[...])


def _outproj(conv_n, attn_n, x, w_out, g, b, tm):
    N, D = x.shape
    row = lambda i: (i, 0)
    const = lambda i: (0, 0)
    return pl.pallas_call(
        _outproj_kernel,
        grid=(N // tm,),
        in_specs=[
            pl.BlockSpec((tm, CONV_WIDTH), row), pl.BlockSpec((tm, ATTN_WIDTH), row),
            pl.BlockSpec((tm, D), row), pl.BlockSpec((D, D), const),
            pl.BlockSpec((1, D), const), pl.BlockSpec((1, D), const),
        ],
        out_specs=pl.BlockSpec((tm, D), row),
        out_shape=jax.ShapeDtypeStruct((N, D), F32),
        compiler_params=pltpu.CompilerParams(
            dimension_semantics=("arbitrary",), vmem_limit_bytes=VMEM_LIMIT),
    )(conv_n, attn_n, x, w_out, g, b)


def _ffn_kernel(x_ref, halo_ref, wup_ref, cw_ref, cb_ref, wdn_ref, g_ref, b_ref, o_ref, ubuf, acc_ref,
                *, tiles_per_seq, fc):
    tm = x_ref.shape[0]
    x = x_ref[...]
    first = (pl.program_id(0) % tiles_per_seq) == 0
    halo = jnp.where(first, 0.0, halo_ref[...])
    xb = jnp.concatenate([halo, x], axis=0).astype(MXU_DTYPE)

    def up_conv(col0):
        ubuf[...] = _dot(xb, wup_ref[:, col0:col0 + fc])
        return (cw_ref[2:3, col0:col0 + fc] * ubuf[SUBLANES:SUBLANES + tm, :]
                + cw_ref[1:2, col0:col0 + fc] * ubuf[SUBLANES - 1:SUBLANES - 1 + tm, :]
                + cw_ref[0:1, col0:col0 + fc] * ubuf[SUBLANES - 2:SUBLANES - 2 + tm, :]
                + cb_ref[:, col0:col0 + fc])

    for c in range(D_FF // fc):
        val = up_conv(c * fc)
        gate = up_conv(D_FF + c * fc)
        act = (gate * jax.nn.sigmoid(gate) * val).astype(MXU_DTYPE)
        part = _dot(act, wdn_ref[c * fc:(c + 1) * fc, :])
        if c == 0:
            acc_ref[...] = part
        else:
            acc_ref[...] += part
    o_ref[...] = _layer_norm(ALPHA * x + acc_ref[...], g_ref[...], b_ref[...])


def _ffn(x, w_up, conv_w, conv_b, w_down, g, b, tm, tiles_per_seq, fc):
    N, D = x.shape
    row = lambda i: (i, 0)
    const = lambda i: (0, 0)
    halo_blocks = tm // SUBLANES
    return pl.pallas_call(
        functools.partial(_ffn_kernel, tiles_per_seq=tiles_per_seq, fc=fc),
        grid=(N // tm,),
        in_specs=[
            pl.BlockSpec((tm, D), row),
            pl.BlockSpec((SUBLANES, D), lambda i: (jnp.maximum(i * halo_blocks - 1, 0), 0)),
            pl.BlockSpec((D, 2 * D_FF), const, pipeline_mode=pl.Buffered(1)),
            pl.BlockSpec((3, 2 * D_FF), const),
            pl.BlockSpec((1, 2 * D_FF), const),
            pl.BlockSpec((D_FF, D), const, pipeline_mode=pl.Buffered(1)),
            pl.BlockSpec((1, D), const), pl.BlockSpec((1, D), const),
        ],
        out_specs=pl.BlockSpec((tm, D), row),
        out_shape=jax.ShapeDtypeStruct((N, D), F32),
        scratch_shapes=[pltpu.VMEM((tm + SUBLANES, fc), F32), pltpu.VMEM((tm, D), F32)],
        compiler_params=pltpu.CompilerParams(
            dimension_semantics=("arbitrary",), vmem_limit_bytes=VMEM_LIMIT),
    )(x, x, w_up, conv_w, conv_b, w_down, g, b)


def _rope_lane_tables(T):
    half = ROPE_DIM // 2
    inv_freq = ROPE_THETA ** (-(np.arange(half, dtype=np.float32) * 2.0 / ROPE_DIM))
    ang = jnp.arange(T, dtype=F32)[:, None] * jnp.asarray(inv_freq, F32)[None]
    cos, sin = jnp.cos(ang), jnp.sin(ang)
    rest = HEAD_DIM - ROPE_DIM
    one, zero = jnp.ones((T, rest), F32), jnp.zeros((T, rest), F32)
    zh = jnp.zeros((T, half), F32)
    cos_t = jnp.concatenate([cos, cos, one], -1)
    sa_t = jnp.concatenate([-sin, zh, zero], -1)
    sb_t = jnp.concatenate([zh, sin, zero], -1)
    rep = LANES // HEAD_DIM
    return tuple(jnp.tile(t, (1, rep)) for t in (cos_t, sa_t, sb_t))


def _slc_aggregation_t(T):
    nc = T // CMP_STRIDE - CMP_LEN // CMP_STRIDE + 1
    ns = T // SLC_LEN
    sc = np.arange(nc)[None, :] * CMP_STRIDE
    ss = np.arange(ns)[:, None] * SLC_LEN
    ov = np.clip(np.minimum(sc + CMP_LEN, ss + SLC_LEN) - np.maximum(sc, ss), 0, None) / CMP_LEN
    ov = np.pad(ov, ((0, 0), (0, T // CMP_STRIDE - nc)))
    return jnp.asarray(ov, dtype=MXU_DTYPE)


def kernel(x, w_in, short_conv_w, cmp_pe, cmp_w1, cmp_b1, cmp_w2, cmp_b2, head_norm_g, w_out, ln1_g, ln1_b,
           w_up, ffn_conv_w, ffn_conv_b, w_down, ln2_g, ln2_b):
    B, T, D = x.shape
    tm = min(512, T)
    cos_t, sa_t, sb_t = _rope_lane_tables(T)
    agg_t = _slc_aggregation_t(T)
    nsub = T // CMP_STRIDE
    for i in range(DEPTH):
        w_in_p = jnp.pad(w_in[i], ((0, 0), (0, _IN_COLS_PAD - _IN_COLS))).astype(MXU_DTYPE)
        hg = head_norm_g[i]
        conv_n, q, cv, ks, vs, kw, vw, gates = _inproj(
            x, w_in_p, short_conv_w[i], cos_t, sa_t, sb_t, hg[None, :CONV_WIDTH], tm)
        sub = cv.reshape(2, B * N_KV_GROUPS, nsub, CMP_STRIDE * HEAD_DIM)
        cmp_kv = _compress(
            sub, cmp_pe[i].reshape(2, 1, CMP_LEN * HEAD_DIM), cmp_w1[i].astype(MXU_DTYPE),
            cmp_b1[i][:, None, :], cmp_w2[i].astype(MXU_DTYPE), cmp_b2[i][:, None, :])
        attn_n = _nsa(q, cmp_kv, ks, vs, kw, vw, gates, agg_t,
                      hg[CONV_WIDTH:].reshape(N_KV_GROUPS, 1, Q_PER_KV * HEAD_DIM), tq=128, tk=256)
        x2 = _outproj(conv_n.reshape(B * T, CONV_WIDTH), attn_n.reshape(B * T, ATTN_WIDTH),
                      x.reshape(B * T, D), w_out[i].astype(MXU_DTYPE), ln1_g[i][None], ln1_b[i][None], tm)
        x2 = _ffn(x2, w_up[i].astype(MXU_DTYPE), ffn_conv_w[i], ffn_conv_b[i][None],
                  w_down[i].astype(MXU_DTYPE), ln2_g[i][None], ln2_b[i][None], tm, T // tm, 256)
        x = x2.reshape(B, T, D)
    return x
```

```python
import functools

import numpy as np
import jax
import jax.numpy as jnp
from jax import lax
from jax.experimental import pallas as pl
from jax.experimental.pallas import tpu as pltpu

D_MODEL = 1024
DEPTH = 2
HEAD_DIM = 64
CONV_WIDTH = 512
ATTN_WIDTH = 512
N_Q_HEADS = 8
N_KV_GROUPS = 2
Q_PER_KV = 4
KV_WIDTH = N_KV_GROUPS * HEAD_DIM
N_GATES = 3
CMP_LEN = 32
CMP_STRIDE = 16
CMP_HIDDEN = 256
SLC_LEN = 64
SLC_TOPK = 16
WINDOW = 512
ROPE_THETA = 500000.0
ROPE_DIM = 16
D_FF = 2816
ALPHA = (2.0 * DEPTH) ** 0.25
NEG = -1e30
BIG = 1e9

F32 = jnp.float32
MXU_DTYPE = jnp.bfloat16
LANES = 128
SUBLANES = 8
VMEM_LIMIT = 56 * 1024 * 1024

_OFF_B, _OFF_C, _OFF_H, _OFF_Q = 0, 512, 1024, 1536
_OFF_KC, _OFF_VC, _OFF_KS, _OFF_VS, _OFF_KW, _OFF_VW, _OFF_GATE = 2048, 2176, 2304, 2432, 2560, 2688, 2816
_IN_COLS = 2840
_IN_COLS_PAD = 2944

_NT = (((1,), (1,)), ((), ()))


def _dot(a, b):
    return jnp.dot(a, b, preferred_element_type=F32)


def _dot_nt(a, b):
    return lax.dot_general(a, b, _NT, preferred_element_type=F32)


def _layer_norm(y, g, b):
    mu = jnp.mean(y, axis=-1, keepdims=True)
    d = y - mu
    var = jnp.mean(d * d, axis=-1, keepdims=True)
    return d * lax.rsqrt(var + 1e-5) * g + b


def _group_rmsnorm(seg, g):
    ms = jnp.mean(seg * seg, axis=-1, keepdims=True)
    return seg * lax.rsqrt(ms + 1e-6) * g


def _inproj_kernel(x_ref, w_ref, cw_ref, cos_ref, sa_ref, sb_ref, hg_ref,
                   conv_ref, q_ref, cv_ref, ks_ref, vs_ref, kw_ref, vw_ref, gate_ref, zbuf):
    tm = x_ref.shape[0]
    xb = x_ref[...].astype(MXU_DTYPE)

    def mm(c0, width):
        return _dot(xb, w_ref[:, c0:c0 + width])

    @pl.when(pl.program_id(1) == 0)
    def _():
        zbuf[0:SUBLANES, :] = jnp.zeros((SUBLANES, CONV_WIDTH), F32)

    z = mm(_OFF_C, CONV_WIDTH) * mm(_OFF_H, CONV_WIDTH)
    zbuf[SUBLANES:SUBLANES + tm, :] = z
    conv = (cw_ref[2:3, :] * z + cw_ref[1:2, :] * zbuf[SUBLANES - 1:SUBLANES - 1 + tm, :]
            + cw_ref[0:1, :] * zbuf[SUBLANES - 2:SUBLANES - 2 + tm, :])
    zbuf[0:SUBLANES, :] = zbuf[tm:tm + SUBLANES, :]
    y = mm(_OFF_B, CONV_WIDTH) * conv
    for j in range(CONV_WIDTH // HEAD_DIM):
        sl = slice(j * HEAD_DIM, (j + 1) * HEAD_DIM)
        conv_ref[:, sl] = _group_rmsnorm(y[:, sl], hg_ref[:, sl]).astype(conv_ref.dtype)

    def rope(v):
        return (v * cos_ref[...] + pltpu.roll(v, LANES - ROPE_DIM // 2, 1) * sa_ref[...]
                + pltpu.roll(v, ROPE_DIM // 2, 1) * sb_ref[...])

    def split_heads(v, ref, lead):
        for g in range(N_KV_GROUPS):
            ref[lead + (g,)] = v[:, g * HEAD_DIM:(g + 1) * HEAD_DIM].astype(ref.dtype)

    lane = lax.broadcasted_iota(jnp.int32, (tm, LANES), 1)
    low = lane < HEAD_DIM

    def head_pair(v):
        return v, pltpu.roll(v, HEAD_DIM, 1)

    scale = HEAD_DIM ** -0.5
    for c in range(ATTN_WIDTH // LANES):
        r = rope(mm(_OFF_Q + c * LANES, LANES)) * scale
        for j, h in enumerate(head_pair(r)):
            q_ref[2 * c + j] = jnp.where(low, h, 0.0).astype(q_ref.dtype)
    split_heads(rope(mm(_OFF_KC, KV_WIDTH)), cv_ref, (0,))
    split_heads(mm(_OFF_VC, KV_WIDTH), cv_ref, (1,))
    pos = pl.program_id(1) * tm + lax.broadcasted_iota(jnp.int32, (tm, LANES), 0)
    block_tag = jnp.where(pos // SLC_LEN == lane - HEAD_DIM, NEG, 0.0)
    for g, h in enumerate(head_pair(rope(mm(_OFF_KS, KV_WIDTH)))):
        ks_ref[g] = jnp.where(low, h, block_tag).astype(ks_ref.dtype)
    ones_col = jnp.where(lane == HEAD_DIM, 1.0, 0.0)
    for g, h in enumerate(head_pair(mm(_OFF_VS, KV_WIDTH))):
        vs_ref[g] = jnp.where(low, h, ones_col).astype(vs_ref.dtype)
    split_heads(rope(mm(_OFF_KW, KV_WIDTH)), kw_ref, ())
    for g, h in enumerate(head_pair(mm(_OFF_VW, KV_WIDTH))):
        vw_ref[g] = jnp.where(low, h, ones_col).astype(vw_ref.dtype)
    sg = jax.nn.sigmoid(mm(_OFF_GATE, LANES))
    per_group = Q_PER_KV * N_GATES
    for g in range(N_KV_GROUPS):
        gate_ref[g] = sg[:, g * per_group:(g + 1) * per_group]


def _inproj(x, w_in_p, conv_w, cos_t, sa_t, sb_t, hg_conv, tm):
    B, T, D = x.shape
    kv_shape = jax.ShapeDtypeStruct((B, N_KV_GROUPS, T, HEAD_DIM), MXU_DTYPE)
    kv_spec = pl.BlockSpec((None, N_KV_GROUPS, tm, HEAD_DIM), lambda b, i: (b, 0, i, 0))
    wide_shape = jax.ShapeDtypeStruct((B, N_KV_GROUPS, T, LANES), MXU_DTYPE)
    wide_spec = pl.BlockSpec((None, N_KV_GROUPS, tm, LANES), lambda b, i: (b, 0, i, 0))
    tab_spec = pl.BlockSpec((tm, LANES), lambda b, i: (i, 0))
    const2 = lambda b, i: (0, 0)
    per_group = Q_PER_KV * N_GATES
    return pl.pallas_call(
        _inproj_kernel,
        grid=(B, T // tm),
        in_specs=[
            pl.BlockSpec((None, tm, D), lambda b, i: (b, i, 0)),
            pl.BlockSpec((D, _IN_COLS_PAD), const2),
            pl.BlockSpec((3, CONV_WIDTH), const2),
            tab_spec, tab_spec, tab_spec,
            pl.BlockSpec((1, CONV_WIDTH), const2),
        ],
        out_specs=[
            pl.BlockSpec((None, tm, CONV_WIDTH), lambda b, i: (b, i, 0)),
            pl.BlockSpec((None, N_Q_HEADS, tm, LANES), lambda b, i: (b, 0, i, 0)),
            pl.BlockSpec((2, None, N_KV_GROUPS, tm, HEAD_DIM), lambda b, i: (0, b, 0, i, 0)),
            wide_spec, wide_spec, kv_spec, wide_spec,
            pl.BlockSpec((None, N_KV_GROUPS, tm, per_group), lambda b, i: (b, 0, i, 0)),
        ],
        out_shape=[
            jax.ShapeDtypeStruct((B, T, CONV_WIDTH), MXU_DTYPE),
            jax.ShapeDtypeStruct((B, N_Q_HEADS, T, LANES), MXU_DTYPE),
            jax.ShapeDtypeStruct((2, B, N_KV_GROUPS, T, HEAD_DIM), MXU_DTYPE),
            wide_shape, wide_shape, kv_shape, wide_shape,
            jax.ShapeDtypeStruct((B, N_KV_GROUPS, T, per_group), F32),
        ],
        scratch_shapes=[pltpu.VMEM((tm + 2 * SUBLANES, CONV_WIDTH), F32)],
        compiler_params=pltpu.CompilerParams(
            dimension_semantics=("arbitrary", "arbitrary"), vmem_limit_bytes=VMEM_LIMIT),
    )(x, w_in_p, conv_w, cos_t, sa_t, sb_t, hg_conv)


def _compress_kernel(sub_ref, pe_ref, w1_ref, b1_ref, w2_ref, b2_ref, out_ref, bbuf):
    nsub = sub_ref.shape[0]
    half = CMP_STRIDE * HEAD_DIM
    sub = sub_ref[...]
    top = _dot(sub, w1_ref[0:half, :])
    bbuf[0:nsub, :] = _dot(sub, w1_ref[half:2 * half, :])
    bbuf[nsub:nsub + SUBLANES, :] = jnp.zeros((SUBLANES, CMP_HIDDEN), F32)
    pe_rows = jnp.broadcast_to(pe_ref[...], (SUBLANES, 2 * half)).astype(MXU_DTYPE)
    const = _dot(pe_rows, w1_ref[...])[0:1, :] + b1_ref[...]
    h = top + bbuf[1:nsub + 1, :] + const
    act = jax.nn.gelu(h)
    out_ref[...] = (_dot(act.astype(MXU_DTYPE), w2_ref[...]) + b2_ref[...]).astype(out_ref.dtype)


def _compress(sub, pe, w1, b1, w2, b2):
    _, BG, nsub, width = sub.shape
    sel = lambda k, n: (k, 0, 0)
    return pl.pallas_call(
        _compress_kernel,
        grid=(2, BG),
        in_specs=[
            pl.BlockSpec((None, None, nsub, width), lambda k, n: (k, n, 0, 0)),
            pl.BlockSpec((None, 1, 2 * width), sel),
            pl.BlockSpec((None, 2 * width, CMP_HIDDEN), sel),
            pl.BlockSpec((None, 1, CMP_HIDDEN), sel),
            pl.BlockSpec((None, CMP_HIDDEN, HEAD_DIM), sel),
            pl.BlockSpec((None, 1, HEAD_DIM), sel),
        ],
        out_specs=pl.BlockSpec((None, None, nsub, HEAD_DIM), lambda k, n: (k, n, 0, 0)),
        out_shape=jax.ShapeDtypeStruct((2, BG, nsub, HEAD_DIM), MXU_DTYPE),
        scratch_shapes=[pltpu.VMEM((nsub + SUBLANES, CMP_HIDDEN), F32)],
        compiler_params=pltpu.CompilerParams(
            dimension_semantics=("arbitrary", "arbitrary"), vmem_limit_bytes=VMEM_LIMIT),
    )(sub, pe, w1, b1, w2, b2)


def _topk_block_mask(score, topk):
    ns, tq = score.shape
    n_slabs = ns // SUBLANES
    slabs = [score[SUBLANES * j:SUBLANES * (j + 1), :] for j in range(n_slabs)]
    ranks = [jnp.zeros((SUBLANES, tq), F32) for _ in range(n_slabs)]
    local = lax.broadcasted_iota(jnp.int32, (SUBLANES, tq), 0)
    for i in range(ns):
        row = jnp.broadcast_to(score[i:i + 1, :], (SUBLANES, tq))
        for j in range(n_slabs):
            ge = jnp.where(row >= slabs[j], 1.0, 0.0)
            gt = jnp.where(row > slabs[j], 1.0, 0.0)
            if i < SUBLANES * j:
                ahead = ge
            elif i >= SUBLANES * (j + 1):
                ahead = gt
            else:
                ahead = jnp.where(local > i - SUBLANES * j, ge, gt)
            ranks[j] = ranks[j] + ahead
    rank = jnp.concatenate(ranks, axis=0)
    return jnp.where(rank < topk, 1.0, 0.0)


def _nsa_kernel(q_ref, kcmp_ref, vcmp_ref, ks_ref, vs_ref, kw_ref, vw_ref, gate_ref, aggt_ref, hg_ref,
                o_ref, mx_ref, acc_ref, *, tq, kc):
    R = Q_PER_KV
    rows = R * tq
    T = ks_ref.shape[0]
    ns = T // SLC_LEN
    ncp = kcmp_ref.shape[0]
    q0 = pl.program_id(2) * tq
    t_col = q0 + lax.broadcasted_iota(jnp.int32, (tq, 1), 0)
    q_pad = q_ref[...]
    q64 = q_pad.reshape(rows, LANES)[:, :HEAD_DIM]

    def per_head(a):
        return a.reshape(R, tq, a.shape[-1])

    cmp_end = lax.broadcasted_iota(jnp.int32, (tq, ncp), 1) * CMP_STRIDE + (CMP_LEN - 1)
    cbias = jnp.where(cmp_end <= t_col, 0.0, NEG)
    s = per_head(_dot_nt(q64, kcmp_ref[...])) + cbias[None]
    p = jnp.exp(s - jnp.max(s, axis=-1, keepdims=True))
    l = jnp.sum(p, axis=-1, keepdims=True)
    p = p * jnp.where((t_col >= CMP_LEN - 1)[None], 1.0 / l, 0.0)
    o_cmp = _dot(p.reshape(rows, ncp).astype(MXU_DTYPE), vcmp_ref[...])

    psum = p[0] + p[1] + p[2] + p[3]
    p_hi = psum.astype(MXU_DTYPE)
    p_lo = (psum - p_hi.astype(F32)).astype(MXU_DTYPE)
    imp = _dot_nt(aggt_ref[...], p_hi) + _dot_nt(aggt_ref[...], p_lo)
    blk = lax.broadcasted_iota(jnp.int32, (ns, tq), 0)
    tb = (q0 + lax.broadcasted_iota(jnp.int32, (ns, tq), 1)) // SLC_LEN
    forced = (blk == 0) | (blk == tb) | (blk == tb - 1)
    score = jnp.where(forced, BIG, imp)
    score = jnp.where(blk <= tb, score, -BIG)
    unselected = 1.0 - _topk_block_mask(score, min(SLC_TOPK, ns))
    tag_rows = [jnp.zeros((HEAD_DIM, tq), F32), unselected]
    if ns < LANES - HEAD_DIM:
        tag_rows.append(jnp.zeros((LANES - HEAD_DIM - ns, tq), F32))
    tag = jnp.concatenate(tag_rows, axis=0).T.astype(MXU_DTYPE)
    lane = lax.broadcasted_iota(jnp.int32, (R, tq, LANES), 2)
    q_aug = jnp.where(lane < HEAD_DIM, q_pad, tag[None]).reshape(rows, LANES)

    def scores(k0):
        return _dot_nt(q_aug, ks_ref[pl.ds(k0, kc), :])

    def lane_fold_max(a):
        out = a[:, :LANES]
        for j in range(1, a.shape[1] // LANES):
            out = jnp.maximum(out, a[:, j * LANES:(j + 1) * LANES])
        return out

    n_full = q0 // kc
    mx_ref[...] = jnp.full((rows, LANES), NEG, F32)

    def max_pass(c, carry):
        mx_ref[...] = jnp.maximum(mx_ref[...], lane_fold_max(scores(pl.multiple_of(c * kc, kc))))
        return carry

    lax.fori_loop(0, n_full, max_pass, 0)
    kd0 = pl.multiple_of(n_full * kc, kc)
    kpos = kd0 + lax.broadcasted_iota(jnp.int32, (tq, kc), 1)
    s_diag = (per_head(scores(kd0)) + jnp.where(kpos <= t_col, 0.0, NEG)[None]).reshape(rows, kc)
    m = jnp.max(jnp.maximum(mx_ref[...], lane_fold_max(s_diag)), axis=-1, keepdims=True)

    acc_ref[...] = _dot(jnp.exp(s_diag - m).astype(MXU_DTYPE), vs_ref[pl.ds(kd0, kc), :])

    def sum_pass(c, carry):
        k0 = pl.multiple_of(c * kc, kc)
        acc_ref[...] += _dot(jnp.exp(scores(k0) - m).astype(MXU_DTYPE), vs_ref[pl.ds(k0, kc), :])
        return carry

    lax.fori_loop(0, n_full, sum_pass, 0)
    acc = acc_ref[...]
    o_slc = acc[:, :HEAD_DIM] / acc[:, HEAD_DIM:HEAD_DIM + 1]

    wl = WINDOW + tq
    ws = pl.multiple_of(jnp.maximum(q0 - WINDOW, 0), tq)
    wpos = ws + lax.broadcasted_iota(jnp.int32, (tq, wl), 1)
    wbias = jnp.where((wpos <= t_col) & (wpos > t_col - WINDOW), 0.0, NEG)
    s = per_head(_dot_nt(q64, kw_ref[pl.ds(ws, wl), :])) + wbias[None]
    p = jnp.exp(s - jnp.max(s, axis=-1, keepdims=True))
    ow = _dot(p.reshape(rows, wl).astype(MXU_DTYPE), vw_ref[pl.ds(ws, wl), :])
    o_win = ow[:, :HEAD_DIM] / ow[:, HEAD_DIM:HEAD_DIM + 1]

    g = gate_ref[...]
    for r in range(R):
        c = N_GATES * r
        hr = slice(r * tq, (r + 1) * tq)
        o = g[:, c:c + 1] * o_cmp[hr] + g[:, c + 1:c + 2] * o_slc[hr] + g[:, c + 2:c + 3] * o_win[hr]
        sl = slice(r * HEAD_DIM, (r + 1) * HEAD_DIM)
        o_ref[:, sl] = _group_rmsnorm(o, hg_ref[:, sl]).astype(o_ref.dtype)


def _nsa(q, cmp_kv, ks, vs, kw, vw, gates, agg_t, hg_attn, tq, kc):
    B, _, T, _ = q.shape
    G, R = N_KV_GROUPS, Q_PER_KV
    ncp = cmp_kv.shape[2]
    ns = T // SLC_LEN
    assert T % kc == 0 and kc % tq == 0 and WINDOW % tq == 0 and T >= WINDOW + tq
    assert ns <= LANES - HEAD_DIM, "block tags must fit in the upper lanes of a key row"
    kv_spec = pl.BlockSpec((None, None, T, HEAD_DIM), lambda b, g, i: (b, g, 0, 0))
    wide_spec = pl.BlockSpec((None, None, T, LANES), lambda b, g, i: (b, g, 0, 0))
    return pl.pallas_call(
        functools.partial(_nsa_kernel, tq=tq, kc=kc),
        grid=(B, G, T // tq),
        in_specs=[
            pl.BlockSpec((None, R, tq, LANES), lambda b, g, i: (b, g, i, 0)),
            pl.BlockSpec((None, None, ncp, HEAD_DIM), lambda b, g, i: (0, b * G + g, 0, 0)),
            pl.BlockSpec((None, None, ncp, HEAD_DIM), lambda b, g, i: (1, b * G + g, 0, 0)),
            wide_spec, wide_spec, kv_spec, wide_spec,
            pl.BlockSpec((None, None, tq, R * N_GATES), lambda b, g, i: (b, g, i, 0)),
            pl.BlockSpec((ns, ncp), lambda b, g, i: (0, 0)),
            pl.BlockSpec((None, 1, R * HEAD_DIM), lambda b, g, i: (g, 0, 0)),
        ],
        out_specs=pl.BlockSpec((None, tq, R * HEAD_DIM), lambda b, g, i: (b, i, g)),
        out_shape=jax.ShapeDtypeStruct((B, T, ATTN_WIDTH), MXU_DTYPE),
        scratch_shapes=[pltpu.VMEM((R * tq, LANES), F32), pltpu.VMEM((R * tq, LANES), F32)],
        compiler_params=pltpu.CompilerParams(
            dimension_semantics=("arbitrary", "arbitrary", "arbitrary"), vmem_limit_bytes=VMEM_LIMIT),
    )(q, cmp_kv, cmp_kv, ks, vs, kw, vw, gates, agg_t, hg_attn)


def _outproj_kernel(conv_ref, attn_ref, x_ref, w_ref, g_ref, b_ref, o_ref):
    mix = _dot(conv_ref[...], w_ref[0:CONV_WIDTH, :]) + _dot(attn_ref[...], w_ref[CONV_WIDTH:, :])
    o_ref[...] = _layer_norm(ALPHA * x_ref[...] + mix, g_ref[...], b_ref[...])


def _outproj(conv_n, attn_n, x, w_out, g, b, tm):
    N, D = x.shape
    row = lambda i: (i, 0)
    const = lambda i: (0, 0)
    return pl.pallas_call(
        _outproj_kernel,
        grid=(N // tm,),
        in_specs=[
            pl.BlockSpec((tm, CONV_WIDTH), row), pl.BlockSpec((tm, ATTN_WIDTH), row),
            pl.BlockSpec((tm, D), row), pl.BlockSpec((D, D), const),
            pl.BlockSpec((1, D), const), pl.BlockSpec((1, D), const),
        ],
        out_specs=pl.BlockSpec((tm, D), row),
        out_shape=jax.ShapeDtypeStruct((N, D), F32),
        compiler_params=pltpu.CompilerParams(
            dimension_semantics=("arbitrary",), vmem_limit_bytes=VMEM_LIMIT),
    )(conv_n, attn_n, x, w_out, g, b)


def _ffn_kernel(x_ref, halo_ref, wup_ref, cw_ref, cb_ref, wdn_ref, g_ref, b_ref, o_ref, ubuf, acc_ref,
                *, tiles_per_seq, fc):
    tm = x_ref.shape[0]
    x = x_ref[...]
    first = (pl.program_id(0) % tiles_per_seq) == 0
    halo = jnp.where(first, 0.0, halo_ref[...])
    xb = jnp.concatenate([halo, x], axis=0).astype(MXU_DTYPE)

    def up_conv(col0):
        ubuf[...] = _dot(xb, wup_ref[:, col0:col0 + fc])
        return (cw_ref[2:3, col0:col0 + fc] * ubuf[SUBLANES:SUBLANES + tm, :]
                + cw_ref[1:2, col0:col0 + fc] * ubuf[SUBLANES - 1:SUBLANES - 1 + tm, :]
                + cw_ref[0:1, col0:col0 + fc] * ubuf[SUBLANES - 2:SUBLANES - 2 + tm, :]
                + cb_ref[:, col0:col0 + fc])

    for c in range(D_FF // fc):
        val = up_conv(c * fc)
        gate = up_conv(D_FF + c * fc)
        act = (gate * jax.nn.sigmoid(gate) * val).astype(MXU_DTYPE)
        part = _dot(act, wdn_ref[c * fc:(c + 1) * fc, :])
        if c == 0:
            acc_ref[...] = part
        else:
            acc_ref[...] += part
    o_ref[...] = _layer_norm(ALPHA * x + acc_ref[...], g_ref[...], b_ref[...])


def _ffn(x, w_up, conv_w, conv_b, w_down, g, b, tm, tiles_per_seq, fc):
    N, D = x.shape
    row = lambda i: (i, 0)
    const = lambda i: (0, 0)
    halo_blocks = tm // SUBLANES
    return pl.pallas_call(
        functools.partial(_ffn_kernel, tiles_per_seq=tiles_per_seq, fc=fc),
        grid=(N // tm,),
        in_specs=[
            pl.BlockSpec((tm, D), row),
            pl.BlockSpec((SUBLANES, D), lambda i: (jnp.maximum(i * halo_blocks - 1, 0), 0)),
            pl.BlockSpec((D, 2 * D_FF), const, pipeline_mode=pl.Buffered(1)),
            pl.BlockSpec((3, 2 * D_FF), const),
            pl.BlockSpec((1, 2 * D_FF), const),
            pl.BlockSpec((D_FF, D), const, pipeline_mode=pl.Buffered(1)),
            pl.BlockSpec((1, D), const), pl.BlockSpec((1, D), const),
        ],
        out_specs=pl.BlockSpec((tm, D), row),
        out_shape=jax.ShapeDtypeStruct((N, D), F32),
        scratch_shapes=[pltpu.VMEM((tm + SUBLANES, fc), F32), pltpu.VMEM((tm, D), F32)],
        compiler_params=pltpu.CompilerParams(
            dimension_semantics=("arbitrary",), vmem_limit_bytes=VMEM_LIMIT),
    )(x, x, w_up, conv_w, conv_b, w_down, g, b)


def _rope_lane_tables(T):
    half = ROPE_DIM // 2
    inv_freq = ROPE_THETA ** (-(np.arange(half, dtype=np.float32) * 2.0 / ROPE_DIM))
    ang = jnp.arange(T, dtype=F32)[:, None] * jnp.asarray(inv_freq, F32)[None]
    cos, sin = jnp.cos(ang), jnp.sin(ang)
    rest = HEAD_DIM - ROPE_DIM
    one, zero = jnp.ones((T, rest), F32), jnp.zeros((T, rest), F32)
    zh = jnp.zeros((T, half), F32)
    cos_t = jnp.concatenate([cos, cos, one], -1)
    sa_t = jnp.concatenate([-sin, zh, zero], -1)
    sb_t = jnp.concatenate([zh, sin, zero], -1)
    rep = LANES // HEAD_DIM
    return tuple(jnp.tile(t, (1, rep)) for t in (cos_t, sa_t, sb_t))


def _slc_aggregation_t(T):
    nc = T // CMP_STRIDE - CMP_LEN // CMP_STRIDE + 1
    ns = T // SLC_LEN
    sc = np.arange(nc)[None, :] * CMP_STRIDE
    ss = np.arange(ns)[:, None] * SLC_LEN
    ov = np.clip(np.minimum(sc + CMP_LEN, ss + SLC_LEN) - np.maximum(sc, ss), 0, None) / CMP_LEN
    ov = np.pad(ov, ((0, 0), (0, T // CMP_STRIDE - nc)))
    return jnp.asarray(ov, dtype=MXU_DTYPE)


def kernel(x, w_in, short_conv_w, cmp_pe, cmp_w1, cmp_b1, cmp_w2, cmp_b2, head_norm_g, w_out, ln1_g, ln1_b,
           w_up, ffn_conv_w, ffn_conv_b, w_down, ln2_g, ln2_b):
    B, T, D = x.shape
    tm = min(512, T)
    cos_t, sa_t, sb_t = _rope_lane_tables(T)
    agg_t = _slc_aggregation_t(T)
    nsub = T // CMP_STRIDE
    for i in range(DEPTH):
        w_in_p = jnp.pad(w_in[i], ((0, 0), (0, _IN_COLS_PAD - _IN_COLS))).astype(MXU_DTYPE)
        hg = head_norm_g[i]
        conv_n, q, cv, ks, vs, kw, vw, gates = _inproj(
            x, w_in_p, short_conv_w[i], cos_t, sa_t, sb_t, hg[None, :CONV_WIDTH], tm)
        sub = cv.reshape(2, B * N_KV_GROUPS, nsub, CMP_STRIDE * HEAD_DIM)
        cmp_kv = _compress(
            sub, cmp_pe[i].reshape(2, 1, CMP_LEN * HEAD_DIM), cmp_w1[i].astype(MXU_DTYPE),
            cmp_b1[i][:, None, :], cmp_w2[i].astype(MXU_DTYPE), cmp_b2[i][:, None, :])
        attn_n = _nsa(q, cmp_kv, ks, vs, kw, vw, gates, agg_t,
                      hg[CONV_WIDTH:].reshape(N_KV_GROUPS, 1, Q_PER_KV * HEAD_DIM), tq=256, kc=512)
        x2 = _outproj(conv_n.reshape(B * T, CONV_WIDTH), attn_n.reshape(B * T, ATTN_WIDTH),
                      x.reshape(B * T, D), w_out[i].astype(MXU_DTYPE), ln1_g[i][None], ln1_b[i][None], tm)
        x2 = _ffn(x2, w_up[i].astype(MXU_DTYPE), ffn_conv_w[i], ffn_conv_b[i][None],
                  w_down[i].astype(MXU_DTYPE), ln2_g[i][None], ln2_b[i][None], tm, T // tm, 256)
        x = x2.reshape(B, T, D)
    return x
```

```python
import functools

import numpy as np
import jax
import jax.numpy as jnp
from jax import lax
from jax.experimental import pallas as pl
from jax.experimental.pallas import tpu as pltpu

D_MODEL = 1024
DEPTH = 2
HEAD_DIM = 64
CONV_WIDTH = 512
ATTN_WIDTH = 512
N_Q_HEADS = 8
N_KV_GROUPS = 2
Q_PER_KV = 4
KV_WIDTH = N_KV_GROUPS * HEAD_DIM
N_GATES = 3
CMP_LEN = 32
CMP_STRIDE = 16
CMP_HIDDEN = 256
SLC_LEN = 64
SLC_TOPK = 16
WINDOW = 512
ROPE_THETA = 500000.0
ROPE_DIM = 16
D_FF = 2816
ALPHA = (2.0 * DEPTH) ** 0.25
NEG = -1e30
BIG = 1e9

F32 = jnp.float32
MXU_DTYPE = jnp.bfloat16
LANES = 128
SUBLANES = 8
VMEM_LIMIT = 56 * 1024 * 1024

_OFF_B, _OFF_C, _OFF_H, _OFF_Q = 0, 512, 1024, 1536
_OFF_KC, _OFF_VC, _OFF_KS, _OFF_VS, _OFF_KW, _OFF_VW, _OFF_GATE = 2048, 2176, 2304, 2432, 2560, 2688, 2816
_IN_COLS = 2840
_IN_COLS_PAD = 2944

_NT = (((1,), (1,)), ((), ()))


def _dot(a, b):
    return jnp.dot(a, b, preferred_element_type=F32)


def _dot_nt(a, b):
    return lax.dot_general(a, b, _NT, preferred_element_type=F32)


def _layer_norm(y, g, b):
    mu = jnp.mean(y, axis=-1, keepdims=True)
    d = y - mu
    var = jnp.mean(d * d, axis=-1, keepdims=True)
    return d * lax.rsqrt(var + 1e-5) * g + b


def _group_rmsnorm(seg, g):
    ms = jnp.mean(seg * seg, axis=-1, keepdims=True)
    return seg * lax.rsqrt(ms + 1e-6) * g


def _inproj_kernel(x_ref, w_ref, cw_ref, cos_ref, sa_ref, sb_ref, hg_ref,
                   conv_ref, q_ref, cv_ref, ks_ref, vs_ref, kw_ref, vw_ref, gate_ref, zbuf):
    tm = x_ref.shape[0]
    xb = x_ref[...].astype(MXU_DTYPE)

    def mm(c0, width):
        return _dot(xb, w_ref[:, c0:c0 + width])

    @pl.when(pl.program_id(1) == 0)
    def _():
        zbuf[0:SUBLANES, :] = jnp.zeros((SUBLANES, CONV_WIDTH), F32)

    z = mm(_OFF_C, CONV_WIDTH) * mm(_OFF_H, CONV_WIDTH)
    zbuf[SUBLANES:SUBLANES + tm, :] = z
    conv = (cw_ref[2:3, :] * z + cw_ref[1:2, :] * zbuf[SUBLANES - 1:SUBLANES - 1 + tm, :]
            + cw_ref[0:1, :] * zbuf[SUBLANES - 2:SUBLANES - 2 + tm, :])
    zbuf[0:SUBLANES, :] = zbuf[tm:tm + SUBLANES, :]
    y = mm(_OFF_B, CONV_WIDTH) * conv
    for j in range(CONV_WIDTH // HEAD_DIM):
        sl = slice(j * HEAD_DIM, (j + 1) * HEAD_DIM)
        conv_ref[:, sl] = _group_rmsnorm(y[:, sl], hg_ref[:, sl]).astype(conv_ref.dtype)

    def rope(v):
        return (v * cos_ref[...] + pltpu.roll(v, LANES - ROPE_DIM // 2, 1) * sa_ref[...]
                + pltpu.roll(v, ROPE_DIM // 2, 1) * sb_ref[...])

    def split_heads(v, ref, lead):
        for g in range(N_KV_GROUPS):
            ref[lead + (g,)] = v[:, g * HEAD_DIM:(g + 1) * HEAD_DIM].astype(ref.dtype)

    lane = lax.broadcasted_iota(jnp.int32, (tm, LANES), 1)
    low = lane < HEAD_DIM

    def head_pair(v):
        return v, pltpu.roll(v, HEAD_DIM, 1)

    scale = HEAD_DIM ** -0.5
    for c in range(ATTN_WIDTH // LANES):
        r = rope(mm(_OFF_Q + c * LANES, LANES)) * scale
        for j, h in enumerate(head_pair(r)):
            q_ref[2 * c + j] = jnp.where(low, h, 0.0).astype(q_ref.dtype)
    split_heads(rope(mm(_OFF_KC, KV_WIDTH)), cv_ref, (0,))
    split_heads(mm(_OFF_VC, KV_WIDTH), cv_ref, (1,))
    pos = pl.program_id(1) * tm + lax.broadcasted_iota(jnp.int32, (tm, LANES), 0)
    block_tag = jnp.where(pos // SLC_LEN == lane - HEAD_DIM, NEG, 0.0)
    for g, h in enumerate(head_pair(rope(mm(_OFF_KS, KV_WIDTH)))):
        ks_ref[g] = jnp.where(low, h, block_tag).astype(ks_ref.dtype)
    for g, h in enumerate(head_pair(mm(_OFF_VS, KV_WIDTH))):
        vs_ref[g] = jnp.where(low, h, 1.0).astype(vs_ref.dtype)
    split_heads(rope(mm(_OFF_KW, KV_WIDTH)), kw_ref, ())
    for g, h in enumerate(head_pair(mm(_OFF_VW, KV_WIDTH))):
        vw_ref[g] = jnp.where(low, h, 1.0).astype(vw_ref.dtype)
    sg = jax.nn.sigmoid(mm(_OFF_GATE, LANES))
    per_group = Q_PER_KV * N_GATES
    for g in range(N_KV_GROUPS):
        gate_ref[g] = sg[:, g * per_group:(g + 1) * per_group]


def _inproj(x, w_in_p, conv_w, cos_t, sa_t, sb_t, hg_conv, tm):
    B, T, D = x.shape
    kv_shape = jax.ShapeDtypeStruct((B, N_KV_GROUPS, T, HEAD_DIM), MXU_DTYPE)
    kv_spec = pl.BlockSpec((None, N_KV_GROUPS, tm, HEAD_DIM), lambda b, i: (b, 0, i, 0))
    wide_shape = jax.ShapeDtypeStruct((B, N_KV_GROUPS, T, LANES), MXU_DTYPE)
    wide_spec = pl.BlockSpec((None, N_KV_GROUPS, tm, LANES), lambda b, i: (b, 0, i, 0))
    tab_spec = pl.BlockSpec((tm, LANES), lambda b, i: (i, 0))
    const2 = lambda b, i: (0, 0)
    per_group = Q_PER_KV * N_GATES
    return pl.pallas_call(
        _inproj_kernel,
        grid=(B, T // tm),
        in_specs=[
            pl.BlockSpec((None, tm, D), lambda b, i: (b, i, 0)),
            pl.BlockSpec((D, _IN_COLS_PAD), const2),
            pl.BlockSpec((3, CONV_WIDTH), const2),
            tab_spec, tab_spec, tab_spec,
            pl.BlockSpec((1, CONV_WIDTH), const2),
        ],
        out_specs=[
            pl.BlockSpec((None, tm, CONV_WIDTH), lambda b, i: (b, i, 0)),
            pl.BlockSpec((None, N_Q_HEADS, tm, LANES), lambda b, i: (b, 0, i, 0)),
            pl.BlockSpec((2, None, N_KV_GROUPS, tm, HEAD_DIM), lambda b, i: (0, b, 0, i, 0)),
            wide_spec, wide_spec, kv_spec, wide_spec,
            pl.BlockSpec((None, N_KV_GROUPS, tm, per_group), lambda b, i: (b, 0, i, 0)),
        ],
        out_shape=[
            jax.ShapeDtypeStruct((B, T, CONV_WIDTH), MXU_DTYPE),
            jax.ShapeDtypeStruct((B, N_Q_HEADS, T, LANES), MXU_DTYPE),
            jax.ShapeDtypeStruct((2, B, N_KV_GROUPS, T, HEAD_DIM), MXU_DTYPE),
            wide_shape, wide_shape, kv_shape, wide_shape,
            jax.ShapeDtypeStruct((B, N_KV_GROUPS, T, per_group), F32),
        ],
        scratch_shapes=[pltpu.VMEM((tm + 2 * SUBLANES, CONV_WIDTH), F32)],
        compiler_params=pltpu.CompilerParams(
            dimension_semantics=("arbitrary", "arbitrary"), vmem_limit_bytes=VMEM_LIMIT),
    )(x, w_in_p, conv_w, cos_t, sa_t, sb_t, hg_conv)


def _compress_kernel(sub_ref, pe_ref, w1_ref, b1_ref, w2_ref, b2_ref, out_ref, bbuf):
    nsub = sub_ref.shape[0]
    half = CMP_STRIDE * HEAD_DIM
    sub = sub_ref[...]
    top = _dot(sub, w1_ref[0:half, :])
    bbuf[0:nsub, :] = _dot(sub, w1_ref[half:2 * half, :])
    bbuf[nsub:nsub + SUBLANES, :] = jnp.zeros((SUBLANES, CMP_HIDDEN), F32)
    pe_rows = jnp.broadcast_to(pe_ref[...], (SUBLANES, 2 * half)).astype(MXU_DTYPE)
    const = _dot(pe_rows, w1_ref[...])[0:1, :] + b1_ref[...]
    h = top + bbuf[1:nsub + 1, :] + const
    act = jax.nn.gelu(h)
    out_ref[...] = (_dot(act.astype(MXU_DTYPE), w2_ref[...]) + b2_ref[...]).astype(out_ref.dtype)


def _compress(sub, pe, w1, b1, w2, b2):
    _, BG, nsub, width = sub.shape
    sel = lambda k, n: (k, 0, 0)
    return pl.pallas_call(
        _compress_kernel,
        grid=(2, BG),
        in_specs=[
            pl.BlockSpec((None, None, nsub, width), lambda k, n: (k, n, 0, 0)),
            pl.BlockSpec((None, 1, 2 * width), sel),
            pl.BlockSpec((None, 2 * width, CMP_HIDDEN), sel),
            pl.BlockSpec((None, 1, CMP_HIDDEN), sel),
            pl.BlockSpec((None, CMP_HIDDEN, HEAD_DIM), sel),
            pl.BlockSpec((None, 1, HEAD_DIM), sel),
        ],
        out_specs=pl.BlockSpec((None, None, nsub, HEAD_DIM), lambda k, n: (k, n, 0, 0)),
        out_shape=jax.ShapeDtypeStruct((2, BG, nsub, HEAD_DIM), MXU_DTYPE),
        scratch_shapes=[pltpu.VMEM((nsub + SUBLANES, CMP_HIDDEN), F32)],
        compiler_params=pltpu.CompilerParams(
            dimension_semantics=("arbitrary", "arbitrary"), vmem_limit_bytes=VMEM_LIMIT),
    )(sub, pe, w1, b1, w2, b2)


def _topk_block_mask(score, topk):
    ns, tq = score.shape
    n_slabs = ns // SUBLANES
    slabs = [score[SUBLANES * j:SUBLANES * (j + 1), :] for j in range(n_slabs)]
    ranks = [jnp.zeros((SUBLANES, tq), F32) for _ in range(n_slabs)]
    local = lax.broadcasted_iota(jnp.int32, (SUBLANES, tq), 0)
    for i in range(ns):
        row = jnp.broadcast_to(score[i:i + 1, :], (SUBLANES, tq))
        for j in range(n_slabs):
            ge = jnp.where(row >= slabs[j], 1.0, 0.0)
            gt = jnp.where(row > slabs[j], 1.0, 0.0)
            if i < SUBLANES * j:
                ahead = ge
            elif i >= SUBLANES * (j + 1):
                ahead = gt
            else:
                ahead = jnp.where(local > i - SUBLANES * j, ge, gt)
            ranks[j] = ranks[j] + ahead
    rank = jnp.concatenate(ranks, axis=0)
    return jnp.where(rank < topk, 1.0, 0.0)


def _nsa_kernel(q_ref, kcmp_ref, vcmp_ref, ks_ref, vs_ref, kw_ref, vw_ref, gate_ref, aggt_ref, hg_ref,
                o_ref, sa_ref, sb_ref, acc_ref, part_ref, gsel_ref, *, tq, kc):
    R = Q_PER_KV
    rows = R * tq
    T = ks_ref.shape[0]
    ns = T // SLC_LEN
    ncp = kcmp_ref.shape[0]
    q0 = pl.program_id(2) * tq
    t_col = q0 + lax.broadcasted_iota(jnp.int32, (tq, 1), 0)
    q_pad = q_ref[...]
    q64 = q_pad.reshape(rows, LANES)[:, :HEAD_DIM]

    def per_head(a):
        return a.reshape(R, tq, a.shape[-1])

    half = tq // 2
    wl = WINDOW + half
    o_win_halves = []
    for h in range(2):
        t_half = t_col[h * half:(h + 1) * half]
        ws = pl.multiple_of(jnp.maximum(q0 + h * half - WINDOW, 0), half)
        wpos = ws + lax.broadcasted_iota(jnp.int32, (half, wl), 1)
        wbias = jnp.where((wpos <= t_half) & (wpos > t_half - WINDOW), 0.0, NEG)
        qh = per_head(q64)[:, h * half:(h + 1) * half, :].reshape(R * half, HEAD_DIM)
        s = _dot_nt(qh, kw_ref[pl.ds(ws, wl), :]).reshape(R, half, wl) + wbias[None]
        p = jnp.exp(s - jnp.max(s, axis=-1, keepdims=True))
        ow = _dot(p.reshape(R * half, wl).astype(MXU_DTYPE), vw_ref[pl.ds(ws, wl), :])
        o_win_halves.append(ow[:, :HEAD_DIM] / pltpu.roll(ow, HEAD_DIM, 1)[:, :HEAD_DIM])

    cmp_end = lax.broadcasted_iota(jnp.int32, (tq, ncp), 1) * CMP_STRIDE + (CMP_LEN - 1)
    cbias = jnp.where(cmp_end <= t_col, 0.0, NEG)
    s = per_head(_dot_nt(q64, kcmp_ref[...])) + cbias[None]
    p = jnp.exp(s - jnp.max(s, axis=-1, keepdims=True))
    l = jnp.sum(p, axis=-1, keepdims=True)
    p = p * jnp.where((t_col >= CMP_LEN - 1)[None], 1.0 / l, 0.0)
    o_cmp = _dot(p.reshape(rows, ncp).astype(MXU_DTYPE), vcmp_ref[...])

    g = gate_ref[...]
    for r in range(R):
        c = N_GATES * r
        hr = slice(r * tq, (r + 1) * tq)
        hh = slice(r * half, (r + 1) * half)
        o_win = jnp.concatenate([o_win_halves[0][hh], o_win_halves[1][hh]], axis=0)
        part_ref[hr, :] = g[:, c:c + 1] * o_cmp[hr] + g[:, c + 2:c + 3] * o_win
        gsel_ref[hr, :] = jnp.broadcast_to(g[:, c + 1:c + 2], (tq, HEAD_DIM))

    psum = p[0] + p[1] + p[2] + p[3]
    p_hi = psum.astype(MXU_DTYPE)
    p_lo = (psum - p_hi.astype(F32)).astype(MXU_DTYPE)
    imp = _dot_nt(aggt_ref[...], p_hi) + _dot_nt(aggt_ref[...], p_lo)
    blk = lax.broadcasted_iota(jnp.int32, (ns, tq), 0)
    tb = (q0 + lax.broadcasted_iota(jnp.int32, (ns, tq), 1)) // SLC_LEN
    forced = (blk == 0) | (blk == tb) | (blk == tb - 1)
    score = jnp.where(forced, BIG, imp)
    score = jnp.where(blk <= tb, score, -BIG)
    unselected = 1.0 - _topk_block_mask(score, min(SLC_TOPK, ns))
    tag_rows = [jnp.zeros((HEAD_DIM, tq), F32), unselected]
    if ns < LANES - HEAD_DIM:
        tag_rows.append(jnp.zeros((LANES - HEAD_DIM - ns, tq), F32))
    tag = jnp.concatenate(tag_rows, axis=0).T.astype(MXU_DTYPE)
    lane = lax.broadcasted_iota(jnp.int32, (R, tq, LANES), 2)
    q_aug = jnp.where(lane < HEAD_DIM, q_pad, tag[None]).reshape(rows, LANES)

    def scores(c):
        return _dot_nt(q_aug, ks_ref[pl.ds(pl.multiple_of(c * kc, kc), kc), :])

    def causal_bias(c):
        kpos = c * kc + lax.broadcasted_iota(jnp.int32, (tq, kc), 1)
        return jnp.where(kpos <= t_col, 0.0, NEG)[None]

    def consume(s, c, m):
        m_new = jnp.maximum(m, jnp.max(s, axis=-1, keepdims=True))
        p = jnp.exp(s - m_new).astype(MXU_DTYPE)
        pv = _dot(p, vs_ref[pl.ds(pl.multiple_of(c * kc, kc), kc), :])
        acc_ref[...] = jnp.exp(m - m_new) * acc_ref[...] + pv
        return m_new

    n_pairs = q0 // (2 * kc)
    acc_ref[...] = jnp.zeros((rows, LANES), F32)
    sa_ref[...] = scores(0)

    def pair_step(j, m):
        sb_ref[...] = scores(2 * j + 1)
        m = consume(sa_ref[...], 2 * j, m)
        sa_ref[...] = scores(2 * j + 2)
        return consume(sb_ref[...], 2 * j + 1, m)

    m = lax.fori_loop(0, n_pairs, pair_step, jnp.full((rows, 1), NEG, F32))
    c_last = 2 * n_pairs
    m = consume((per_head(sa_ref[...]) + causal_bias(c_last)).reshape(rows, kc), c_last, m)

    @pl.when(q0 + tq > (c_last + 1) * kc)
    def _():
        consume((per_head(scores(c_last + 1)) + causal_bias(c_last + 1)).reshape(rows, kc), c_last + 1, m)

    acc = acc_ref[...]
    o_slc = acc[:, :HEAD_DIM] / pltpu.roll(acc, HEAD_DIM, 1)[:, :HEAD_DIM]

    o = part_ref[...] + gsel_ref[...] * o_slc
    for r in range(R):
        sl = slice(r * HEAD_DIM, (r + 1) * HEAD_DIM)
        o_ref[:, sl] = _group_rmsnorm(o[r * tq:(r + 1) * tq], hg_ref[:, sl]).astype(o_ref.dtype)


def _nsa(q, cmp_kv, ks, vs, kw, vw, gates, agg_t, hg_attn, tq, kc):
    B, _, T, _ = q.shape
    G, R = N_KV_GROUPS, Q_PER_KV
    ncp = cmp_kv.shape[2]
    ns = T // SLC_LEN
    assert T % (2 * kc) == 0 and kc % tq == 0 and WINDOW % (tq // 2) == 0 and T >= WINDOW + tq
    assert ns <= LANES - HEAD_DIM, "block tags must fit in the upper lanes of a key row"
    kv_spec = pl.BlockSpec((None, None, T, HEAD_DIM), lambda b, g, i: (b, g, 0, 0))
    wide_spec = pl.BlockSpec((None, None, T, LANES), lambda b, g, i: (b, g, 0, 0))
    return pl.pallas_call(
        functools.partial(_nsa_kernel, tq=tq, kc=kc),
        grid=(B, G, T // tq),
        in_specs=[
            pl.BlockSpec((None, R, tq, LANES), lambda b, g, i: (b, g, i, 0)),
            pl.BlockSpec((None, None, ncp, HEAD_DIM), lambda b, g, i: (0, b * G + g, 0, 0)),
            pl.BlockSpec((None, None, ncp, HEAD_DIM), lambda b, g, i: (1, b * G + g, 0, 0)),
            wide_spec, wide_spec, kv_spec, wide_spec,
            pl.BlockSpec((None, None, tq, R * N_GATES), lambda b, g, i: (b, g, i, 0)),
            pl.BlockSpec((ns, ncp), lambda b, g, i: (0, 0)),
            pl.BlockSpec((None, 1, R * HEAD_DIM), lambda b, g, i: (g, 0, 0)),
        ],
        out_specs=pl.BlockSpec((None, tq, R * HEAD_DIM), lambda b, g, i: (b, i, g)),
        out_shape=jax.ShapeDtypeStruct((B, T, ATTN_WIDTH), MXU_DTYPE),
        scratch_shapes=[pltpu.VMEM((R * tq, kc), F32), pltpu.VMEM((R * tq, kc), F32),
                        pltpu.VMEM((R * tq, LANES), F32),
                        pltpu.VMEM((R * tq, HEAD_DIM), F32), pltpu.VMEM((R * tq, HEAD_DIM), F32)],
        compiler_params=pltpu.CompilerParams(
            dimension_semantics=("arbitrary", "arbitrary", "arbitrary"), vmem_limit_bytes=VMEM_LIMIT),
    )(q, cmp_kv, cmp_kv, ks, vs, kw, vw, gates, agg_t, hg_attn)


def _outproj_kernel(conv_ref, attn_ref, x_ref, w_ref, g_ref, b_ref, o_ref):
    mix = _dot(conv_ref[...], w_ref[0:CONV_WIDTH, :]) + _dot(attn_ref[...], w_ref[CONV_WIDTH:, :])
    o_ref[...] = _layer_norm(ALPHA * x_ref[...] + mix, g_ref[...], b_ref[...])


def _outproj(conv_n, attn_n, x, w_out, g, b, tm):
    N, D = x.shape
    row = lambda i: (i, 0)
    const = lambda i: (0, 0)
    return pl.pallas_call(
        _outproj_kernel,
        grid=(N // tm,),
        in_specs=[
            pl.BlockSpec((tm, CONV_WIDTH), row), pl.BlockSpec((tm, ATTN_WIDTH), row),
            pl.BlockSpec((tm, D), row), pl.BlockSpec((D, D), const),
            pl.BlockSpec((1, D), const), pl.BlockSpec((1, D), const),
        ],
        out_specs=pl.BlockSpec((tm, D), row),
        out_shape=jax.ShapeDtypeStruct((N, D), F32),
        compiler_params=pltpu.CompilerParams(
            dimension_semantics=("arbitrary",), vmem_limit_bytes=VMEM_LIMIT),
    )(conv_n, attn_n, x, w_out, g, b)


def _ffn_kernel(x_ref, halo_ref, wup_ref, cw_ref, cb_ref, wdn_ref, g_ref, b_ref, o_ref, ubuf, acc_ref,
                *, tiles_per_seq, fc):
    tm = x_ref.shape[0]
    x = x_ref[...]
    first = (pl.program_id(0) % tiles_per_seq) == 0
    halo = jnp.where(first, 0.0, halo_ref[...])
    xb = jnp.concatenate([halo, x], axis=0).astype(MXU_DTYPE)

    def up_conv(col0):
        ubuf[...] = _dot(xb, wup_ref[:, col0:col0 + fc])
        return (cw_ref[2:3, col0:col0 + fc] * ubuf[SUBLANES:SUBLANES + tm, :]
                + cw_ref[1:2, col0:col0 + fc] * ubuf[SUBLANES - 1:SUBLANES - 1 + tm, :]
                + cw_ref[0:1, col0:col0 + fc] * ubuf[SUBLANES - 2:SUBLANES - 2 + tm, :]
                + cb_ref[:, col0:col0 + fc])

    for c in range(D_FF // fc):
        val = up_conv(c * fc)
        gate = up_conv(D_FF + c * fc)
        act = (gate * jax.nn.sigmoid(gate) * val).astype(MXU_DTYPE)
        part = _dot(act, wdn_ref[c * fc:(c + 1) * fc, :])
        if c == 0:
            acc_ref[...] = part
        else:
            acc_ref[...] += part
    o_ref[...] = _layer_norm(ALPHA * x + acc_ref[...], g_ref[...], b_ref[...])


def _ffn(x, w_up, conv_w, conv_b, w_down, g, b, tm, tiles_per_seq, fc):
    N, D = x.shape
    row = lambda i: (i, 0)
    const = lambda i: (0, 0)
    halo_blocks = tm // SUBLANES
    return pl.pallas_call(
        functools.partial(_ffn_kernel, tiles_per_seq=tiles_per_seq, fc=fc),
        grid=(N // tm,),
        in_specs=[
            pl.BlockSpec((tm, D), row),
            pl.BlockSpec((SUBLANES, D), lambda i: (jnp.maximum(i * halo_blocks - 1, 0), 0)),
            pl.BlockSpec((D, 2 * D_FF), const, pipeline_mode=pl.Buffered(1)),
            pl.BlockSpec((3, 2 * D_FF), const),
            pl.BlockSpec((1, 2 * D_FF), const),
            pl.BlockSpec((D_FF, D), const, pipeline_mode=pl.Buffered(1)),
            pl.BlockSpec((1, D), const), pl.BlockSpec((1, D), const),
        ],
        out_specs=pl.BlockSpec((tm, D), row),
        out_shape=jax.ShapeDtypeStruct((N, D), F32),
        scratch_shapes=[pltpu.VMEM((tm + SUBLANES, fc), F32), pltpu.VMEM((tm, D), F32)],
        compiler_params=pltpu.CompilerParams(
            dimension_semantics=("arbitrary",), vmem_limit_bytes=VMEM_LIMIT),
    )(x, x, w_up, conv_w, conv_b, w_down, g, b)


def _rope_lane_tables(T):
    half = ROPE_DIM // 2
    inv_freq = ROPE_THETA ** (-(np.arange(half, dtype=np.float32) * 2.0 / ROPE_DIM))
    ang = jnp.arange(T, dtype=F32)[:, None] * jnp.asarray(inv_freq, F32)[None]
    cos, sin = jnp.cos(ang), jnp.sin(ang)
    rest = HEAD_DIM - ROPE_DIM
    one, zero = jnp.ones((T, rest), F32), jnp.zeros((T, rest), F32)
    zh = jnp.zeros((T, half), F32)
    cos_t = jnp.concatenate([cos, cos, one], -1)
    sa_t = jnp.concatenate([-sin, zh, zero], -1)
    sb_t = jnp.concatenate([zh, sin, zero], -1)
    rep = LANES // HEAD_DIM
    return tuple(jnp.tile(t, (1, rep)) for t in (cos_t, sa_t, sb_t))


def _slc_aggregation_t(T):
    nc = T // CMP_STRIDE - CMP_LEN // CMP_STRIDE + 1
    ns = T // SLC_LEN
    sc = np.arange(nc)[None, :] * CMP_STRIDE
    ss = np.arange(ns)[:, None] * SLC_LEN
    ov = np.clip(np.minimum(sc + CMP_LEN, ss + SLC_LEN) - np.maximum(sc, ss), 0, None) / CMP_LEN
    ov = np.pad(ov, ((0, 0), (0, T // CMP_STRIDE - nc)))
    return jnp.asarray(ov, dtype=MXU_DTYPE)


def kernel(x, w_in, short_conv_w, cmp_pe, cmp_w1, cmp_b1, cmp_w2, cmp_b2, head_norm_g, w_out, ln1_g, ln1_b,
           w_up, ffn_conv_w, ffn_conv_b, w_down, ln2_g, ln2_b):
    B, T, D = x.shape
    tm = min(512, T)
    cos_t, sa_t, sb_t = _rope_lane_tables(T)
    agg_t = _slc_aggregation_t(T)
    nsub = T // CMP_STRIDE
    for i in range(DEPTH):
        w_in_p = jnp.pad(w_in[i], ((0, 0), (0, _IN_COLS_PAD - _IN_COLS))).astype(MXU_DTYPE)
        hg = head_norm_g[i]
        conv_n, q, cv, ks, vs, kw, vw, gates = _inproj(
            x, w_in_p, short_conv_w[i], cos_t, sa_t, sb_t, hg[None, :CONV_WIDTH], tm)
        sub = cv.reshape(2, B * N_KV_GROUPS, nsub, CMP_STRIDE * HEAD_DIM)
        cmp_kv = _compress(
            sub, cmp_pe[i].reshape(2, 1, CMP_LEN * HEAD_DIM), cmp_w1[i].astype(MXU_DTYPE),
            cmp_b1[i][:, None, :], cmp_w2[i].astype(MXU_DTYPE), cmp_b2[i][:, None, :])
        attn_n = _nsa(q, cmp_kv, ks, vs, kw, vw, gates, agg_t,
                      hg[CONV_WIDTH:].reshape(N_KV_GROUPS, 1, Q_PER_KV * HEAD_DIM), tq=256, kc=512)
        x2 = _outproj(conv_n.reshape(B * T, CONV_WIDTH), attn_n.reshape(B * T, ATTN_WIDTH),
                      x.reshape(B * T, D), w_out[i].astype(MXU_DTYPE), ln1_g[i][None], ln1_b[i][None], tm)
        x2 = _ffn(x2, w_up[i].astype(MXU_DTYPE), ffn_conv_w[i], ffn_conv_b[i][None],
                  w_down[i].astype(MXU_DTYPE), ln2_g[i][None], ln2_b[i][None], tm, T // tm, 256)
        x = x2.reshape(B, T, D)
    return x
```

```python
import functools

import numpy as np
import jax
import jax.numpy as jnp
from jax import lax
from jax.experimental import pallas as pl
from jax.experimental.pallas import tpu as pltpu

D_MODEL = 1024
DEPTH = 2
HEAD_DIM = 64
CONV_WIDTH = 512
ATTN_WIDTH = 512
N_Q_HEADS = 8
N_KV_GROUPS = 2
Q_PER_KV = 4
KV_WIDTH = N_KV_GROUPS * HEAD_DIM
N_GATES = 3
CMP_LEN = 32
CMP_STRIDE = 16
CMP_HIDDEN = 256
SLC_LEN = 64
SLC_TOPK = 16
WINDOW = 512
ROPE_THETA = 500000.0
ROPE_DIM = 16
D_FF = 2816
ALPHA = (2.0 * DEPTH) ** 0.25
NEG = -1e30
BIG = 1e9

F32 = jnp.float32
MXU_DTYPE = jnp.bfloat16
LANES = 128
SUBLANES = 8
MXU_COLS = 256
VMEM_LIMIT = 56 * 1024 * 1024

_OFF_B, _OFF_C, _OFF_H, _OFF_Q = 0, 512, 1024, 1536
_OFF_KC, _OFF_VC, _OFF_KS, _OFF_VS, _OFF_KW, _OFF_VW, _OFF_GATE = 2048, 2176, 2304, 2432, 2560, 2688, 2816
_IN_COLS = 2840
_IN_COLS_PAD = 2944

_NT = (((1,), (1,)), ((), ()))


def _dot(a, b):
    return jnp.dot(a, b, preferred_element_type=F32)


def _dot_nt(a, b):
    return lax.dot_general(a, b, _NT, preferred_element_type=F32)


def _layer_norm(y, g, b):
    mu = jnp.mean(y, axis=-1, keepdims=True)
    d = y - mu
    var = jnp.mean(d * d, axis=-1, keepdims=True)
    return d * lax.rsqrt(var + 1e-5) * g + b


def _group_rmsnorm(seg, g):
    ms = jnp.mean(seg * seg, axis=-1, keepdims=True)
    return seg * lax.rsqrt(ms + 1e-6) * g


def _inproj_kernel(x_ref, w_ref, cw_ref, cos_ref, sa_ref, sb_ref, hg_ref,
                   conv_ref, q_ref, cv_ref, ks_ref, vs_ref, kw_ref, vw_ref, gate_ref, zbuf):
    tm = x_ref.shape[0]
    xb = x_ref[...].astype(MXU_DTYPE)

    def mm(c0, width):
        return _dot(xb, w_ref[:, c0:c0 + width])

    @pl.when(pl.program_id(1) == 0)
    def _():
        zbuf[0:SUBLANES, :] = jnp.zeros((SUBLANES, CONV_WIDTH), F32)

    def conv_block(c0):
        cs = slice(c0, c0 + MXU_COLS)
        z = mm(_OFF_C + c0, MXU_COLS) * mm(_OFF_H + c0, MXU_COLS)
        zbuf[SUBLANES:SUBLANES + tm, cs] = z
        conv = (cw_ref[2:3, cs] * z + cw_ref[1:2, cs] * zbuf[SUBLANES - 1:SUBLANES - 1 + tm, cs]
                + cw_ref[0:1, cs] * zbuf[SUBLANES - 2:SUBLANES - 2 + tm, cs])
        zbuf[0:SUBLANES, cs] = zbuf[tm:tm + SUBLANES, cs]
        y = mm(_OFF_B + c0, MXU_COLS) * conv
        for j in range(MXU_COLS // HEAD_DIM):
            sl = slice(c0 + j * HEAD_DIM, c0 + (j + 1) * HEAD_DIM)
            seg = y[:, j * HEAD_DIM:(j + 1) * HEAD_DIM]
            conv_ref[:, sl] = _group_rmsnorm(seg, hg_ref[:, sl]).astype(conv_ref.dtype)

    def rope(v):
        return (v * cos_ref[...] + pltpu.roll(v, LANES - ROPE_DIM // 2, 1) * sa_ref[...]
                + pltpu.roll(v, ROPE_DIM // 2, 1) * sb_ref[...])

    def split_heads(v, ref, lead):
        for g in range(N_KV_GROUPS):
            ref[lead + (g,)] = v[:, g * HEAD_DIM:(g + 1) * HEAD_DIM].astype(ref.dtype)

    lane = lax.broadcasted_iota(jnp.int32, (tm, LANES), 1)
    low = lane < HEAD_DIM

    def head_pair(v):
        return v, pltpu.roll(v, HEAD_DIM, 1)

    def q_block(c):
        r4 = mm(_OFF_Q + c * MXU_COLS, MXU_COLS)
        for i in range(MXU_COLS // LANES):
            r = rope(r4[:, i * LANES:(i + 1) * LANES]) * HEAD_DIM ** -0.5
            for j, h in enumerate(head_pair(r)):
                q_ref[4 * c + 2 * i + j] = jnp.where(low, h, 0.0).astype(q_ref.dtype)

    def kv_block(c0):
        kv = mm(c0, 2 * KV_WIDTH)
        return rope(kv[:, :KV_WIDTH]), kv[:, KV_WIDTH:]

    conv_block(0)
    q_block(0)
    conv_block(MXU_COLS)
    q_block(1)
    kc, vc = kv_block(_OFF_KC)
    split_heads(kc, cv_ref, (0,))
    split_heads(vc, cv_ref, (1,))
    k_sel, v_sel = kv_block(_OFF_KS)
    pos = pl.program_id(1) * tm + lax.broadcasted_iota(jnp.int32, (tm, LANES), 0)
    block_tag = jnp.where(pos // SLC_LEN == lane - HEAD_DIM, NEG, 0.0)
    for g, h in enumerate(head_pair(k_sel)):
        ks_ref[g] = jnp.where(low, h, block_tag).astype(ks_ref.dtype)
    for g, h in enumerate(head_pair(v_sel)):
        vs_ref[g] = jnp.where(low, h, 1.0).astype(vs_ref.dtype)
    k_win, v_win = kv_block(_OFF_KW)
    split_heads(k_win, kw_ref, ())
    for g, h in enumerate(head_pair(v_win)):
        vw_ref[g] = jnp.where(low, h, 1.0).astype(vw_ref.dtype)
    sg = jax.nn.sigmoid(mm(_OFF_GATE, LANES))
    per_group = Q_PER_KV * N_GATES
    for g in range(N_KV_GROUPS):
        gate_ref[g] = sg[:, g * per_group:(g + 1) * per_group]


def _inproj(x, w_in_p, conv_w, cos_t, sa_t, sb_t, hg_conv, tm):
    B, T, D = x.shape
    kv_shape = jax.ShapeDtypeStruct((B, N_KV_GROUPS, T, HEAD_DIM), MXU_DTYPE)
    kv_spec = pl.BlockSpec((None, N_KV_GROUPS, tm, HEAD_DIM), lambda b, i: (b, 0, i, 0))
    wide_shape = jax.ShapeDtypeStruct((B, N_KV_GROUPS, T, LANES), MXU_DTYPE)
    wide_spec = pl.BlockSpec((None, N_KV_GROUPS, tm, LANES), lambda b, i: (b, 0, i, 0))
    tab_spec = pl.BlockSpec((tm, LANES), lambda b, i: (i, 0))
    const2 = lambda b, i: (0, 0)
    per_group = Q_PER_KV * N_GATES
    return pl.pallas_call(
        _inproj_kernel,
        grid=(B, T // tm),
        in_specs=[
            pl.BlockSpec((None, tm, D), lambda b, i: (b, i, 0)),
            pl.BlockSpec((D, _IN_COLS_PAD), const2),
            pl.BlockSpec((3, CONV_WIDTH), const2),
            tab_spec, tab_spec, tab_spec,
            pl.BlockSpec((1, CONV_WIDTH), const2),
        ],
        out_specs=[
            pl.BlockSpec((None, tm, CONV_WIDTH), lambda b, i: (b, i, 0)),
            pl.BlockSpec((None, N_Q_HEADS, tm, LANES), lambda b, i: (b, 0, i, 0)),
            pl.BlockSpec((2, None, N_KV_GROUPS, tm, HEAD_DIM), lambda b, i: (0, b, 0, i, 0)),
            wide_spec, wide_spec, kv_spec, wide_spec,
            pl.BlockSpec((None, N_KV_GROUPS, tm, per_group), lambda b, i: (b, 0, i, 0)),
        ],
        out_shape=[
            jax.ShapeDtypeStruct((B, T, CONV_WIDTH), MXU_DTYPE),
            jax.ShapeDtypeStruct((B, N_Q_HEADS, T, LANES), MXU_DTYPE),
            jax.ShapeDtypeStruct((2, B, N_KV_GROUPS, T, HEAD_DIM), MXU_DTYPE),
            wide_shape, wide_shape, kv_shape, wide_shape,
            jax.ShapeDtypeStruct((B, N_KV_GROUPS, T, per_group), F32),
        ],
        scratch_shapes=[pltpu.VMEM((tm + 2 * SUBLANES, CONV_WIDTH), F32)],
        compiler_params=pltpu.CompilerParams(
            dimension_semantics=("arbitrary", "arbitrary"), vmem_limit_bytes=VMEM_LIMIT),
    )(x, w_in_p, conv_w, cos_t, sa_t, sb_t, hg_conv)


def _compress_kernel(sub_ref, pe_ref, w1_ref, b1_ref, w2_ref, b2_ref, out_ref, bbuf):
    nsub = sub_ref.shape[0]
    half = CMP_STRIDE * HEAD_DIM
    sub = sub_ref[...]
    top = _dot(sub, w1_ref[0:half, :])
    bbuf[0:nsub, :] = _dot(sub, w1_ref[half:2 * half, :])
    bbuf[nsub:nsub + SUBLANES, :] = jnp.zeros((SUBLANES, CMP_HIDDEN), F32)
    pe_rows = jnp.broadcast_to(pe_ref[...], (SUBLANES, 2 * half)).astype(MXU_DTYPE)
    const = _dot(pe_rows, w1_ref[...])[0:1, :] + b1_ref[...]
    h = top + bbuf[1:nsub + 1, :] + const
    act = jax.nn.gelu(h)
    out_ref[...] = (_dot(act.astype(MXU_DTYPE), w2_ref[...]) + b2_ref[...]).astype(out_ref.dtype)


def _compress(sub, pe, w1, b1, w2, b2):
    _, BG, nsub, width = sub.shape
    sel = lambda k, n: (k, 0, 0)
    return pl.pallas_call(
        _compress_kernel,
        grid=(2, BG),
        in_specs=[
            pl.BlockSpec((None, None, nsub, width), lambda k, n: (k, n, 0, 0)),
            pl.BlockSpec((None, 1, 2 * width), sel),
            pl.BlockSpec((None, 2 * width, CMP_HIDDEN), sel),
            pl.BlockSpec((None, 1, CMP_HIDDEN), sel),
            pl.BlockSpec((None, CMP_HIDDEN, HEAD_DIM), sel),
            pl.BlockSpec((None, 1, HEAD_DIM), sel),
        ],
        out_specs=pl.BlockSpec((None, None, nsub, HEAD_DIM), lambda k, n: (k, n, 0, 0)),
        out_shape=jax.ShapeDtypeStruct((2, BG, nsub, HEAD_DIM), MXU_DTYPE),
        scratch_shapes=[pltpu.VMEM((nsub + SUBLANES, CMP_HIDDEN), F32)],
        compiler_params=pltpu.CompilerParams(
            dimension_semantics=("arbitrary", "arbitrary"), vmem_limit_bytes=VMEM_LIMIT),
    )(sub, pe, w1, b1, w2, b2)


def _topk_block_mask(score, topk):
    ns, tq = score.shape
    n_slabs = ns // SUBLANES
    slabs = [score[SUBLANES * j:SUBLANES * (j + 1), :] for j in range(n_slabs)]
    ranks = [jnp.zeros((SUBLANES, tq), F32) for _ in range(n_slabs)]
    local = lax.broadcasted_iota(jnp.int32, (SUBLANES, tq), 0)
    for i in range(ns):
        row = jnp.broadcast_to(score[i:i + 1, :], (SUBLANES, tq))
        for j in range(n_slabs):
            ge = jnp.where(row >= slabs[j], 1.0, 0.0)
            gt = jnp.where(row > slabs[j], 1.0, 0.0)
            if i < SUBLANES * j:
                ahead = ge
            elif i >= SUBLANES * (j + 1):
                ahead = gt
            else:
                ahead = jnp.where(local > i - SUBLANES * j, ge, gt)
            ranks[j] = ranks[j] + ahead
    rank = jnp.concatenate(ranks, axis=0)
    return jnp.where(rank < topk, 1.0, 0.0)


def _nsa_kernel(q_ref, kcmp_ref, vcmp_ref, ks_ref, vs_ref, kw_ref, vw_ref, gate_ref, aggt_ref, hg_ref,
                o_ref, sa_ref, sb_ref, acc_ref, part_ref, gsel_ref, *, tq, kc):
    R = Q_PER_KV
    rows = R * tq
    T = ks_ref.shape[0]
    ns = T // SLC_LEN
    ncp = kcmp_ref.shape[0]
    q0 = pl.program_id(2) * tq
    t_col = q0 + lax.broadcasted_iota(jnp.int32, (tq, 1), 0)
    q_pad = q_ref[...]
    q64 = q_pad.reshape(rows, LANES)[:, :HEAD_DIM]

    def per_head(a):
        return a.reshape(R, tq, a.shape[-1])

    half = tq // 2
    wl = WINDOW + half
    o_win_halves = []
    for h in range(2):
        t_half = t_col[h * half:(h + 1) * half]
        ws = pl.multiple_of(jnp.maximum(q0 + h * half - WINDOW, 0), half)
        wpos = ws + lax.broadcasted_iota(jnp.int32, (half, wl), 1)
        wbias = jnp.where((wpos <= t_half) & (wpos > t_half - WINDOW), 0.0, NEG)
        qh = per_head(q64)[:, h * half:(h + 1) * half, :].reshape(R * half, HEAD_DIM)
        s = _dot_nt(qh, kw_ref[pl.ds(ws, wl), :]).reshape(R, half, wl) + wbias[None]
        p = jnp.exp(s - jnp.max(s, axis=-1, keepdims=True))
        ow = _dot(p.reshape(R * half, wl).astype(MXU_DTYPE), vw_ref[pl.ds(ws, wl), :])
        o_win_halves.append(ow[:, :HEAD_DIM] / pltpu.roll(ow, HEAD_DIM, 1)[:, :HEAD_DIM])

    cmp_end = lax.broadcasted_iota(jnp.int32, (tq, ncp), 1) * CMP_STRIDE + (CMP_LEN - 1)
    cbias = jnp.where(cmp_end <= t_col, 0.0, NEG)
    s = per_head(_dot_nt(q64, kcmp_ref[...])) + cbias[None]
    p = jnp.exp(s - jnp.max(s, axis=-1, keepdims=True))
    l = jnp.sum(p, axis=-1, keepdims=True)
    p = p * jnp.where((t_col >= CMP_LEN - 1)[None], 1.0 / l, 0.0)
    o_cmp = _dot(p.reshape(rows, ncp).astype(MXU_DTYPE), vcmp_ref[...])

    psum = p[0] + p[1] + p[2] + p[3]
    p_hi = psum.astype(MXU_DTYPE)
    p_lo = (psum - p_hi.astype(F32)).astype(MXU_DTYPE)
    imp = _dot_nt(aggt_ref[...], p_hi) + _dot_nt(aggt_ref[...], p_lo)
    blk = lax.broadcasted_iota(jnp.int32, (ns, tq), 0)
    tb = (q0 + lax.broadcasted_iota(jnp.int32, (ns, tq), 1)) // SLC_LEN
    forced = (blk == 0) | (blk == tb) | (blk == tb - 1)
    score = jnp.where(forced, BIG, imp)
    score = jnp.where(blk <= tb, score, -BIG)
    unselected = 1.0 - _topk_block_mask(score, min(SLC_TOPK, ns))
    tag_rows = [jnp.zeros((HEAD_DIM, tq), F32), unselected]
    if ns < LANES - HEAD_DIM:
        tag_rows.append(jnp.zeros((LANES - HEAD_DIM - ns, tq), F32))
    tag = jnp.concatenate(tag_rows, axis=0).T.astype(MXU_DTYPE)
    lane = lax.broadcasted_iota(jnp.int32, (R, tq, LANES), 2)
    q_aug = jnp.where(lane < HEAD_DIM, q_pad, tag[None]).reshape(rows, LANES)

    def scores(c):
        return _dot_nt(q_aug, ks_ref[pl.ds(pl.multiple_of(c * kc, kc), kc), :])

    def causal_bias(c):
        kpos = c * kc + lax.broadcasted_iota(jnp.int32, (tq, kc), 1)
        return jnp.where(kpos <= t_col, 0.0, NEG)[None]

    def consume(s, c, m):
        m_new = jnp.maximum(m, jnp.max(s, axis=-1, keepdims=True))
        p = jnp.exp(s - m_new).astype(MXU_DTYPE)
        pv = _dot(p, vs_ref[pl.ds(pl.multiple_of(c * kc, kc), kc), :])
        acc_ref[...] = jnp.exp(m - m_new) * acc_ref[...] + pv
        return m_new

    n_pairs = q0 // (2 * kc)
    acc_ref[...] = jnp.zeros((rows, LANES), F32)
    sa_ref[...] = scores(0)

    g = gate_ref[...]
    for r in range(R):
        c = N_GATES * r
        hr = slice(r * tq, (r + 1) * tq)
        hh = slice(r * half, (r + 1) * half)
        o_win = jnp.concatenate([o_win_halves[0][hh], o_win_halves[1][hh]], axis=0)
        part_ref[hr, :] = g[:, c:c + 1] * o_cmp[hr] + g[:, c + 2:c + 3] * o_win
        gsel_ref[hr, :] = jnp.broadcast_to(g[:, c + 1:c + 2], (tq, HEAD_DIM))

    def pair_step(j, m):
        sb_ref[...] = scores(2 * j + 1)
        m = consume(sa_ref[...], 2 * j, m)
        sa_ref[...] = scores(2 * j + 2)
        return consume(sb_ref[...], 2 * j + 1, m)

    m = lax.fori_loop(0, n_pairs, pair_step, jnp.full((rows, 1), NEG, F32))
    c_last = 2 * n_pairs
    m = consume((per_head(sa_ref[...]) + causal_bias(c_last)).reshape(rows, kc), c_last, m)

    @pl.when(q0 + tq > (c_last + 1) * kc)
    def _():
        consume((per_head(scores(c_last + 1)) + causal_bias(c_last + 1)).reshape(rows, kc), c_last + 1, m)

    acc = acc_ref[...]
    o_slc = acc[:, :HEAD_DIM] / pltpu.roll(acc, HEAD_DIM, 1)[:, :HEAD_DIM]

    o = part_ref[...] + gsel_ref[...] * o_slc
    for r in range(R):
        sl = slice(r * HEAD_DIM, (r + 1) * HEAD_DIM)
        o_ref[:, sl] = _group_rmsnorm(o[r * tq:(r + 1) * tq], hg_ref[:, sl]).astype(o_ref.dtype)


def _nsa(q, cmp_kv, ks, vs, kw, vw, gates, agg_t, hg_attn, tq, kc):
    B, _, T, _ = q.shape
    G, R = N_KV_GROUPS, Q_PER_KV
    ncp = cmp_kv.shape[2]
    ns = T // SLC_LEN
    assert T % (2 * kc) == 0 and kc % tq == 0 and WINDOW % (tq // 2) == 0 and T >= WINDOW + tq
    assert ns <= LANES - HEAD_DIM, "block tags must fit in the upper lanes of a key row"
    kv_spec = pl.BlockSpec((None, None, T, HEAD_DIM), lambda b, g, i: (b, g, 0, 0))
    wide_spec = pl.BlockSpec((None, None, T, LANES), lambda b, g, i: (b, g, 0, 0))
    return pl.pallas_call(
        functools.partial(_nsa_kernel, tq=tq, kc=kc),
        grid=(B, G, T // tq),
        in_specs=[
            pl.BlockSpec((None, R, tq, LANES), lambda b, g, i: (b, g, i, 0)),
            pl.BlockSpec((None, None, ncp, HEAD_DIM), lambda b, g, i: (0, b * G + g, 0, 0)),
            pl.BlockSpec((None, None, ncp, HEAD_DIM), lambda b, g, i: (1, b * G + g, 0, 0)),
            wide_spec, wide_spec, kv_spec, wide_spec,
            pl.BlockSpec((None, None, tq, R * N_GATES), lambda b, g, i: (b, g, i, 0)),
            pl.BlockSpec((ns, ncp), lambda b, g, i: (0, 0)),
            pl.BlockSpec((None, 1, R * HEAD_DIM), lambda b, g, i: (g, 0, 0)),
        ],
        out_specs=pl.BlockSpec((None, tq, R * HEAD_DIM), lambda b, g, i: (b, i, g)),
        out_shape=jax.ShapeDtypeStruct((B, T, ATTN_WIDTH), MXU_DTYPE),
        scratch_shapes=[pltpu.VMEM((R * tq, kc), F32), pltpu.VMEM((R * tq, kc), F32),
                        pltpu.VMEM((R * tq, LANES), F32),
                        pltpu.VMEM((R * tq, HEAD_DIM), F32), pltpu.VMEM((R * tq, HEAD_DIM), F32)],
        compiler_params=pltpu.CompilerParams(
            dimension_semantics=("arbitrary", "arbitrary", "arbitrary"), vmem_limit_bytes=VMEM_LIMIT),
    )(q, cmp_kv, cmp_kv, ks, vs, kw, vw, gates, agg_t, hg_attn)


def _outproj_kernel(conv_ref, attn_ref, x_ref, w_ref, g_ref, b_ref, o_ref):
    mix = _dot(conv_ref[...], w_ref[0:CONV_WIDTH, :]) + _dot(attn_ref[...], w_ref[CONV_WIDTH:, :])
    o_ref[...] = _layer_norm(ALPHA * x_ref[...] + mix, g_ref[...], b_ref[...])


def _outproj(conv_n, attn_n, x, w_out, g, b, tm):
    N, D = x.shape
    row = lambda i: (i, 0)
    const = lambda i: (0, 0)
    return pl.pallas_call(
        _outproj_kernel,
        grid=(N // tm,),
        in_specs=[
            pl.BlockSpec((tm, CONV_WIDTH), row), pl.BlockSpec((tm, ATTN_WIDTH), row),
            pl.BlockSpec((tm, D), row), pl.BlockSpec((D, D), const),
            pl.BlockSpec((1, D), const), pl.BlockSpec((1, D), const),
        ],
        out_specs=pl.BlockSpec((tm, D), row),
        out_shape=jax.ShapeDtypeStruct((N, D), F32),
        compiler_params=pltpu.CompilerParams(
            dimension_semantics=("arbitrary",), vmem_limit_bytes=VMEM_LIMIT),
    )(conv_n, attn_n, x, w_out, g, b)


def _ffn_kernel(x_ref, halo_ref, wup_ref, cw_ref, cb_ref, wdn_ref, g_ref, b_ref, o_ref, ubuf, act_ref,
                *, tiles_per_seq, fc):
    tm = x_ref.shape[0]
    x = x_ref[...]
    first = (pl.program_id(0) % tiles_per_seq) == 0
    halo = jnp.where(first, 0.0, halo_ref[...])
    xb = jnp.concatenate([halo, x], axis=0).astype(MXU_DTYPE)

    def up_conv(col0, buf):
        buf[...] = _dot(xb, wup_ref[:, col0:col0 + fc])
        return (cw_ref[2:3, col0:col0 + fc] * buf[SUBLANES:SUBLANES + tm, :]
                + cw_ref[1:2, col0:col0 + fc] * buf[SUBLANES - 1:SUBLANES - 1 + tm, :]
                + cw_ref[0:1, col0:col0 + fc] * buf[SUBLANES - 2:SUBLANES - 2 + tm, :]
                + cb_ref[:, col0:col0 + fc])

    for c in range(D_FF // fc):
        val = up_conv(c * fc, ubuf.at[c % 2, 0])
        gate = up_conv(D_FF + c * fc, ubuf.at[c % 2, 1])
        act_ref[:, c * fc:(c + 1) * fc] = (gate * jax.nn.sigmoid(gate) * val).astype(act_ref.dtype)
    o_ref[...] = _layer_norm(ALPHA * x + _dot(act_ref[...], wdn_ref[...]), g_ref[...], b_ref[...])


def _ffn(x, w_up, conv_w, conv_b, w_down, g, b, tm, tiles_per_seq, fc):
    N, D = x.shape
    row = lambda i: (i, 0)
    const = lambda i: (0, 0)
    halo_blocks = tm // SUBLANES
    return pl.pallas_call(
        functools.partial(_ffn_kernel, tiles_per_seq=tiles_per_seq, fc=fc),
        grid=(N // tm,),
        in_specs=[
            pl.BlockSpec((tm, D), row),
            pl.BlockSpec((SUBLANES, D), lambda i: (jnp.maximum(i * halo_blocks - 1, 0), 0)),
            pl.BlockSpec((D, 2 * D_FF), const, pipeline_mode=pl.Buffered(1)),
            pl.BlockSpec((3, 2 * D_FF), const),
            pl.BlockSpec((1, 2 * D_FF), const),
            pl.BlockSpec((D_FF, D), const, pipeline_mode=pl.Buffered(1)),
            pl.BlockSpec((1, D), const), pl.BlockSpec((1, D), const),
        ],
        out_specs=pl.BlockSpec((tm, D), row),
        out_shape=jax.ShapeDtypeStruct((N, D), F32),
        scratch_shapes=[pltpu.VMEM((2, 2, tm + SUBLANES, fc), F32), pltpu.VMEM((tm, D_FF), MXU_DTYPE)],
        compiler_params=pltpu.CompilerParams(
            dimension_semantics=("arbitrary",), vmem_limit_bytes=VMEM_LIMIT),
    )(x, x, w_up, conv_w, conv_b, w_down, g, b)


def _rope_lane_tables(T):
    half = ROPE_DIM // 2
    inv_freq = ROPE_THETA ** (-(np.arange(half, dtype=np.float32) * 2.0 / ROPE_DIM))
    ang = jnp.arange(T, dtype=F32)[:, None] * jnp.asarray(inv_freq, F32)[None]
    cos, sin = jnp.cos(ang), jnp.sin(ang)
    rest = HEAD_DIM - ROPE_DIM
    one, zero = jnp.ones((T, rest), F32), jnp.zeros((T, rest), F32)
    zh = jnp.zeros((T, half), F32)
    cos_t = jnp.concatenate([cos, cos, one], -1)
    sa_t = jnp.concatenate([-sin, zh, zero], -1)
    sb_t = jnp.concatenate([zh, sin, zero], -1)
    rep = LANES // HEAD_DIM
    return tuple(jnp.tile(t, (1, rep)) for t in (cos_t, sa_t, sb_t))


def _slc_aggregation_t(T):
    nc = T // CMP_STRIDE - CMP_LEN // CMP_STRIDE + 1
    ns = T // SLC_LEN
    sc = np.arange(nc)[None, :] * CMP_STRIDE
    ss = np.arange(ns)[:, None] * SLC_LEN
    ov = np.clip(np.minimum(sc + CMP_LEN, ss + SLC_LEN) - np.maximum(sc, ss), 0, None) / CMP_LEN
    ov = np.pad(ov, ((0, 0), (0, T // CMP_STRIDE - nc)))
    return jnp.asarray(ov, dtype=MXU_DTYPE)


def kernel(x, w_in, short_conv_w, cmp_pe, cmp_w1, cmp_b1, cmp_w2, cmp_b2, head_norm_g, w_out, ln1_g, ln1_b,
           w_up, ffn_conv_w, ffn_conv_b, w_down, ln2_g, ln2_b):
    B, T, D = x.shape
    tm = min(512, T)
    cos_t, sa_t, sb_t = _rope_lane_tables(T)
    agg_t = _slc_aggregation_t(T)
    nsub = T // CMP_STRIDE
    for i in range(DEPTH):
        w_in_p = jnp.pad(w_in[i], ((0, 0), (0, _IN_COLS_PAD - _IN_COLS))).astype(MXU_DTYPE)
        hg = head_norm_g[i]
        conv_n, q, cv, ks, vs, kw, vw, gates = _inproj(
            x, w_in_p, short_conv_w[i], cos_t, sa_t, sb_t, hg[None, :CONV_WIDTH], tm)
        sub = cv.reshape(2, B * N_KV_GROUPS, nsub, CMP_STRIDE * HEAD_DIM)
        cmp_kv = _compress(
            sub, cmp_pe[i].reshape(2, 1, CMP_LEN * HEAD_DIM), cmp_w1[i].astype(MXU_DTYPE),
            cmp_b1[i][:, None, :], cmp_w2[i].astype(MXU_DTYPE), cmp_b2[i][:, None, :])
        attn_n = _nsa(q, cmp_kv, ks, vs, kw, vw, gates, agg_t,
                      hg[CONV_WIDTH:].reshape(N_KV_GROUPS, 1, Q_PER_KV * HEAD_DIM), tq=256, kc=512)
        x2 = _outproj(conv_n.reshape(B * T, CONV_WIDTH), attn_n.reshape(B * T, ATTN_WIDTH),
                      x.reshape(B * T, D), w_out[i].astype(MXU_DTYPE), ln1_g[i][None], ln1_b[i][None], tm)
        x2 = _ffn(x2, w_up[i].astype(MXU_DTYPE), ffn_conv_w[i], ffn_conv_b[i][None],
                  w_down[i].astype(MXU_DTYPE), ln2_g[i][None], ln2_b[i][None], tm, T // tm, 256)
        x = x2.reshape(B, T, D)
    return x
```

```python
import functools

import numpy as np
import jax
import jax.numpy as jnp
from jax import lax
from jax.experimental import pallas as pl
from jax.experimental.pallas import tpu as pltpu

D_MODEL = 1024
DEPTH = 2
HEAD_DIM = 64
CONV_WIDTH = 512
ATTN_WIDTH = 512
N_Q_HEADS = 8
N_KV_GROUPS = 2
Q_PER_KV = 4
KV_WIDTH = N_KV_GROUPS * HEAD_DIM
N_GATES = 3
CMP_LEN = 32
CMP_STRIDE = 16
CMP_HIDDEN = 256
SLC_LEN = 64
SLC_TOPK = 16
WINDOW = 512
ROPE_THETA = 500000.0
ROPE_DIM = 16
D_FF = 2816
ALPHA = (2.0 * DEPTH) ** 0.25
NEG = -1e30
BIG = 1e9

F32 = jnp.float32
MXU_DTYPE = jnp.bfloat16
LANES = 128
SUBLANES = 8
MXU_COLS = 256
VMEM_LIMIT = 56 * 1024 * 1024

_W_VS, _W_KW, _W_VW, _W_GATE, _W_END = 2432, 2560, 2688, 2816, 2840
_OFF_B, _OFF_C, _OFF_H, _OFF_Q, _OFF_KC, _OFF_KSW = 0, 512, 1024, 1536, 2048, 2304
_MAIN_COLS = 2560
_GATE_ROWS = 16
_ROW_VS, _ROW_VW, _ROW_GATE = 0, KV_WIDTH, 2 * KV_WIDTH
_T_ROWS = 2 * KV_WIDTH + N_KV_GROUPS * _GATE_ROWS

_NT = (((1,), (1,)), ((), ()))


def _dot(a, b):
    return jnp.dot(a, b, preferred_element_type=F32)


def _dot_nt(a, b):
    return lax.dot_general(a, b, _NT, preferred_element_type=F32)


def _layer_norm(y, g, b):
    mu = jnp.mean(y, axis=-1, keepdims=True)
    d = y - mu
    var = jnp.mean(d * d, axis=-1, keepdims=True)
    return d * lax.rsqrt(var + 1e-5) * g + b


def _group_rmsnorm(seg, g):
    ms = jnp.mean(seg * seg, axis=-1, keepdims=True)
    return seg * lax.rsqrt(ms + 1e-6) * g


def _inproj_kernel(x_ref, w_ref, wt_ref, cw_ref, cos_ref, sa_ref, sb_ref, hg_ref,
                   conv_ref, q_ref, cv_ref, ks_ref, kw_ref, vst_ref, vwt_ref, gate_ref, zbuf):
    tm = x_ref.shape[0]
    xb = x_ref[...].astype(MXU_DTYPE)

    def mm(c0, width):
        return _dot(xb, w_ref[:, c0:c0 + width])

    @pl.when(pl.program_id(1) == 0)
    def _():
        zbuf[0:SUBLANES, :] = jnp.zeros((SUBLANES, CONV_WIDTH), F32)

    def conv_block(c0):
        cs = slice(c0, c0 + MXU_COLS)
        z = mm(_OFF_C + c0, MXU_COLS) * mm(_OFF_H + c0, MXU_COLS)
        zbuf[SUBLANES:SUBLANES + tm, cs] = z
        conv = (cw_ref[2:3, cs] * z + cw_ref[1:2, cs] * zbuf[SUBLANES - 1:SUBLANES - 1 + tm, cs]
                + cw_ref[0:1, cs] * zbuf[SUBLANES - 2:SUBLANES - 2 + tm, cs])
        zbuf[0:SUBLANES, cs] = zbuf[tm:tm + SUBLANES, cs]
        y = mm(_OFF_B + c0, MXU_COLS) * conv
        for j in range(MXU_COLS // HEAD_DIM):
            sl = slice(c0 + j * HEAD_DIM, c0 + (j + 1) * HEAD_DIM)
            seg = y[:, j * HEAD_DIM:(j + 1) * HEAD_DIM]
            conv_ref[:, sl] = _group_rmsnorm(seg, hg_ref[:, sl]).astype(conv_ref.dtype)

    def rope(v):
        return (v * cos_ref[...] + pltpu.roll(v, LANES - ROPE_DIM // 2, 1) * sa_ref[...]
                + pltpu.roll(v, ROPE_DIM // 2, 1) * sb_ref[...])

    def split_heads(v, ref, lead):
        for g in range(N_KV_GROUPS):
            ref[lead + (g,)] = v[:, g * HEAD_DIM:(g + 1) * HEAD_DIM].astype(ref.dtype)

    lane = lax.broadcasted_iota(jnp.int32, (tm, LANES), 1)
    low = lane < HEAD_DIM

    def head_pair(v):
        return v, pltpu.roll(v, HEAD_DIM, 1)

    def q_block(c):
        r4 = mm(_OFF_Q + c * MXU_COLS, MXU_COLS)
        for i in range(MXU_COLS // LANES):
            r = rope(r4[:, i * LANES:(i + 1) * LANES]) * HEAD_DIM ** -0.5
            for j, h in enumerate(head_pair(r)):
                q_ref[4 * c + 2 * i + j] = jnp.where(low, h, 0.0).astype(q_ref.dtype)

    conv_block(0)
    q_block(0)
    conv_block(MXU_COLS)
    q_block(1)
    kvc = mm(_OFF_KC, 2 * KV_WIDTH)
    split_heads(rope(kvc[:, :KV_WIDTH]), cv_ref, (0,))
    split_heads(kvc[:, KV_WIDTH:], cv_ref, (1,))
    ksw = mm(_OFF_KSW, 2 * KV_WIDTH)
    pos = pl.program_id(1) * tm + lax.broadcasted_iota(jnp.int32, (tm, LANES), 0)
    block_tag = jnp.where(pos // SLC_LEN == lane - HEAD_DIM, NEG, 0.0)
    for g, h in enumerate(head_pair(rope(ksw[:, :KV_WIDTH]))):
        ks_ref[g] = jnp.where(low, h, block_tag).astype(ks_ref.dtype)
    split_heads(rope(ksw[:, KV_WIDTH:]), kw_ref, ())

    vt = _dot_nt(wt_ref[...], xb)
    ones = jnp.ones((HEAD_DIM, tm), vst_ref.dtype)
    for g in range(N_KV_GROUPS):
        for ref, row0 in ((vst_ref, _ROW_VS), (vwt_ref, _ROW_VW)):
            ref[g, 0:HEAD_DIM, :] = vt[row0 + g * HEAD_DIM:row0 + (g + 1) * HEAD_DIM].astype(ref.dtype)
            ref[g, HEAD_DIM:2 * HEAD_DIM, :] = ones
        gate_ref[g] = jax.nn.sigmoid(vt[_ROW_GATE + g * _GATE_ROWS:_ROW_GATE + (g + 1) * _GATE_ROWS])


def _inproj(x, w_main, w_t, conv_w, cos_t, sa_t, sb_t, hg_conv, tm):
    B, T, D = x.shape
    G = N_KV_GROUPS
    kv_shape = jax.ShapeDtypeStruct((B, G, T, HEAD_DIM), MXU_DTYPE)
    kv_spec = pl.BlockSpec((None, G, tm, HEAD_DIM), lambda b, i: (b, 0, i, 0))
    vt_shape = jax.ShapeDtypeStruct((B, G, 2 * HEAD_DIM, T), MXU_DTYPE)
    vt_spec = pl.BlockSpec((None, G, 2 * HEAD_DIM, tm), lambda b, i: (b, 0, 0, i))
    tab_spec = pl.BlockSpec((tm, LANES), lambda b, i: (i, 0))
    const2 = lambda b, i: (0, 0)
    return pl.pallas_call(
        _inproj_kernel,
        grid=(B, T // tm),
        in_specs=[
            pl.BlockSpec((None, tm, D), lambda b, i: (b, i, 0)),
            pl.BlockSpec((D, _MAIN_COLS), const2),
            pl.BlockSpec((_T_ROWS, D), const2),
            pl.BlockSpec((3, CONV_WIDTH), const2),
            tab_spec, tab_spec, tab_spec,
            pl.BlockSpec((1, CONV_WIDTH), const2),
        ],
        out_specs=[
            pl.BlockSpec((None, tm, CONV_WIDTH), lambda b, i: (b, i, 0)),
            pl.BlockSpec((None, N_Q_HEADS, tm, LANES), lambda b, i: (b, 0, i, 0)),
            pl.BlockSpec((2, None, G, tm, HEAD_DIM), lambda b, i: (0, b, 0, i, 0)),
            pl.BlockSpec((None, G, tm, LANES), lambda b, i: (b, 0, i, 0)),
            kv_spec, vt_spec, vt_spec,
            pl.BlockSpec((None, G, _GATE_ROWS, tm), lambda b, i: (b, 0, 0, i)),
        ],
        out_shape=[
            jax.ShapeDtypeStruct((B, T, CONV_WIDTH), MXU_DTYPE),
            jax.ShapeDtypeStruct((B, N_Q_HEADS, T, LANES), MXU_DTYPE),
            jax.ShapeDtypeStruct((2, B, G, T, HEAD_DIM), MXU_DTYPE),
            jax.ShapeDtypeStruct((B, G, T, LANES), MXU_DTYPE),
            kv_shape, vt_shape, vt_shape,
            jax.ShapeDtypeStruct((B, G, _GATE_ROWS, T), F32),
        ],
        scratch_shapes=[pltpu.VMEM((tm + 2 * SUBLANES, CONV_WIDTH), F32)],
        compiler_params=pltpu.CompilerParams(
            dimension_semantics=("arbitrary", "arbitrary"), vmem_limit_bytes=VMEM_LIMIT),
    )(x, w_main, w_t, conv_w, cos_t, sa_t, sb_t, hg_conv)


def _compress_kernel(sub_ref, pe_ref, w1_ref, b1_ref, w2_ref, b2_ref, w2t_ref, b2t_ref, out_ref, outt_ref, bbuf):
    nsub = sub_ref.shape[0]
    half = CMP_STRIDE * HEAD_DIM
    sub = sub_ref[...]
    top = _dot(sub, w1_ref[0:half, :])
    bbuf[0:nsub, :] = _dot(sub, w1_ref[half:2 * half, :])
    bbuf[nsub:nsub + SUBLANES, :] = jnp.zeros((SUBLANES, CMP_HIDDEN), F32)
    pe_rows = jnp.broadcast_to(pe_ref[...], (SUBLANES, 2 * half)).astype(MXU_DTYPE)
    const = _dot(pe_rows, w1_ref[...])[0:1, :] + b1_ref[...]
    h = top + bbuf[1:nsub + 1, :] + const
    act = jax.nn.gelu(h).astype(MXU_DTYPE)
    out_ref[...] = (_dot(act, w2_ref[...]) + b2_ref[...]).astype(out_ref.dtype)
    outt_ref[...] = (_dot_nt(w2t_ref[...], act) + b2t_ref[...]).astype(outt_ref.dtype)


def _compress(sub, pe, w1, b1, w2, b2):
    _, BG, nsub, width = sub.shape
    sel = lambda k, n: (k, 0, 0)
    per = lambda k, n: (k, n, 0, 0)
    return pl.pallas_call(
        _compress_kernel,
        grid=(2, BG),
        in_specs=[
            pl.BlockSpec((None, None, nsub, width), per),
            pl.BlockSpec((None, 1, 2 * width), sel),
            pl.BlockSpec((None, 2 * width, CMP_HIDDEN), sel),
            pl.BlockSpec((None, 1, CMP_HIDDEN), sel),
            pl.BlockSpec((None, CMP_HIDDEN, HEAD_DIM), sel),
            pl.BlockSpec((None, 1, HEAD_DIM), sel),
            pl.BlockSpec((None, HEAD_DIM, CMP_HIDDEN), sel),
            pl.BlockSpec((None, HEAD_DIM, 1), sel),
        ],
        out_specs=[pl.BlockSpec((None, None, nsub, HEAD_DIM), per),
                   pl.BlockSpec((None, None, HEAD_DIM, nsub), per)],
        out_shape=[jax.ShapeDtypeStruct((2, BG, nsub, HEAD_DIM), MXU_DTYPE),
                   jax.ShapeDtypeStruct((2, BG, HEAD_DIM, nsub), MXU_DTYPE)],
        scratch_shapes=[pltpu.VMEM((nsub + SUBLANES, CMP_HIDDEN), F32)],
        compiler_params=pltpu.CompilerParams(
            dimension_semantics=("arbitrary", "arbitrary"), vmem_limit_bytes=VMEM_LIMIT),
    )(sub, pe, w1, b1[:, None, :], w2, b2[:, None, :], jnp.swapaxes(w2, 1, 2), b2[:, :, None])


def _topk_block_mask(score, topk):
    ns, tq = score.shape
    n_slabs = ns // SUBLANES
    slabs = [score[SUBLANES * j:SUBLANES * (j + 1), :] for j in range(n_slabs)]
    ranks = [jnp.zeros((SUBLANES, tq), F32) for _ in range(n_slabs)]
    local = lax.broadcasted_iota(jnp.int32, (SUBLANES, tq), 0)
    for i in range(ns):
        row = jnp.broadcast_to(score[i:i + 1, :], (SUBLANES, tq))
        for j in range(n_slabs):
            ge = jnp.where(row >= slabs[j], 1.0, 0.0)
            gt = jnp.where(row > slabs[j], 1.0, 0.0)
            if i < SUBLANES * j:
                ahead = ge
            elif i >= SUBLANES * (j + 1):
                ahead = gt
            else:
                ahead = jnp.where(local > i - SUBLANES * j, ge, gt)
            ranks[j] = ranks[j] + ahead
    rank = jnp.concatenate(ranks, axis=0)
    return jnp.where(rank < topk, 1.0, 0.0)


def _nsa_kernel(q_ref, kcmp_ref, vcmpt_ref, ks_ref, vst_ref, kw_ref, vwt_ref, gate_ref, aggt_ref, hg_ref,
                o_ref, sa_ref, sb_ref, acc_ref, *, tq, kc):
    R = Q_PER_KV
    rows = R * tq
    T = ks_ref.shape[0]
    ns = T // SLC_LEN
    ncp = kcmp_ref.shape[0]
    q0 = pl.program_id(2) * tq
    t_row = q0 + lax.broadcasted_iota(jnp.int32, (1, tq), 1)
    q_pad = q_ref[...]
    q64 = q_pad.reshape(rows, LANES)[:, :HEAD_DIM]

    def all_heads(a):
        return jnp.concatenate([a] * R, axis=1)

    def head(a, r, width):
        return a[:, r * width:(r + 1) * width]

    half = tq // 2
    wl = WINDOW + half
    o_win_halves = []
    for h in range(2):
        t_half = t_row[:, h * half:(h + 1) * half]
        ws = pl.multiple_of(jnp.maximum(q0 + h * half - WINDOW, 0), half)
        wpos = ws + lax.broadcasted_iota(jnp.int32, (wl, half), 0)
        wbias = jnp.where((wpos <= t_half) & (wpos > t_half - WINDOW), 0.0, NEG)
        qh = q64.reshape(R, tq, HEAD_DIM)[:, h * half:(h + 1) * half, :].reshape(R * half, HEAD_DIM)
        s = _dot_nt(kw_ref[pl.ds(ws, wl), :], qh) + jnp.concatenate([wbias] * R, axis=1)
        p = jnp.exp(s - jnp.max(s, axis=0, keepdims=True)).astype(MXU_DTYPE)
        ow = _dot(vwt_ref[:, pl.ds(ws, wl)], p)
        o_win_halves.append(ow[:HEAD_DIM] / ow[HEAD_DIM:])

    cmp_end = lax.broadcasted_iota(jnp.int32, (ncp, tq), 0) * CMP_STRIDE + (CMP_LEN - 1)
    cbias = jnp.where(cmp_end <= t_row, 0.0, NEG)
    s = _dot_nt(kcmp_ref[...], q64) + all_heads(cbias)
    p = jnp.exp(s - jnp.max(s, axis=0, keepdims=True))
    l = jnp.sum(p, axis=0, keepdims=True)
    p = p * (all_heads(jnp.where(t_row >= CMP_LEN - 1, 1.0, 0.0)) / l)
    o_cmp = _dot(vcmpt_ref[...], p.astype(MXU_DTYPE))

    psum = head(p, 0, tq) + head(p, 1, tq) + head(p, 2, tq) + head(p, 3, tq)
    p_hi = psum.astype(MXU_DTYPE)
    p_lo = (psum - p_hi.astype(F32)).astype(MXU_DTYPE)
    imp = _dot(aggt_ref[...], p_hi) + _dot(aggt_ref[...], p_lo)
    blk = lax.broadcasted_iota(jnp.int32, (ns, tq), 0)
    tb = (q0 + lax.broadcasted_iota(jnp.int32, (ns, tq), 1)) // SLC_LEN
    forced = (blk == 0) | (blk == tb) | (blk == tb - 1)
    score = jnp.where(forced, BIG, imp)
    score = jnp.where(blk <= tb, score, -BIG)
    unselected = 1.0 - _topk_block_mask(score, min(SLC_TOPK, ns))
    tag_rows = [jnp.zeros((HEAD_DIM, tq), F32), unselected]
    if ns < LANES - HEAD_DIM:
        tag_rows.append(jnp.zeros((LANES - HEAD_DIM - ns, tq), F32))
    tag = jnp.concatenate(tag_rows, axis=0).T.astype(MXU_DTYPE)
    lane = lax.broadcasted_iota(jnp.int32, (R, tq, LANES), 2)
    q_aug = jnp.where(lane < HEAD_DIM, q_pad, tag[None]).reshape(rows, LANES)

    def scores(c):
        return _dot_nt(ks_ref[pl.ds(pl.multiple_of(c * kc, kc), kc), :], q_aug)

    def causal_bias(c):
        kpos = c * kc + lax.broadcasted_iota(jnp.int32, (kc, tq), 0)
        return all_heads(jnp.where(kpos <= t_row, 0.0, NEG))

    def consume(s, c, m):
        m_new = jnp.maximum(m, jnp.max(s, axis=0, keepdims=True))
        p = jnp.exp(s - m_new).astype(MXU_DTYPE)
        pv = _dot(vst_ref[:, pl.ds(pl.multiple_of(c * kc, kc), kc)], p)
        acc_ref[...] = jnp.exp(m - m_new) * acc_ref[...] + pv
        return m_new

    n_pairs = q0 // (2 * kc)
    acc_ref[...] = jnp.zeros((2 * HEAD_DIM, rows), F32)
    sa_ref[...] = scores(0)

    def pair_step(j, m):
        sb_ref[...] = scores(2 * j + 1)
        m = consume(sa_ref[...], 2 * j, m)
        sa_ref[...] = scores(2 * j + 2)
        return consume(sb_ref[...], 2 * j + 1, m)

    m = lax.fori_loop(0, n_pairs, pair_step, jnp.full((1, rows), NEG, F32))
    c_last = 2 * n_pairs
    m = consume(sa_ref[...] + causal_bias(c_last), c_last, m)

    @pl.when(q0 + tq > (c_last + 1) * kc)
    def _():
        consume(scores(c_last + 1) + causal_bias(c_last + 1), c_last + 1, m)

    acc = acc_ref[...]
    o_slc = acc[:HEAD_DIM] / acc[HEAD_DIM:]

    g = gate_ref[...]
    outs = []
    for r in range(R):
        c = N_GATES * r
        o_win = jnp.concatenate([head(o_win_halves[0], r, half), head(o_win_halves[1], r, half)], axis=1)
        o = g[c:c + 1] * head(o_cmp, r, tq) + g[c + 1:c + 2] * head(o_slc, r, tq) + g[c + 2:c + 3] * o_win
        ms = jnp.mean(o * o, axis=0, keepdims=True)
        outs.append(o * lax.rsqrt(ms + 1e-6) * hg_ref[r * HEAD_DIM:(r + 1) * HEAD_DIM, :])
    o_ref[...] = jnp.concatenate(outs, axis=0).T.astype(o_ref.dtype)


def _nsa(q, cmp_kv, cmp_kv_t, ks, vst, kw, vwt, gates, agg_t, hg_attn, tq, kc):
    B, _, T, _ = q.shape
    G, R = N_KV_GROUPS, Q_PER_KV
    ncp = cmp_kv.shape[2]
    ns = T // SLC_LEN
    assert T % (2 * kc) == 0 and kc % tq == 0 and WINDOW % (tq // 2) == 0 and T >= WINDOW + tq
    assert ns <= LANES - HEAD_DIM, "block tags must fit in the upper lanes of a key row"
    whole = lambda b, g, i: (b, g, 0, 0)
    hg_lanes = jnp.broadcast_to(hg_attn.reshape(G, R * HEAD_DIM, 1), (G, R * HEAD_DIM, tq))
    return pl.pallas_call(
        functools.partial(_nsa_kernel, tq=tq, kc=kc),
        grid=(B, G, T // tq),
        in_specs=[
            pl.BlockSpec((None, R, tq, LANES), lambda b, g, i: (b, g, i, 0)),
            pl.BlockSpec((None, None, ncp, HEAD_DIM), lambda b, g, i: (0, b * G + g, 0, 0)),
            pl.BlockSpec((None, None, HEAD_DIM, ncp), lambda b, g, i: (1, b * G + g, 0, 0)),
            pl.BlockSpec((None, None, T, LANES), whole),
            pl.BlockSpec((None, None, 2 * HEAD_DIM, T), whole),
            pl.BlockSpec((None, None, T, HEAD_DIM), whole),
            pl.BlockSpec((None, None, 2 * HEAD_DIM, T), whole),
            pl.BlockSpec((None, None, _GATE_ROWS, tq), lambda b, g, i: (b, g, 0, i)),
            pl.BlockSpec((ns, ncp), lambda b, g, i: (0, 0)),
            pl.BlockSpec((None, R * HEAD_DIM, tq), lambda b, g, i: (g, 0, 0)),
        ],
        out_specs=pl.BlockSpec((None, tq, R * HEAD_DIM), lambda b, g, i: (b, i, g)),
        out_shape=jax.ShapeDtypeStruct((B, T, ATTN_WIDTH), MXU_DTYPE),
        scratch_shapes=[pltpu.VMEM((kc, R * tq), F32), pltpu.VMEM((kc, R * tq), F32),
                        pltpu.VMEM((2 * HEAD_DIM, R * tq), F32)],
        compiler_params=pltpu.CompilerParams(
            dimension_semantics=("arbitrary", "arbitrary", "arbitrary"), vmem_limit_bytes=VMEM_LIMIT),
    )(q, cmp_kv, cmp_kv_t, ks, vst, kw, vwt, gates, agg_t, hg_lanes)


def _outproj_kernel(conv_ref, attn_ref, x_ref, w_ref, g_ref, b_ref, o_ref):
    mix = _dot(conv_ref[...], w_ref[0:CONV_WIDTH, :]) + _dot(attn_ref[...], w_ref[CONV_WIDTH:, :])
    o_ref[...] = _layer_norm(ALPHA * x_ref[...] + mix, g_ref[...], b_ref[...])


def _outproj(conv_n, attn_n, x, w_out, g, b, tm):
    N, D = x.shape
    row = lambda i: (i, 0)
    const = lambda i: (0, 0)
    return pl.pallas_call(
        _outproj_kernel,
        grid=(N // tm,),
        in_specs=[
            pl.BlockSpec((tm, CONV_WIDTH), row), pl.BlockSpec((tm, ATTN_WIDTH), row),
            pl.BlockSpec((tm, D), row), pl.BlockSpec((D, D), const),
            pl.BlockSpec((1, D), const), pl.BlockSpec((1, D), const),
        ],
        out_specs=pl.BlockSpec((tm, D), row),
        out_shape=jax.ShapeDtypeStruct((N, D), F32),
        compiler_params=pltpu.CompilerParams(
            dimension_semantics=("arbitrary",), vmem_limit_bytes=VMEM_LIMIT),
    )(conv_n, attn_n, x, w_out, g, b)


def _ffn_kernel(x_ref, halo_ref, wup_ref, cw_ref, cb_ref, wdn_ref, g_ref, b_ref, o_ref, ubuf, act_ref,
                *, tiles_per_seq, fc):
    tm = x_ref.shape[0]
    x = x_ref[...]
    first = (pl.program_id(0) % tiles_per_seq) == 0
    halo = jnp.where(first, 0.0, halo_ref[...])
    xb = jnp.concatenate([halo, x], axis=0).astype(MXU_DTYPE)

    def up_conv(col0, buf):
        buf[...] = _dot(xb, wup_ref[:, col0:col0 + fc])
        return (cw_ref[2:3, col0:col0 + fc] * buf[SUBLANES:SUBLANES + tm, :]
                + cw_ref[1:2, col0:col0 + fc] * buf[SUBLANES - 1:SUBLANES - 1 + tm, :]
                + cw_ref[0:1, col0:col0 + fc] * buf[SUBLANES - 2:SUBLANES - 2 + tm, :]
                + cb_ref[:, col0:col0 + fc])

    for c in range(D_FF // fc):
        val = up_conv(c * fc, ubuf.at[c % 2, 0])
        gate = up_conv(D_FF + c * fc, ubuf.at[c % 2, 1])
        act_ref[:, c * fc:(c + 1) * fc] = (gate * jax.nn.sigmoid(gate) * val).astype(act_ref.dtype)
    o_ref[...] = _layer_norm(ALPHA * x + _dot(act_ref[...], wdn_ref[...]), g_ref[...], b_ref[...])


def _ffn(x, w_up, conv_w, conv_b, w_down, g, b, tm, tiles_per_seq, fc):
    N, D = x.shape
    row = lambda i: (i, 0)
    const = lambda i: (0, 0)
    halo_blocks = tm // SUBLANES
    return pl.pallas_call(
        functools.partial(_ffn_kernel, tiles_per_seq=tiles_per_seq, fc=fc),
        grid=(N // tm,),
        in_specs=[
            pl.BlockSpec((tm, D), row),
            pl.BlockSpec((SUBLANES, D), lambda i: (jnp.maximum(i * halo_blocks - 1, 0), 0)),
            pl.BlockSpec((D, 2 * D_FF), const, pipeline_mode=pl.Buffered(1)),
            pl.BlockSpec((3, 2 * D_FF), const),
            pl.BlockSpec((1, 2 * D_FF), const),
            pl.BlockSpec((D_FF, D), const, pipeline_mode=pl.Buffered(1)),
            pl.BlockSpec((1, D), const), pl.BlockSpec((1, D), const),
        ],
        out_specs=pl.BlockSpec((tm, D), row),
        out_shape=jax.ShapeDtypeStruct((N, D), F32),
        scratch_shapes=[pltpu.VMEM((2, 2, tm + SUBLANES, fc), F32), pltpu.VMEM((tm, D_FF), MXU_DTYPE)],
        compiler_params=pltpu.CompilerParams(
            dimension_semantics=("arbitrary",), vmem_limit_bytes=VMEM_LIMIT),
    )(x, x, w_up, conv_w, conv_b, w_down, g, b)


def _rope_lane_tables(T):
    half = ROPE_DIM // 2
    inv_freq = ROPE_THETA ** (-(np.arange(half, dtype=np.float32) * 2.0 / ROPE_DIM))
    ang = jnp.arange(T, dtype=F32)[:, None] * jnp.asarray(inv_freq, F32)[None]
    cos, sin = jnp.cos(ang), jnp.sin(ang)
    rest = HEAD_DIM - ROPE_DIM
    one, zero = jnp.ones((T, rest), F32), jnp.zeros((T, rest), F32)
    zh = jnp.zeros((T, half), F32)
    cos_t = jnp.concatenate([cos, cos, one], -1)
    sa_t = jnp.concatenate([-sin, zh, zero], -1)
    sb_t = jnp.concatenate([zh, sin, zero], -1)
    rep = LANES // HEAD_DIM
    return tuple(jnp.tile(t, (1, rep)) for t in (cos_t, sa_t, sb_t))


def _slc_aggregation_t(T):
    nc = T // CMP_STRIDE - CMP_LEN // CMP_STRIDE + 1
    ns = T // SLC_LEN
    sc = np.arange(nc)[None, :] * CMP_STRIDE
    ss = np.arange(ns)[:, None] * SLC_LEN
    ov = np.clip(np.minimum(sc + CMP_LEN, ss + SLC_LEN) - np.maximum(sc, ss), 0, None) / CMP_LEN
    ov = np.pad(ov, ((0, 0), (0, T // CMP_STRIDE - nc)))
    return jnp.asarray(ov, dtype=MXU_DTYPE)


def _split_in_weights(w):
    w_main = jnp.concatenate([w[:, :_W_VS], w[:, _W_KW:_W_VW]], axis=1)
    per_group = Q_PER_KV * N_GATES
    pad = jnp.zeros((w.shape[0], _GATE_ROWS - per_group), w.dtype)
    cols = [w[:, _W_VS:_W_KW], w[:, _W_VW:_W_GATE]]
    for g in range(N_KV_GROUPS):
        cols += [w[:, _W_GATE + g * per_group:_W_GATE + (g + 1) * per_group], pad]
    return w_main.astype(MXU_DTYPE), jnp.concatenate(cols, axis=1).T.astype(MXU_DTYPE)


def kernel(x, w_in, short_conv_w, cmp_pe, cmp_w1, cmp_b1, cmp_w2, cmp_b2, head_norm_g, w_out, ln1_g, ln1_b,
           w_up, ffn_conv_w, ffn_conv_b, w_down, ln2_g, ln2_b):
    B, T, D = x.shape
    tm = min(512, T)
    cos_t, sa_t, sb_t = _rope_lane_tables(T)
    agg_t = _slc_aggregation_t(T)
    nsub = T // CMP_STRIDE
    for i in range(DEPTH):
        w_main, w_t = _split_in_weights(w_in[i])
        hg = head_norm_g[i]
        conv_n, q, cv, ks, kw, vst, vwt, gates = _inproj(
            x, w_main, w_t, short_conv_w[i], cos_t, sa_t, sb_t, hg[None, :CONV_WIDTH], tm)
        sub = cv.reshape(2, B * N_KV_GROUPS, nsub, CMP_STRIDE * HEAD_DIM)
        cmp_kv, cmp_kv_t = _compress(
            sub, cmp_pe[i].reshape(2, 1, CMP_LEN * HEAD_DIM), cmp_w1[i].astype(MXU_DTYPE),
            cmp_b1[i], cmp_w2[i].astype(MXU_DTYPE), cmp_b2[i])
        attn_n = _nsa(q, cmp_kv, cmp_kv_t, ks, vst, kw, vwt, gates, agg_t, hg[CONV_WIDTH:], tq=256, kc=512)
        x2 = _outproj(conv_n.reshape(B * T, CONV_WIDTH), attn_n.reshape(B * T, ATTN_WIDTH),
                      x.reshape(B * T, D), w_out[i].astype(MXU_DTYPE), ln1_g[i][None], ln1_b[i][None], tm)
        x2 = _ffn(x2, w_up[i].astype(MXU_DTYPE), ffn_conv_w[i], ffn_conv_b[i][None],
                  w_down[i].astype(MXU_DTYPE), ln2_g[i][None], ln2_b[i][None], tm, T // tm, 256)
        x = x2.reshape(B, T, D)
    return x
```

```python
import functools

import numpy as np
import jax
import jax.numpy as jnp
from jax import lax
from jax.experimental import pallas as pl
from jax.experimental.pallas import tpu as pltpu

D_MODEL = 1024
DEPTH = 2
HEAD_DIM = 64
CONV_WIDTH = 512
ATTN_WIDTH = 512
N_Q_HEADS = 8
N_KV_GROUPS = 2
Q_PER_KV = 4
KV_WIDTH = N_KV_GROUPS * HEAD_DIM
N_GATES = 3
CMP_LEN = 32
CMP_STRIDE = 16
CMP_HIDDEN = 256
SLC_LEN = 64
SLC_TOPK = 16
WINDOW = 512
ROPE_THETA = 500000.0
ROPE_DIM = 16
D_FF = 2816
ALPHA = (2.0 * DEPTH) ** 0.25
NEG = -1e30
BIG = 1e9

F32 = jnp.float32
MXU_DTYPE = jnp.bfloat16
LANES = 128
SUBLANES = 8
MXU_COLS = 256
VMEM_LIMIT = 56 * 1024 * 1024

_W_VS, _W_KW, _W_VW, _W_GATE, _W_END = 2432, 2560, 2688, 2816, 2840
_OFF_B, _OFF_C, _OFF_H, _OFF_Q, _OFF_KC, _OFF_KSW = 0, 512, 1024, 1536, 2048, 2304
_OFF_PQ, _OFF_PKSW, _OFF_PKC = 2560, 3072, 3328
_MAIN_COLS = 3456
LOG2E = 1.4426950408889634
_GATE_ROWS = 16
_ROW_VS, _ROW_VW, _ROW_GATE = 0, KV_WIDTH, 2 * KV_WIDTH
_T_ROWS = 2 * KV_WIDTH + N_KV_GROUPS * _GATE_ROWS

_NT = (((1,), (1,)), ((), ()))


def _dot(a, b):
    return jnp.dot(a, b, preferred_element_type=F32)


def _dot_nt(a, b):
    return lax.dot_general(a, b, _NT, preferred_element_type=F32)


def _layer_norm(y, g, b):
    mu = jnp.mean(y, axis=-1, keepdims=True)
    d = y - mu
    var = jnp.mean(d * d, axis=-1, keepdims=True)
    return d * lax.rsqrt(var + 1e-5) * g + b


def _inproj_kernel(x_ref, w_ref, wt_ref, cw_ref, cos_ref, rot_ref, hg_ref, gmean_ref,
                   conv_ref, q_ref, cv_ref, ks_ref, kw_ref, vst_ref, vwt_ref, gate_ref, zbuf):
    tm = x_ref.shape[0]
    xb = x_ref[...].astype(MXU_DTYPE)

    def mm(c0, width):
        return _dot(xb, w_ref[:, c0:c0 + width])

    @pl.when(pl.program_id(1) == 0)
    def _():
        zbuf[0:SUBLANES, :] = jnp.zeros((SUBLANES, CONV_WIDTH), F32)

    def conv_block(c0):
        cs = slice(c0, c0 + MXU_COLS)
        z = mm(_OFF_C + c0, MXU_COLS) * mm(_OFF_H + c0, MXU_COLS)
        zbuf[SUBLANES:SUBLANES + tm, cs] = z
        conv = (cw_ref[2:3, cs] * z + cw_ref[1:2, cs] * zbuf[SUBLANES - 1:SUBLANES - 1 + tm, cs]
                + cw_ref[0:1, cs] * zbuf[SUBLANES - 2:SUBLANES - 2 + tm, cs])
        zbuf[0:SUBLANES, cs] = zbuf[tm:tm + SUBLANES, cs]
        y = mm(_OFF_B + c0, MXU_COLS) * conv
        y2 = y * y
        y2_hi = y2.astype(MXU_DTYPE)
        y2_lo = (y2 - y2_hi.astype(F32)).astype(MXU_DTYPE)
        ms = _dot(y2_hi, gmean_ref[...]) + _dot(y2_lo, gmean_ref[...])
        conv_ref[:, cs] = (y * lax.rsqrt(ms + 1e-6) * hg_ref[:, cs]).astype(conv_ref.dtype)

    def rope(v, partner):
        return v * cos_ref[...] + partner * rot_ref[...]

    def split_heads(v, ref, lead):
        for g in range(N_KV_GROUPS):
            ref[lead + (g,)] = v[:, g * HEAD_DIM:(g + 1) * HEAD_DIM].astype(ref.dtype)

    lane = lax.broadcasted_iota(jnp.int32, (tm, LANES), 1)
    low = lane < HEAD_DIM

    def head_pair(v):
        return v, pltpu.roll(v, HEAD_DIM, 1)

    def q_block(c):
        r4 = mm(_OFF_Q + c * MXU_COLS, MXU_COLS)
        p4 = mm(_OFF_PQ + c * MXU_COLS, MXU_COLS)
        for i in range(MXU_COLS // LANES):
            sl = slice(i * LANES, (i + 1) * LANES)
            r = rope(r4[:, sl], p4[:, sl]) * (HEAD_DIM ** -0.5 * LOG2E)
            for j, h in enumerate(head_pair(r)):
                q_ref[4 * c + 2 * i + j] = jnp.where(low, h, 0.0).astype(q_ref.dtype)

    conv_block(0)
    q_block(0)
    conv_block(MXU_COLS)
    q_block(1)
    kvc = mm(_OFF_KC, 2 * KV_WIDTH)
    split_heads(rope(kvc[:, :KV_WIDTH], mm(_OFF_PKC, KV_WIDTH)), cv_ref, (0,))
    split_heads(kvc[:, KV_WIDTH:], cv_ref, (1,))
    ksw = mm(_OFF_KSW, 2 * KV_WIDTH)
    pksw = mm(_OFF_PKSW, 2 * KV_WIDTH)
    pos = pl.program_id(1) * tm + lax.broadcasted_iota(jnp.int32, (tm, LANES), 0)
    block_tag = jnp.where(pos // SLC_LEN == lane - HEAD_DIM, NEG, 0.0)
    for g, h in enumerate(head_pair(rope(ksw[:, :KV_WIDTH], pksw[:, :KV_WIDTH]))):
        ks_ref[g] = jnp.where(low, h, block_tag).astype(ks_ref.dtype)
    split_heads(rope(ksw[:, KV_WIDTH:], pksw[:, KV_WIDTH:]), kw_ref, ())

    vt = _dot_nt(wt_ref[...], xb)
    ones = jnp.ones((HEAD_DIM, tm), vst_ref.dtype)
    for g in range(N_KV_GROUPS):
        for ref, row0 in ((vst_ref, _ROW_VS), (vwt_ref, _ROW_VW)):
            ref[g, 0:HEAD_DIM, :] = vt[row0 + g * HEAD_DIM:row0 + (g + 1) * HEAD_DIM].astype(ref.dtype)
            ref[g, HEAD_DIM:2 * HEAD_DIM, :] = ones
        gate_ref[g] = jax.nn.sigmoid(vt[_ROW_GATE + g * _GATE_ROWS:_ROW_GATE + (g + 1) * _GATE_ROWS])


def _inproj(x, w_main, w_t, conv_w, cos_t, rot_t, hg_conv, gmean, tm):
    B, T, D = x.shape
    G = N_KV_GROUPS
    kv_shape = jax.ShapeDtypeStruct((B, G, T, HEAD_DIM), MXU_DTYPE)
    kv_spec = pl.BlockSpec((None, G, tm, HEAD_DIM), lambda b, i: (b, 0, i, 0))
    vt_shape = jax.ShapeDtypeStruct((B, G, 2 * HEAD_DIM, T), MXU_DTYPE)
    vt_spec = pl.BlockSpec((None, G, 2 * HEAD_DIM, tm), lambda b, i: (b, 0, 0, i))
    tab_spec = pl.BlockSpec((tm, LANES), lambda b, i: (i, 0))
    const2 = lambda b, i: (0, 0)
    return pl.pallas_call(
        _inproj_kernel,
        grid=(B, T // tm),
        in_specs=[
            pl.BlockSpec((None, tm, D), lambda b, i: (b, i, 0)),
            pl.BlockSpec((D, _MAIN_COLS), const2),
            pl.BlockSpec((_T_ROWS, D), const2),
            pl.BlockSpec((3, CONV_WIDTH), const2),
            tab_spec, tab_spec,
            pl.BlockSpec((1, CONV_WIDTH), const2),
            pl.BlockSpec((MXU_COLS, MXU_COLS), const2),
        ],
        out_specs=[
            pl.BlockSpec((None, tm, CONV_WIDTH), lambda b, i: (b, i, 0)),
            pl.BlockSpec((None, N_Q_HEADS, tm, LANES), lambda b, i: (b, 0, i, 0)),
            pl.BlockSpec((2, None, G, tm, HEAD_DIM), lambda b, i: (0, b, 0, i, 0)),
            pl.BlockSpec((None, G, tm, LANES), lambda b, i: (b, 0, i, 0)),
            kv_spec, vt_spec, vt_spec,
            pl.BlockSpec((None, G, _GATE_ROWS, tm), lambda b, i: (b, 0, 0, i)),
        ],
        out_shape=[
            jax.ShapeDtypeStruct((B, T, CONV_WIDTH), MXU_DTYPE),
            jax.ShapeDtypeStruct((B, N_Q_HEADS, T, LANES), MXU_DTYPE),
            jax.ShapeDtypeStruct((2, B, G, T, HEAD_DIM), MXU_DTYPE),
            jax.ShapeDtypeStruct((B, G, T, LANES), MXU_DTYPE),
            kv_shape, vt_shape, vt_shape,
            jax.ShapeDtypeStruct((B, G, _GATE_ROWS, T), F32),
        ],
        scratch_shapes=[pltpu.VMEM((tm + 2 * SUBLANES, CONV_WIDTH), F32)],
        compiler_params=pltpu.CompilerParams(
            dimension_semantics=("arbitrary", "arbitrary"), vmem_limit_bytes=VMEM_LIMIT),
    )(x, w_main, w_t, conv_w, cos_t, rot_t, hg_conv, gmean)


def _compress_kernel(sub_ref, pe_ref, w1_ref, b1_ref, w2_ref, b2_ref, w2t_ref, b2t_ref, out_ref, outt_ref, bbuf):
    nsub = sub_ref.shape[0]
    half = CMP_STRIDE * HEAD_DIM
    sub = sub_ref[...]
    top = _dot(sub, w1_ref[0:half, :])
    bbuf[0:nsub, :] = _dot(sub, w1_ref[half:2 * half, :])
    bbuf[nsub:nsub + SUBLANES, :] = jnp.zeros((SUBLANES, CMP_HIDDEN), F32)
    pe_rows = jnp.broadcast_to(pe_ref[...], (SUBLANES, 2 * half)).astype(MXU_DTYPE)
    const = _dot(pe_rows, w1_ref[...])[0:1, :] + b1_ref[...]
    h = top + bbuf[1:nsub + 1, :] + const
    act = jax.nn.gelu(h).astype(MXU_DTYPE)
    out_ref[...] = (_dot(act, w2_ref[...]) + b2_ref[...]).astype(out_ref.dtype)
    outt_ref[...] = (_dot_nt(w2t_ref[...], act) + b2t_ref[...]).astype(outt_ref.dtype)


def _compress(sub, pe, w1, b1, w2, b2):
    _, BG, nsub, width = sub.shape
    sel = lambda k, n: (k, 0, 0)
    per = lambda k, n: (k, n, 0, 0)
    return pl.pallas_call(
        _compress_kernel,
        grid=(2, BG),
        in_specs=[
            pl.BlockSpec((None, None, nsub, width), per),
            pl.BlockSpec((None, 1, 2 * width), sel),
            pl.BlockSpec((None, 2 * width, CMP_HIDDEN), sel),
            pl.BlockSpec((None, 1, CMP_HIDDEN), sel),
            pl.BlockSpec((None, CMP_HIDDEN, HEAD_DIM), sel),
            pl.BlockSpec((None, 1, HEAD_DIM), sel),
            pl.BlockSpec((None, HEAD_DIM, CMP_HIDDEN), sel),
            pl.BlockSpec((None, HEAD_DIM, 1), sel),
        ],
        out_specs=[pl.BlockSpec((None, None, nsub, HEAD_DIM), per),
                   pl.BlockSpec((None, None, HEAD_DIM, nsub), per)],
        out_shape=[jax.ShapeDtypeStruct((2, BG, nsub, HEAD_DIM), MXU_DTYPE),
                   jax.ShapeDtypeStruct((2, BG, HEAD_DIM, nsub), MXU_DTYPE)],
        scratch_shapes=[pltpu.VMEM((nsub + SUBLANES, CMP_HIDDEN), F32)],
        compiler_params=pltpu.CompilerParams(
            dimension_semantics=("arbitrary", "arbitrary"), vmem_limit_bytes=VMEM_LIMIT),
    )(sub, pe, w1, b1[:, None, :], w2, b2[:, None, :], jnp.swapaxes(w2, 1, 2), b2[:, :, None])


def _rank_accumulate(score, ranks, i0, i1):
    tq = score.shape[1]
    local = lax.broadcasted_iota(jnp.int32, (SUBLANES, tq), 0)
    ranks = list(ranks)
    for i in range(i0, i1):
        row = jnp.broadcast_to(score[i:i + 1, :], (SUBLANES, tq))
        for j in range(len(ranks)):
            slab = score[SUBLANES * j:SUBLANES * (j + 1), :]
            ge = jnp.where(row >= slab, 1.0, 0.0)
            gt = jnp.where(row > slab, 1.0, 0.0)
            if i < SUBLANES * j:
                ahead = ge
            elif i >= SUBLANES * (j + 1):
                ahead = gt
            else:
                ahead = jnp.where(local > i - SUBLANES * j, ge, gt)
            ranks[j] = ranks[j] + ahead
    return ranks


def _nsa_kernel(q_ref, kcmp_ref, vcmpt_ref, ks_ref, vst_ref, kw_ref, vwt_ref, gate_ref, aggt_ref, hg_ref,
                o_ref, sa_ref, sb_ref, acc_ref, *, tq, kc):
    R = Q_PER_KV
    rows = R * tq
    T = ks_ref.shape[0]
    ns = T // SLC_LEN
    ncp = kcmp_ref.shape[0]
    q0 = pl.program_id(2) * tq
    t_row = q0 + lax.broadcasted_iota(jnp.int32, (1, tq), 1)
    q_pad = q_ref[...]
    q64 = q_pad.reshape(rows, LANES)[:, :HEAD_DIM]

    def all_heads(a):
        return jnp.concatenate([a] * R, axis=1)

    def head(a, r, width):
        return a[:, r * width:(r + 1) * width]

    half = tq // 2
    wl = WINDOW + half

    def window_half(h):
        t_half = t_row[:, h * half:(h + 1) * half]
        ws = pl.multiple_of(jnp.maximum(q0 + h * half - WINDOW, 0), half)
        wpos = ws + lax.broadcasted_iota(jnp.int32, (wl, half), 0)
        wbias = jnp.where((wpos <= t_half) & (wpos > t_half - WINDOW), 0.0, NEG)
        qh = q64.reshape(R, tq, HEAD_DIM)[:, h * half:(h + 1) * half, :].reshape(R * half, HEAD_DIM)
        s = _dot_nt(kw_ref[pl.ds(ws, wl), :], qh) + jnp.concatenate([wbias] * R, axis=1)
        p = jnp.exp2(s - jnp.max(s, axis=0, keepdims=True)).astype(MXU_DTYPE)
        ow = _dot(vwt_ref[:, pl.ds(ws, wl)], p)
        return ow[:HEAD_DIM] / ow[HEAD_DIM:]

    cmp_end = lax.broadcasted_iota(jnp.int32, (ncp, tq), 0) * CMP_STRIDE + (CMP_LEN - 1)
    cbias = jnp.where(cmp_end <= t_row, 0.0, NEG)
    s = _dot_nt(kcmp_ref[...], q64) + all_heads(cbias)
    p = jnp.exp2(s - jnp.max(s, axis=0, keepdims=True))
    l = jnp.sum(p, axis=0, keepdims=True)
    p = p * (all_heads(jnp.where(t_row >= CMP_LEN - 1, 1.0, 0.0)) / l)
    o_cmp = _dot(vcmpt_ref[...], p.astype(MXU_DTYPE))

    psum = head(p, 0, tq) + head(p, 1, tq) + head(p, 2, tq) + head(p, 3, tq)
    p_hi = psum.astype(MXU_DTYPE)
    p_lo = (psum - p_hi.astype(F32)).astype(MXU_DTYPE)
    imp = _dot(aggt_ref[...], p_hi) + _dot(aggt_ref[...], p_lo)
    blk = lax.broadcasted_iota(jnp.int32, (ns, tq), 0)
    tb = (q0 + lax.broadcasted_iota(jnp.int32, (ns, tq), 1)) // SLC_LEN
    forced = (blk == 0) | (blk == tb) | (blk == tb - 1)
    score = jnp.where(forced, BIG, imp)
    score = jnp.where(blk <= tb, score, -BIG)
    ranks = [jnp.zeros((SUBLANES, tq), F32) for _ in range(ns // SUBLANES)]
    ranks = _rank_accumulate(score, ranks, 0, ns // 2)
    o_win_halves = [window_half(0)]
    ranks = _rank_accumulate(score, ranks, ns // 2, ns)
    unselected = jnp.where(jnp.concatenate(ranks, axis=0) < min(SLC_TOPK, ns), 0.0, 1.0)
    tag_rows = [jnp.zeros((HEAD_DIM, tq), F32), unselected]
    if ns < LANES - HEAD_DIM:
        tag_rows.append(jnp.zeros((LANES - HEAD_DIM - ns, tq), F32))
    tag = jnp.concatenate(tag_rows, axis=0).T.astype(MXU_DTYPE)
    lane = lax.broadcasted_iota(jnp.int32, (R, tq, LANES), 2)
    q_aug = jnp.where(lane < HEAD_DIM, q_pad, tag[None]).reshape(rows, LANES)

    def scores(c):
        return _dot_nt(ks_ref[pl.ds(pl.multiple_of(c * kc, kc), kc), :], q_aug)

    def causal_bias(c):
        kpos = c * kc + lax.broadcasted_iota(jnp.int32, (kc, tq), 0)
        return all_heads(jnp.where(kpos <= t_row, 0.0, NEG))

    def consume(s, c, m):
        m_new = jnp.maximum(m, jnp.max(s, axis=0, keepdims=True))
        p = jnp.exp2(s - m_new).astype(MXU_DTYPE)
        pv = _dot(vst_ref[:, pl.ds(pl.multiple_of(c * kc, kc), kc)], p)
        acc_ref[...] = jnp.exp2(m - m_new) * acc_ref[...] + pv
        return m_new

    n_pairs = q0 // (2 * kc)
    acc_ref[...] = jnp.zeros((2 * HEAD_DIM, rows), F32)
    sa_ref[...] = scores(0)
    o_win_halves.append(window_half(1))

    def pair_step(j, m):
        sb_ref[...] = scores(2 * j + 1)
        m = consume(sa_ref[...], 2 * j, m)
        sa_ref[...] = scores(2 * j + 2)
        return consume(sb_ref[...], 2 * j + 1, m)

    m = lax.fori_loop(0, n_pairs, pair_step, jnp.full((1, rows), NEG, F32))
    c_last = 2 * n_pairs
    m = consume(sa_ref[...] + causal_bias(c_last), c_last, m)

    @pl.when(q0 + tq > (c_last + 1) * kc)
    def _():
        consume(scores(c_last + 1) + causal_bias(c_last + 1), c_last + 1, m)

    acc = acc_ref[...]
    o_slc = acc[:HEAD_DIM] / acc[HEAD_DIM:]

    g = gate_ref[...]
    outs = []
    for r in range(R):
        c = N_GATES * r
        o_win = jnp.concatenate([head(o_win_halves[0], r, half), head(o_win_halves[1], r, half)], axis=1)
        o = g[c:c + 1] * head(o_cmp, r, tq) + g[c + 1:c + 2] * head(o_slc, r, tq) + g[c + 2:c + 3] * o_win
        ms = jnp.mean(o * o, axis=0, keepdims=True)
        outs.append(o * lax.rsqrt(ms + 1e-6) * hg_ref[r * HEAD_DIM:(r + 1) * HEAD_DIM, :])
    o_ref[...] = jnp.concatenate(outs, axis=0).T.astype(o_ref.dtype)


def _nsa(q, cmp_kv, cmp_kv_t, ks, vst, kw, vwt, gates, agg_t, hg_attn, tq, kc):
    B, _, T, _ = q.shape
    G, R = N_KV_GROUPS, Q_PER_KV
    ncp = cmp_kv.shape[2]
    ns = T // SLC_LEN
    assert T % (2 * kc) == 0 and kc % tq == 0 and WINDOW % (tq // 2) == 0 and T >= WINDOW + tq
    assert ns <= LANES - HEAD_DIM, "block tags must fit in the upper lanes of a key row"
    whole = lambda b, g, i: (b, g, 0, 0)
    hg_lanes = jnp.broadcast_to(hg_attn.reshape(G, R * HEAD_DIM, 1), (G, R * HEAD_DIM, tq))
    return pl.pallas_call(
        functools.partial(_nsa_kernel, tq=tq, kc=kc),
        grid=(B, G, T // tq),
        in_specs=[
            pl.BlockSpec((None, R, tq, LANES), lambda b, g, i: (b, g, i, 0)),
            pl.BlockSpec((None, None, ncp, HEAD_DIM), lambda b, g, i: (0, b * G + g, 0, 0)),
            pl.BlockSpec((None, None, HEAD_DIM, ncp), lambda b, g, i: (1, b * G + g, 0, 0)),
            pl.BlockSpec((None, None, T, LANES), whole),
            pl.BlockSpec((None, None, 2 * HEAD_DIM, T), whole),
            pl.BlockSpec((None, None, T, HEAD_DIM), whole),
            pl.BlockSpec((None, None, 2 * HEAD_DIM, T), whole),
            pl.BlockSpec((None, None, _GATE_ROWS, tq), lambda b, g, i: (b, g, 0, i)),
            pl.BlockSpec((ns, ncp), lambda b, g, i: (0, 0)),
            pl.BlockSpec((None, R * HEAD_DIM, tq), lambda b, g, i: (g, 0, 0)),
        ],
        out_specs=pl.BlockSpec((None, tq, R * HEAD_DIM), lambda b, g, i: (b, i, g)),
        out_shape=jax.ShapeDtypeStruct((B, T, ATTN_WIDTH), MXU_DTYPE),
        scratch_shapes=[pltpu.VMEM((kc, R * tq), F32), pltpu.VMEM((kc, R * tq), F32),
                        pltpu.VMEM((2 * HEAD_DIM, R * tq), F32)],
        compiler_params=pltpu.CompilerParams(
            dimension_semantics=("arbitrary", "arbitrary", "arbitrary"), vmem_limit_bytes=VMEM_LIMIT),
    )(q, cmp_kv, cmp_kv_t, ks, vst, kw, vwt, gates, agg_t, hg_lanes)


def _outproj_kernel(conv_ref, attn_ref, x_ref, w_ref, g_ref, b_ref, o_ref):
    mix = _dot(conv_ref[...], w_ref[0:CONV_WIDTH, :]) + _dot(attn_ref[...], w_ref[CONV_WIDTH:, :])
    o_ref[...] = _layer_norm(ALPHA * x_ref[...] + mix, g_ref[...], b_ref[...])


def _outproj(conv_n, attn_n, x, w_out, g, b, tm):
    N, D = x.shape
    row = lambda i: (i, 0)
    const = lambda i: (0, 0)
    return pl.pallas_call(
        _outproj_kernel,
        grid=(N // tm,),
        in_specs=[
            pl.BlockSpec((tm, CONV_WIDTH), row), pl.BlockSpec((tm, ATTN_WIDTH), row),
            pl.BlockSpec((tm, D), row), pl.BlockSpec((D, D), const),
            pl.BlockSpec((1, D), const), pl.BlockSpec((1, D), const),
        ],
        out_specs=pl.BlockSpec((tm, D), row),
        out_shape=jax.ShapeDtypeStruct((N, D), F32),
        compiler_params=pltpu.CompilerParams(
            dimension_semantics=("arbitrary",), vmem_limit_bytes=VMEM_LIMIT),
    )(conv_n, attn_n, x, w_out, g, b)


def _ffn_kernel(x_ref, halo_ref, wup_ref, cw_ref, cb_ref, wdn_ref, g_ref, b_ref, o_ref, ubuf, act_ref,
                *, tiles_per_seq, fc):
    tm = x_ref.shape[0]
    x = x_ref[...]
    first = (pl.program_id(0) % tiles_per_seq) == 0
    halo = jnp.where(first, 0.0, halo_ref[...])
    xb = jnp.concatenate([halo, x], axis=0).astype(MXU_DTYPE)

    def up_conv(col0, buf):
        buf[...] = _dot(xb, wup_ref[:, col0:col0 + fc])
        return (cw_ref[2:3, col0:col0 + fc] * buf[SUBLANES:SUBLANES + tm, :]
                + cw_ref[1:2, col0:col0 + fc] * buf[SUBLANES - 1:SUBLANES - 1 + tm, :]
                + cw_ref[0:1, col0:col0 + fc] * buf[SUBLANES - 2:SUBLANES - 2 + tm, :]
                + cb_ref[:, col0:col0 + fc])

    for c in range(D_FF // fc):
        val = up_conv(c * fc, ubuf.at[c % 2, 0])
        gate = up_conv(D_FF + c * fc, ubuf.at[c % 2, 1])
        act_ref[:, c * fc:(c + 1) * fc] = (gate * jax.nn.sigmoid(gate) * val).astype(act_ref.dtype)
    o_ref[...] = _layer_norm(ALPHA * x + _dot(act_ref[...], wdn_ref[...]), g_ref[...], b_ref[...])


def _ffn(x, w_up, conv_w, conv_b, w_down, g, b, tm, tiles_per_seq, fc):
    N, D = x.shape
    row = lambda i: (i, 0)
    const = lambda i: (0, 0)
    halo_blocks = tm // SUBLANES
    return pl.pallas_call(
        functools.partial(_ffn_kernel, tiles_per_seq=tiles_per_seq, fc=fc),
        grid=(N // tm,),
        in_specs=[
            pl.BlockSpec((tm, D), row),
            pl.BlockSpec((SUBLANES, D), lambda i: (jnp.maximum(i * halo_blocks - 1, 0), 0)),
            pl.BlockSpec((D, 2 * D_FF), const, pipeline_mode=pl.Buffered(1)),
            pl.BlockSpec((3, 2 * D_FF), const),
            pl.BlockSpec((1, 2 * D_FF), const),
            pl.BlockSpec((D_FF, D), const, pipeline_mode=pl.Buffered(1)),
            pl.BlockSpec((1, D), const), pl.BlockSpec((1, D), const),
        ],
        out_specs=pl.BlockSpec((tm, D), row),
        out_shape=jax.ShapeDtypeStruct((N, D), F32),
        scratch_shapes=[pltpu.VMEM((2, 2, tm + SUBLANES, fc), F32), pltpu.VMEM((tm, D_FF), MXU_DTYPE)],
        compiler_params=pltpu.CompilerParams(
            dimension_semantics=("arbitrary",), vmem_limit_bytes=VMEM_LIMIT),
    )(x, x, w_up, conv_w, conv_b, w_down, g, b)


def _rope_lane_tables(T):
    half = ROPE_DIM // 2
    inv_freq = ROPE_THETA ** (-(np.arange(half, dtype=np.float32) * 2.0 / ROPE_DIM))
    ang = jnp.arange(T, dtype=F32)[:, None] * jnp.asarray(inv_freq, F32)[None]
    cos, sin = jnp.cos(ang), jnp.sin(ang)
    rest = HEAD_DIM - ROPE_DIM
    one, zero = jnp.ones((T, rest), F32), jnp.zeros((T, rest), F32)
    zh = jnp.zeros((T, half), F32)
    cos_t = jnp.concatenate([cos, cos, one], -1)
    rot_t = jnp.concatenate([-sin, sin, zero], -1)
    rep = LANES // HEAD_DIM
    return tuple(jnp.tile(t, (1, rep)) for t in (cos_t, rot_t))


def _group_mean_matrix():
    grp = np.arange(MXU_COLS) // HEAD_DIM
    return jnp.asarray((grp[:, None] == grp[None, :]) / HEAD_DIM, dtype=MXU_DTYPE)


def _slc_aggregation_t(T):
    nc = T // CMP_STRIDE - CMP_LEN // CMP_STRIDE + 1
    ns = T // SLC_LEN
    sc = np.arange(nc)[None, :] * CMP_STRIDE
    ss = np.arange(ns)[:, None] * SLC_LEN
    ov = np.clip(np.minimum(sc + CMP_LEN, ss + SLC_LEN) - np.maximum(sc, ss), 0, None) / CMP_LEN
    ov = np.pad(ov, ((0, 0), (0, T // CMP_STRIDE - nc)))
    return jnp.asarray(ov, dtype=MXU_DTYPE)


def _split_in_weights(w):
    w_main = jnp.concatenate([w[:, :_W_VS], w[:, _W_KW:_W_VW]], axis=1)
    lanes = np.arange(LANES)
    rotated = (lanes % HEAD_DIM) < ROPE_DIM
    partners = []
    for c0, width in ((_OFF_Q, ATTN_WIDTH), (_OFF_KSW, 2 * KV_WIDTH), (_OFF_KC, KV_WIDTH)):
        for b0 in range(c0, c0 + width, LANES):
            partners.append(jnp.where(rotated[None, :], w_main[:, b0 + (lanes ^ (ROPE_DIM // 2))], 0.0))
    w_main = jnp.concatenate([w_main] + partners, axis=1)
    per_group = Q_PER_KV * N_GATES
    pad = jnp.zeros((w.shape[0], _GATE_ROWS - per_group), w.dtype)
    cols = [w[:, _W_VS:_W_KW], w[:, _W_VW:_W_GATE]]
    for g in range(N_KV_GROUPS):
        cols += [w[:, _W_GATE + g * per_group:_W_GATE + (g + 1) * per_group], pad]
    return w_main.astype(MXU_DTYPE), jnp.concatenate(cols, axis=1).T.astype(MXU_DTYPE)


def kernel(x, w_in, short_conv_w, cmp_pe, cmp_w1, cmp_b1, cmp_w2, cmp_b2, head_norm_g, w_out, ln1_g, ln1_b,
           w_up, ffn_conv_w, ffn_conv_b, w_down, ln2_g, ln2_b):
    B, T, D = x.shape
    tm = min(512, T)
    cos_t, rot_t = _rope_lane_tables(T)
    gmean = _group_mean_matrix()
    agg_t = _slc_aggregation_t(T)
    nsub = T // CMP_STRIDE
    for i in range(DEPTH):
        w_main, w_t = _split_in_weights(w_in[i])
        hg = head_norm_g[i]
        conv_n, q, cv, ks, kw, vst, vwt, gates = _inproj(
            x, w_main, w_t, short_conv_w[i], cos_t, rot_t, hg[None, :CONV_WIDTH], gmean, tm)
        sub = cv.reshape(2, B * N_KV_GROUPS, nsub, CMP_STRIDE * HEAD_DIM)
        cmp_kv, cmp_kv_t = _compress(
            sub, cmp_pe[i].reshape(2, 1, CMP_LEN * HEAD_DIM), cmp_w1[i].astype(MXU_DTYPE),
            cmp_b1[i], cmp_w2[i].astype(MXU_DTYPE), cmp_b2[i])
        attn_n = _nsa(q, cmp_kv, cmp_kv_t, ks, vst, kw, vwt, gates, agg_t, hg[CONV_WIDTH:], tq=256, kc=512)
        x2 = _outproj(conv_n.reshape(B * T, CONV_WIDTH), attn_n.reshape(B * T, ATTN_WIDTH),
                      x.reshape(B * T, D), w_out[i].astype(MXU_DTYPE), ln1_g[i][None], ln1_b[i][None], tm)
        x2 = _ffn(x2, w_up[i].astype(MXU_DTYPE), ffn_conv_w[i], ffn_conv_b[i][None],
                  w_down[i].astype(MXU_DTYPE), ln2_g[i][None], ln2_b[i][None], tm, T // tm, 256)
        x = x2.reshape(B, T, D)
    return x
```

```python
import functools

import numpy as np
import jax
import jax.numpy as jnp
from jax import lax
from jax.experimental import pallas as pl
from jax.experimental.pallas import tpu as pltpu

D_MODEL = 1024
DEPTH = 2
HEAD_DIM = 64
CONV_WIDTH = 512
ATTN_WIDTH = 512
N_Q_HEADS = 8
N_KV_GROUPS = 2
Q_PER_KV = 4
KV_WIDTH = N_KV_GROUPS * HEAD_DIM
N_GATES = 3
CMP_LEN = 32
CMP_STRIDE = 16
CMP_HIDDEN = 256
SLC_LEN = 64
SLC_TOPK = 16
WINDOW = 512
ROPE_THETA = 500000.0
ROPE_DIM = 16
D_FF = 2816
ALPHA = (2.0 * DEPTH) ** 0.25
NEG = -1e30
BIG = 1e9

F32 = jnp.float32
MXU_DTYPE = jnp.bfloat16
LANES = 128
SUBLANES = 8
MXU_COLS = 256
VMEM_LIMIT = 56 * 1024 * 1024

_W_VS, _W_KW, _W_VW, _W_GATE, _W_END = 2432, 2560, 2688, 2816, 2840
_OFF_B, _OFF_C, _OFF_H, _OFF_Q, _OFF_KC, _OFF_KSW = 0, 512, 1024, 1536, 2048, 2304
_OFF_PQ, _OFF_PKSW, _OFF_PKC = 2560, 3072, 3328
_MAIN_COLS = 3456
LOG2E = 1.4426950408889634
_GATE_ROWS = 16
_ROW_VS, _ROW_VW, _ROW_GATE = 0, KV_WIDTH, 2 * KV_WIDTH
_T_ROWS = 2 * KV_WIDTH + N_KV_GROUPS * _GATE_ROWS

_NT = (((1,), (1,)), ((), ()))


def _dot(a, b):
    return jnp.dot(a, b, preferred_element_type=F32)


def _dot_nt(a, b):
    return lax.dot_general(a, b, _NT, preferred_element_type=F32)


def _layer_norm(y, g, b):
    mu = jnp.mean(y, axis=-1, keepdims=True)
    d = y - mu
    var = jnp.mean(d * d, axis=-1, keepdims=True)
    return d * lax.rsqrt(var + 1e-5) * g + b


def _inproj_kernel(x_ref, w_ref, wt_ref, cw_ref, cos_ref, rot_ref, hg_ref, gmean_ref,
                   conv_ref, q_ref, cv_ref, ks_ref, kw_ref, vst_ref, vwt_ref, gate_ref, zbuf):
    tm = x_ref.shape[0]
    xb = x_ref[...].astype(MXU_DTYPE)

    def mm(c0, width):
        return _dot(xb, w_ref[:, c0:c0 + width])

    @pl.when(pl.program_id(1) == 0)
    def _():
        zbuf[0:SUBLANES, :] = jnp.zeros((SUBLANES, CONV_WIDTH), F32)

    def conv_block(c0):
        cs = slice(c0, c0 + MXU_COLS)
        z = mm(_OFF_C + c0, MXU_COLS) * mm(_OFF_H + c0, MXU_COLS)
        zbuf[SUBLANES:SUBLANES + tm, cs] = z
        conv = (cw_ref[2:3, cs] * z + cw_ref[1:2, cs] * zbuf[SUBLANES - 1:SUBLANES - 1 + tm, cs]
                + cw_ref[0:1, cs] * zbuf[SUBLANES - 2:SUBLANES - 2 + tm, cs])
        zbuf[0:SUBLANES, cs] = zbuf[tm:tm + SUBLANES, cs]
        y = mm(_OFF_B + c0, MXU_COLS) * conv
        y2 = y * y
        y2_hi = y2.astype(MXU_DTYPE)
        y2_lo = (y2 - y2_hi.astype(F32)).astype(MXU_DTYPE)
        ms = _dot(y2_hi, gmean_ref[...]) + _dot(y2_lo, gmean_ref[...])
        conv_ref[:, cs] = (y * lax.rsqrt(ms + 1e-6) * hg_ref[:, cs]).astype(conv_ref.dtype)

    def rope(v, partner):
        return v * cos_ref[...] + partner * rot_ref[...]

    def split_heads(v, ref, lead):
        for g in range(N_KV_GROUPS):
            ref[lead + (g,)] = v[:, g * HEAD_DIM:(g + 1) * HEAD_DIM].astype(ref.dtype)

    lane = lax.broadcasted_iota(jnp.int32, (tm, LANES), 1)
    low = lane < HEAD_DIM

    def head_pair(v):
        return v, pltpu.roll(v, HEAD_DIM, 1)

    def q_block(c):
        r4 = mm(_OFF_Q + c * MXU_COLS, MXU_COLS)
        p4 = mm(_OFF_PQ + c * MXU_COLS, MXU_COLS)
        for i in range(MXU_COLS // LANES):
            sl = slice(i * LANES, (i + 1) * LANES)
            r = rope(r4[:, sl], p4[:, sl]) * (HEAD_DIM ** -0.5 * LOG2E)
            for j, h in enumerate(head_pair(r)):
                q_ref[4 * c + 2 * i + j] = jnp.where(low, h, 0.0).astype(q_ref.dtype)

    conv_block(0)
    q_block(0)
    conv_block(MXU_COLS)
    q_block(1)
    kvc = mm(_OFF_KC, 2 * KV_WIDTH)
    split_heads(rope(kvc[:, :KV_WIDTH], mm(_OFF_PKC, KV_WIDTH)), cv_ref, (0,))
    split_heads(kvc[:, KV_WIDTH:], cv_ref, (1,))
    ksw = mm(_OFF_KSW, 2 * KV_WIDTH)
    pksw = mm(_OFF_PKSW, 2 * KV_WIDTH)
    pos = pl.program_id(1) * tm + lax.broadcasted_iota(jnp.int32, (tm, LANES), 0)
    block_tag = jnp.where(pos // SLC_LEN == lane - HEAD_DIM, NEG, 0.0)
    for g, h in enumerate(head_pair(rope(ksw[:, :KV_WIDTH], pksw[:, :KV_WIDTH]))):
        ks_ref[g] = jnp.where(low, h, block_tag).astype(ks_ref.dtype)
    split_heads(rope(ksw[:, KV_WIDTH:], pksw[:, KV_WIDTH:]), kw_ref, ())

    vt = _dot_nt(wt_ref[...], xb)
    ones = jnp.ones((HEAD_DIM, tm), vst_ref.dtype)
    for g in range(N_KV_GROUPS):
        for ref, row0 in ((vst_ref, _ROW_VS), (vwt_ref, _ROW_VW)):
            ref[g, 0:HEAD_DIM, :] = vt[row0 + g * HEAD_DIM:row0 + (g + 1) * HEAD_DIM].astype(ref.dtype)
            ref[g, HEAD_DIM:2 * HEAD_DIM, :] = ones
        gate_ref[g] = jax.nn.sigmoid(vt[_ROW_GATE + g * _GATE_ROWS:_ROW_GATE + (g + 1) * _GATE_ROWS])


def _inproj(x, w_main, w_t, conv_w, cos_t, rot_t, hg_conv, gmean, tm):
    B, T, D = x.shape
    G = N_KV_GROUPS
    kv_shape = jax.ShapeDtypeStruct((B, G, T, HEAD_DIM), MXU_DTYPE)
    kv_spec = pl.BlockSpec((None, G, tm, HEAD_DIM), lambda b, i: (b, 0, i, 0))
    vt_shape = jax.ShapeDtypeStruct((B, G, 2 * HEAD_DIM, T), MXU_DTYPE)
    vt_spec = pl.BlockSpec((None, G, 2 * HEAD_DIM, tm), lambda b, i: (b, 0, 0, i))
    tab_spec = pl.BlockSpec((tm, LANES), lambda b, i: (i, 0))
    const2 = lambda b, i: (0, 0)
    return pl.pallas_call(
        _inproj_kernel,
        grid=(B, T // tm),
        in_specs=[
            pl.BlockSpec((None, tm, D), lambda b, i: (b, i, 0)),
            pl.BlockSpec((D, _MAIN_COLS), const2),
            pl.BlockSpec((_T_ROWS, D), const2),
            pl.BlockSpec((3, CONV_WIDTH), const2),
            tab_spec, tab_spec,
            pl.BlockSpec((1, CONV_WIDTH), const2),
            pl.BlockSpec((MXU_COLS, MXU_COLS), const2),
        ],
        out_specs=[
            pl.BlockSpec((None, tm, CONV_WIDTH), lambda b, i: (b, i, 0)),
            pl.BlockSpec((None, N_Q_HEADS, tm, LANES), lambda b, i: (b, 0, i, 0)),
            pl.BlockSpec((2, None, G, tm, HEAD_DIM), lambda b, i: (0, b, 0, i, 0)),
            pl.BlockSpec((None, G, tm, LANES), lambda b, i: (b, 0, i, 0)),
            kv_spec, vt_spec, vt_spec,
            pl.BlockSpec((None, G, _GATE_ROWS, tm), lambda b, i: (b, 0, 0, i)),
        ],
        out_shape=[
            jax.ShapeDtypeStruct((B, T, CONV_WIDTH), MXU_DTYPE),
            jax.ShapeDtypeStruct((B, N_Q_HEADS, T, LANES), MXU_DTYPE),
            jax.ShapeDtypeStruct((2, B, G, T, HEAD_DIM), F32),
            jax.ShapeDtypeStruct((B, G, T, LANES), MXU_DTYPE),
            kv_shape, vt_shape, vt_shape,
            jax.ShapeDtypeStruct((B, G, _GATE_ROWS, T), F32),
        ],
        scratch_shapes=[pltpu.VMEM((tm + 2 * SUBLANES, CONV_WIDTH), F32)],
        compiler_params=pltpu.CompilerParams(
            dimension_semantics=("arbitrary", "arbitrary"), vmem_limit_bytes=VMEM_LIMIT),
    )(x, w_main, w_t, conv_w, cos_t, rot_t, hg_conv, gmean)


def _compress_kernel(a_ref, pe_ref, w1_ref, b1_ref, w2_ref, b2_ref, w2t_ref, b2t_ref, out_ref, outt_ref,
                     sub_ref, bbuf):
    nsub = sub_ref.shape[0]
    half = CMP_STRIDE * HEAD_DIM
    for j in range(CMP_STRIDE):
        sub_ref[:, j * HEAD_DIM:(j + 1) * HEAD_DIM] = (
            a_ref[pl.ds(j, nsub, stride=CMP_STRIDE), :].astype(sub_ref.dtype))
    sub = sub_ref[...]
    top = _dot(sub, w1_ref[0:half, :])
    bbuf[0:nsub, :] = _dot(sub, w1_ref[half:2 * half, :])
    bbuf[nsub:nsub + SUBLANES, :] = jnp.zeros((SUBLANES, CMP_HIDDEN), F32)
    pe_rows = jnp.broadcast_to(pe_ref[...], (SUBLANES, 2 * half)).astype(MXU_DTYPE)
    const = _dot(pe_rows, w1_ref[...])[0:1, :] + b1_ref[...]
    h = top + bbuf[1:nsub + 1, :] + const
    act = jax.nn.gelu(h).astype(MXU_DTYPE)
    out_ref[...] = (_dot(act, w2_ref[...]) + b2_ref[...]).astype(out_ref.dtype)
    outt_ref[...] = (_dot_nt(w2t_ref[...], act) + b2t_ref[...]).astype(outt_ref.dtype)


def _compress(a, pe, w1, b1, w2, b2):
    _, BG, T, _ = a.shape
    nsub = T // CMP_STRIDE
    width = CMP_STRIDE * HEAD_DIM
    sel = lambda k, n: (k, 0, 0)
    per = lambda k, n: (k, n, 0, 0)
    return pl.pallas_call(
        _compress_kernel,
        grid=(2, BG),
        in_specs=[
            pl.BlockSpec((None, None, T, HEAD_DIM), per),
            pl.BlockSpec((None, 1, 2 * width), sel),
            pl.BlockSpec((None, 2 * width, CMP_HIDDEN), sel),
            pl.BlockSpec((None, 1, CMP_HIDDEN), sel),
            pl.BlockSpec((None, CMP_HIDDEN, HEAD_DIM), sel),
            pl.BlockSpec((None, 1, HEAD_DIM), sel),
            pl.BlockSpec((None, HEAD_DIM, CMP_HIDDEN), sel),
            pl.BlockSpec((None, HEAD_DIM, 1), sel),
        ],
        out_specs=[pl.BlockSpec((None, None, nsub, HEAD_DIM), per),
                   pl.BlockSpec((None, None, HEAD_DIM, nsub), per)],
        out_shape=[jax.ShapeDtypeStruct((2, BG, nsub, HEAD_DIM), MXU_DTYPE),
                   jax.ShapeDtypeStruct((2, BG, HEAD_DIM, nsub), MXU_DTYPE)],
        scratch_shapes=[pltpu.VMEM((nsub, width), MXU_DTYPE), pltpu.VMEM((nsub + SUBLANES, CMP_HIDDEN), F32)],
        compiler_params=pltpu.CompilerParams(
            dimension_semantics=("arbitrary", "arbitrary"), vmem_limit_bytes=VMEM_LIMIT),
    )(a, pe, w1, b1[:, None, :], w2, b2[:, None, :], jnp.swapaxes(w2, 1, 2), b2[:, :, None])


def _unselected_blocks(score, n_live, topk):
    ns, tq = score.shape
    local = lax.broadcasted_iota(jnp.int32, (SUBLANES, tq), 0)
    ranks = [jnp.zeros((SUBLANES, tq), F32) for _ in range(n_live // SUBLANES)]
    for i in range(n_live):
        row = jnp.broadcast_to(score[i:i + 1, :], (SUBLANES, tq))
        for j in range(len(ranks)):
            slab = score[SUBLANES * j:SUBLANES * (j + 1), :]
            ge = jnp.where(row >= slab, 1.0, 0.0)
            gt = jnp.where(row > slab, 1.0, 0.0)
            if i < SUBLANES * j:
                ahead = ge
            elif i >= SUBLANES * (j + 1):
                ahead = gt
            else:
                ahead = jnp.where(local > i - SUBLANES * j, ge, gt)
            ranks[j] = ranks[j] + ahead
    flags = [jnp.where(jnp.concatenate(ranks, axis=0) < topk, 0.0, 1.0)]
    if n_live < ns:
        flags.append(jnp.ones((ns - n_live, tq), F32))
    return jnp.concatenate(flags, axis=0)


def _nsa_kernel(q_ref, kcmp_ref, vcmpt_ref, ks_ref, vst_ref, kw_ref, vwt_ref, gate_ref, aggt_ref, hg_ref,
                o_ref, sa_ref, sb_ref, acc_ref, *, tq, kc):
    R = Q_PER_KV
    rows = R * tq
    T = ks_ref.shape[0]
    ns = T // SLC_LEN
    ncp = kcmp_ref.shape[0]
    q0 = pl.program_id(2) * tq
    t_row = q0 + lax.broadcasted_iota(jnp.int32, (1, tq), 1)
    q_pad = q_ref[...]
    q64 = q_pad.reshape(rows, LANES)[:, :HEAD_DIM]

    def all_heads(a):
        return jnp.concatenate([a] * R, axis=1)

    def head(a, r, width):
        return a[:, r * width:(r + 1) * width]

    half = tq // 2
    wl = WINDOW + half

    def window_half(h):
        t_half = t_row[:, h * half:(h + 1) * half]
        ws = pl.multiple_of(jnp.maximum(q0 + h * half - WINDOW, 0), half)
        wpos = ws + lax.broadcasted_iota(jnp.int32, (wl, half), 0)
        wbias = jnp.where((wpos <= t_half) & (wpos > t_half - WINDOW), 0.0, NEG)
        qh = q64.reshape(R, tq, HEAD_DIM)[:, h * half:(h + 1) * half, :].reshape(R * half, HEAD_DIM)
        s = _dot_nt(kw_ref[pl.ds(ws, wl), :], qh) + jnp.concatenate([wbias] * R, axis=1)
        p = jnp.exp2(s - jnp.max(s, axis=0, keepdims=True)).astype(MXU_DTYPE)
        ow = _dot(vwt_ref[:, pl.ds(ws, wl)], p)
        return ow[:HEAD_DIM] / ow[HEAD_DIM:]

    cmp_end = lax.broadcasted_iota(jnp.int32, (ncp, tq), 0) * CMP_STRIDE + (CMP_LEN - 1)
    cbias = jnp.where(cmp_end <= t_row, 0.0, NEG)
    s = _dot_nt(kcmp_ref[...], q64) + all_heads(cbias)
    p = jnp.exp2(s - jnp.max(s, axis=0, keepdims=True))
    l = jnp.sum(p, axis=0, keepdims=True)
    p = p * (all_heads(jnp.where(t_row >= CMP_LEN - 1, 1.0, 0.0)) / l)
    o_cmp = _dot(vcmpt_ref[...], p.astype(MXU_DTYPE))

    psum = head(p, 0, tq) + head(p, 1, tq) + head(p, 2, tq) + head(p, 3, tq)
    p_hi = psum.astype(MXU_DTYPE)
    p_lo = (psum - p_hi.astype(F32)).astype(MXU_DTYPE)
    imp = _dot(aggt_ref[...], p_hi) + _dot(aggt_ref[...], p_lo)
    blk = lax.broadcasted_iota(jnp.int32, (ns, tq), 0)
    tb = (q0 + lax.broadcasted_iota(jnp.int32, (ns, tq), 1)) // SLC_LEN
    forced = (blk == 0) | (blk == tb) | (blk == tb - 1)
    score = jnp.where(forced, BIG, imp)
    score = jnp.where(blk <= tb, score, -BIG)
    live_steps = [n for n in (ns // 4, ns // 2, 3 * ns // 4) if n % SUBLANES == 0 and n > 0] + [ns]
    live_needed = (q0 + tq - 1) // SLC_LEN + 1
    case = sum((live_needed > n).astype(jnp.int32) for n in live_steps[:-1])
    unselected = lax.switch(
        case, [functools.partial(_unselected_blocks, n_live=n, topk=min(SLC_TOPK, ns)) for n in live_steps],
        score)
    o_win_halves = [window_half(0)]
    tag_rows = [jnp.zeros((HEAD_DIM, tq), F32), unselected]
    if ns < LANES - HEAD_DIM:
        tag_rows.append(jnp.zeros((LANES - HEAD_DIM - ns, tq), F32))
    tag = jnp.concatenate(tag_rows, axis=0).T.astype(MXU_DTYPE)
    lane = lax.broadcasted_iota(jnp.int32, (R, tq, LANES), 2)
    q_aug = jnp.where(lane < HEAD_DIM, q_pad, tag[None]).reshape(rows, LANES)

    def scores(c):
        return _dot_nt(ks_ref[pl.ds(pl.multiple_of(c * kc, kc), kc), :], q_aug)

    def causal_bias(c):
        kpos = c * kc + lax.broadcasted_iota(jnp.int32, (kc, tq), 0)
        return all_heads(jnp.where(kpos <= t_row, 0.0, NEG))

    def consume(s, c, m):
        m_new = jnp.maximum(m, jnp.max(s, axis=0, keepdims=True))
        p = jnp.exp2(s - m_new).astype(MXU_DTYPE)
        pv = _dot(vst_ref[:, pl.ds(pl.multiple_of(c * kc, kc), kc)], p)
        acc_ref[...] = jnp.exp2(m - m_new) * acc_ref[...] + pv
        return m_new

    n_pairs = q0 // (2 * kc)
    acc_ref[...] = jnp.zeros((2 * HEAD_DIM, rows), F32)
    sa_ref[...] = scores(0)
    o_win_halves.append(window_half(1))

    def pair_step(j, m):
        sb_ref[...] = scores(2 * j + 1)
        m = consume(sa_ref[...], 2 * j, m)
        sa_ref[...] = scores(2 * j + 2)
        return consume(sb_ref[...], 2 * j + 1, m)

    m = lax.fori_loop(0, n_pairs, pair_step, jnp.full((1, rows), NEG, F32))
    c_last = 2 * n_pairs
    m = consume(sa_ref[...] + causal_bias(c_last), c_last, m)

    @pl.when(q0 + tq > (c_last + 1) * kc)
    def _():
        consume(scores(c_last + 1) + causal_bias(c_last + 1), c_last + 1, m)

    acc = acc_ref[...]
    o_slc = acc[:HEAD_DIM] / acc[HEAD_DIM:]

    g = gate_ref[...]
    outs = []
    for r in range(R):
        c = N_GATES * r
        o_win = jnp.concatenate([head(o_win_halves[0], r, half), head(o_win_halves[1], r, half)], axis=1)
        o = g[c:c + 1] * head(o_cmp, r, tq) + g[c + 1:c + 2] * head(o_slc, r, tq) + g[c + 2:c + 3] * o_win
        ms = jnp.mean(o * o, axis=0, keepdims=True)
        outs.append(o * lax.rsqrt(ms + 1e-6) * hg_ref[r * HEAD_DIM:(r + 1) * HEAD_DIM, :])
    o_ref[...] = jnp.concatenate(outs, axis=0).T.astype(o_ref.dtype)


def _nsa(q, cmp_kv, cmp_kv_t, ks, vst, kw, vwt, gates, agg_t, hg_attn, tq, kc):
    B, _, T, _ = q.shape
    G, R = N_KV_GROUPS, Q_PER_KV
    ncp = cmp_kv.shape[2]
    ns = T // SLC_LEN
    assert T % (2 * kc) == 0 and kc % tq == 0 and WINDOW % (tq // 2) == 0 and T >= WINDOW + tq
    assert ns <= LANES - HEAD_DIM, "block tags must fit in the upper lanes of a key row"
    whole = lambda b, g, i: (b, g, 0, 0)
    hg_lanes = jnp.broadcast_to(hg_attn.reshape(G, R * HEAD_DIM, 1), (G, R * HEAD_DIM, tq))
    return pl.pallas_call(
        functools.partial(_nsa_kernel, tq=tq, kc=kc),
        grid=(B, G, T // tq),
        in_specs=[
            pl.BlockSpec((None, R, tq, LANES), lambda b, g, i: (b, g, i, 0)),
            pl.BlockSpec((None, None, ncp, HEAD_DIM), lambda b, g, i: (0, b * G + g, 0, 0)),
            pl.BlockSpec((None, None, HEAD_DIM, ncp), lambda b, g, i: (1, b * G + g, 0, 0)),
            pl.BlockSpec((None, None, T, LANES), whole),
            pl.BlockSpec((None, None, 2 * HEAD_DIM, T), whole),
            pl.BlockSpec((None, None, T, HEAD_DIM), whole),
            pl.BlockSpec((None, None, 2 * HEAD_DIM, T), whole),
            pl.BlockSpec((None, None, _GATE_ROWS, tq), lambda b, g, i: (b, g, 0, i)),
            pl.BlockSpec((ns, ncp), lambda b, g, i: (0, 0)),
            pl.BlockSpec((None, R * HEAD_DIM, tq), lambda b, g, i: (g, 0, 0)),
        ],
        out_specs=pl.BlockSpec((None, tq, R * HEAD_DIM), lambda b, g, i: (b, i, g)),
        out_shape=jax.ShapeDtypeStruct((B, T, ATTN_WIDTH), MXU_DTYPE),
        scratch_shapes=[pltpu.VMEM((kc, R * tq), F32), pltpu.VMEM((kc, R * tq), F32),
                        pltpu.VMEM((2 * HEAD_DIM, R * tq), F32)],
        compiler_params=pltpu.CompilerParams(
            dimension_semantics=("arbitrary", "arbitrary", "arbitrary"), vmem_limit_bytes=VMEM_LIMIT),
    )(q, cmp_kv, cmp_kv_t, ks, vst, kw, vwt, gates, agg_t, hg_lanes)


def _outproj_kernel(conv_ref, attn_ref, x_ref, w_ref, g_ref, b_ref, o_ref):
    mix = _dot(conv_ref[...], w_ref[0:CONV_WIDTH, :]) + _dot(attn_ref[...], w_ref[CONV_WIDTH:, :])
    o_ref[...] = _layer_norm(ALPHA * x_ref[...] + mix, g_ref[...], b_ref[...])


def _outproj(conv_n, attn_n, x, w_out, g, b, tm):
    N, D = x.shape
    row = lambda i: (i, 0)
    const = lambda i: (0, 0)
    return pl.pallas_call(
        _outproj_kernel,
        grid=(N // tm,),
        in_specs=[
            pl.BlockSpec((tm, CONV_WIDTH), row), pl.BlockSpec((tm, ATTN_WIDTH), row),
            pl.BlockSpec((tm, D), row), pl.BlockSpec((D, D), const),
            pl.BlockSpec((1, D), const), pl.BlockSpec((1, D), const),
        ],
        out_specs=pl.BlockSpec((tm, D), row),
        out_shape=jax.ShapeDtypeStruct((N, D), F32),
        compiler_params=pltpu.CompilerParams(
            dimension_semantics=("arbitrary",), vmem_limit_bytes=VMEM_LIMIT),
    )(conv_n, attn_n, x, w_out, g, b)


def _ffn_kernel(x_ref, halo_ref, wup_ref, cw_ref, cb_ref, wdn_ref, g_ref, b_ref, o_ref, ubuf, act_ref,
                *, tiles_per_seq, fc):
    tm = x_ref.shape[0]
    x = x_ref[...]
    first = (pl.program_id(0) % tiles_per_seq) == 0
    halo = jnp.where(first, 0.0, halo_ref[...])
    xb = jnp.concatenate([halo, x], axis=0).astype(MXU_DTYPE)

    def up_conv(col0, buf):
        buf[...] = _dot(xb, wup_ref[:, col0:col0 + fc])
        return (cw_ref[2:3, col0:col0 + fc] * buf[SUBLANES:SUBLANES + tm, :]
                + cw_ref[1:2, col0:col0 + fc] * buf[SUBLANES - 1:SUBLANES - 1 + tm, :]
                + cw_ref[0:1, col0:col0 + fc] * buf[SUBLANES - 2:SUBLANES - 2 + tm, :]
                + cb_ref[:, col0:col0 + fc])

    for c in range(D_FF // fc):
        val = up_conv(c * fc, ubuf.at[c % 2, 0])
        gate = up_conv(D_FF + c * fc, ubuf.at[c % 2, 1])
        act_ref[:, c * fc:(c + 1) * fc] = (gate * jax.nn.sigmoid(gate) * val).astype(act_ref.dtype)
    o_ref[...] = _layer_norm(ALPHA * x + _dot(act_ref[...], wdn_ref[...]), g_ref[...], b_ref[...])


def _ffn(x, w_up, conv_w, conv_b, w_down, g, b, tm, tiles_per_seq, fc):
    N, D = x.shape
    row = lambda i: (i, 0)
    const = lambda i: (0, 0)
    halo_blocks = tm // SUBLANES
    return pl.pallas_call(
        functools.partial(_ffn_kernel, tiles_per_seq=tiles_per_seq, fc=fc),
        grid=(N // tm,),
        in_specs=[
            pl.BlockSpec((tm, D), row),
            pl.BlockSpec((SUBLANES, D), lambda i: (jnp.maximum(i * halo_blocks - 1, 0), 0)),
            pl.BlockSpec((D, 2 * D_FF), const, pipeline_mode=pl.Buffered(1)),
            pl.BlockSpec((3, 2 * D_FF), const),
            pl.BlockSpec((1, 2 * D_FF), const),
            pl.BlockSpec((D_FF, D), const, pipeline_mode=pl.Buffered(1)),
            pl.BlockSpec((1, D), const), pl.BlockSpec((1, D), const),
        ],
        out_specs=pl.BlockSpec((tm, D), row),
        out_shape=jax.ShapeDtypeStruct((N, D), F32),
        scratch_shapes=[pltpu.VMEM((2, 2, tm + SUBLANES, fc), F32), pltpu.VMEM((tm, D_FF), MXU_DTYPE)],
        compiler_params=pltpu.CompilerParams(
            dimension_semantics=("arbitrary",), vmem_limit_bytes=VMEM_LIMIT),
    )(x, x, w_up, conv_w, conv_b, w_down, g, b)


def _rope_lane_tables(T):
    half = ROPE_DIM // 2
    inv_freq = ROPE_THETA ** (-(np.arange(half, dtype=np.float32) * 2.0 / ROPE_DIM))
    ang = jnp.arange(T, dtype=F32)[:, None] * jnp.asarray(inv_freq, F32)[None]
    cos, sin = jnp.cos(ang), jnp.sin(ang)
    rest = HEAD_DIM - ROPE_DIM
    one, zero = jnp.ones((T, rest), F32), jnp.zeros((T, rest), F32)
    zh = jnp.zeros((T, half), F32)
    cos_t = jnp.concatenate([cos, cos, one], -1)
    rot_t = jnp.concatenate([-sin, sin, zero], -1)
    rep = LANES // HEAD_DIM
    return tuple(jnp.tile(t, (1, rep)) for t in (cos_t, rot_t))


def _group_mean_matrix():
    grp = np.arange(MXU_COLS) // HEAD_DIM
    return jnp.asarray((grp[:, None] == grp[None, :]) / HEAD_DIM, dtype=MXU_DTYPE)


def _slc_aggregation_t(T):
    nc = T // CMP_STRIDE - CMP_LEN // CMP_STRIDE + 1
    ns = T // SLC_LEN
    sc = np.arange(nc)[None, :] * CMP_STRIDE
    ss = np.arange(ns)[:, None] * SLC_LEN
    ov = np.clip(np.minimum(sc + CMP_LEN, ss + SLC_LEN) - np.maximum(sc, ss), 0, None) / CMP_LEN
    ov = np.pad(ov, ((0, 0), (0, T // CMP_STRIDE - nc)))
    return jnp.asarray(ov, dtype=MXU_DTYPE)


def _split_in_weights(w):
    w = w.astype(MXU_DTYPE)
    src = np.concatenate([np.arange(_W_VS), np.arange(_W_KW, _W_VW)])
    lanes = np.arange(LANES)
    rotated = (lanes % HEAD_DIM) < ROPE_DIM
    keep = [np.ones(src.shape, bool)]
    partner_src = []
    for c0, width in ((_OFF_Q, ATTN_WIDTH), (_OFF_KSW, 2 * KV_WIDTH), (_OFF_KC, KV_WIDTH)):
        for b0 in range(c0, c0 + width, LANES):
            partner_src.append(src[b0 + (lanes ^ (ROPE_DIM // 2))])
            keep.append(rotated)
    cols = np.concatenate([src] + partner_src)
    w_main = jnp.where(np.concatenate(keep)[None, None, :], w[:, :, cols], 0)
    per_group = Q_PER_KV * N_GATES
    t_cols = [np.arange(_W_VS, _W_KW), np.arange(_W_VW, _W_GATE)]
    t_keep = [np.ones(2 * KV_WIDTH, bool)]
    for g in range(N_KV_GROUPS):
        rows = np.arange(_GATE_ROWS)
        t_cols.append(_W_GATE + g * per_group + np.minimum(rows, per_group - 1))
        t_keep.append(rows < per_group)
    w_t = jnp.where(np.concatenate(t_keep)[None, None, :], w[:, :, np.concatenate(t_cols)], 0)
    return w_main, jnp.swapaxes(w_t, 1, 2)


def kernel(x, w_in, short_conv_w, cmp_pe, cmp_w1, cmp_b1, cmp_w2, cmp_b2, head_norm_g, w_out, ln1_g, ln1_b,
           w_up, ffn_conv_w, ffn_conv_b, w_down, ln2_g, ln2_b):
    B, T, D = x.shape
    tm = min(512, T)
    cos_t, rot_t = _rope_lane_tables(T)
    gmean = _group_mean_matrix()
    agg_t = _slc_aggregation_t(T)
    w_main, w_t = _split_in_weights(w_in)
    cmp_w1_b, cmp_w2_b = cmp_w1.astype(MXU_DTYPE), cmp_w2.astype(MXU_DTYPE)
    w_out_b, w_up_b, w_down_b = w_out.astype(MXU_DTYPE), w_up.astype(MXU_DTYPE), w_down.astype(MXU_DTYPE)
    cmp_pe_flat = cmp_pe.reshape(DEPTH, 2, 1, CMP_LEN * HEAD_DIM)
    for i in range(DEPTH):
        hg = head_norm_g[i]
        conv_n, q, cv, ks, kw, vst, vwt, gates = _inproj(
            x, w_main[i], w_t[i], short_conv_w[i], cos_t, rot_t, hg[None, :CONV_WIDTH], gmean, tm)
        cmp_kv, cmp_kv_t = _compress(
            cv.reshape(2, B * N_KV_GROUPS, T, HEAD_DIM), cmp_pe_flat[i], cmp_w1_b[i], cmp_b1[i], cmp_w2_b[i],
            cmp_b2[i])
        attn_n = _nsa(q, cmp_kv, cmp_kv_t, ks, vst, kw, vwt, gates, agg_t, hg[CONV_WIDTH:], tq=256, kc=512)
        x2 = _outproj(conv_n.reshape(B * T, CONV_WIDTH), attn_n.reshape(B * T, ATTN_WIDTH),
                      x.reshape(B * T, D), w_out_b[i], ln1_g[i][None], ln1_b[i][None], tm)
        x2 = _ffn(x2, w_up_b[i], ffn_conv_w[i], ffn_conv_b[i][None], w_down_b[i], ln2_g[i][None],
                  ln2_b[i][None], tm, T // tm, 256)
        x = x2.reshape(B, T, D)
    return x
```

```python
import functools

import numpy as np
import jax
import jax.numpy as jnp
from jax import lax
from jax.experimental import pallas as pl
from jax.experimental.pallas import tpu as pltpu

D_MODEL = 1024
DEPTH = 2
HEAD_DIM = 64
CONV_WIDTH = 512
ATTN_WIDTH = 512
N_Q_HEADS = 8
N_KV_GROUPS = 2
Q_PER_KV = 4
KV_WIDTH = N_KV_GROUPS * HEAD_DIM
N_GATES = 3
CMP_LEN = 32
CMP_STRIDE = 16
CMP_HIDDEN = 256
SLC_LEN = 64
SLC_TOPK = 16
WINDOW = 512
ROPE_THETA = 500000.0
ROPE_DIM = 16
D_FF = 2816
ALPHA = (2.0 * DEPTH) ** 0.25
NEG = -1e30
BIG = 1e9

F32 = jnp.float32
MXU_DTYPE = jnp.bfloat16
LANES = 128
SUBLANES = 8
MXU_COLS = 256
VMEM_LIMIT = 56 * 1024 * 1024

_W_VS, _W_KW, _W_VW, _W_GATE, _W_END = 2432, 2560, 2688, 2816, 2840
_OFF_B, _OFF_C, _OFF_H, _OFF_Q, _OFF_KC, _OFF_KSW = 0, 512, 1024, 1536, 2048, 2304
_OFF_PQ, _OFF_PKSW, _OFF_PKC = 2560, 3072, 3328
_MAIN_COLS = 3456
LOG2E = 1.4426950408889634
_GATE_ROWS = 16
_ROW_VS, _ROW_VW, _ROW_GATE = 0, KV_WIDTH, 2 * KV_WIDTH
_T_ROWS = 2 * KV_WIDTH + N_KV_GROUPS * _GATE_ROWS

_NT = (((1,), (1,)), ((), ()))


def _dot(a, b):
    return jnp.dot(a, b, preferred_element_type=F32)


def _dot_nt(a, b):
    return lax.dot_general(a, b, _NT, preferred_element_type=F32)


def _layer_norm(y, g, b):
    mu = jnp.mean(y, axis=-1, keepdims=True)
    d = y - mu
    var = jnp.mean(d * d, axis=-1, keepdims=True)
    return d * lax.rsqrt(var + 1e-5) * g + b


def _inproj_kernel(x_ref, w_ref, wt_ref, cw_ref, cos_ref, rot_ref, hg_ref, gmean_ref,
                   conv_ref, q_ref, cv_ref, ks_ref, kw_ref, vst_ref, vwt_ref, gate_ref, zbuf):
    tm = x_ref.shape[0]
    xb = x_ref[...].astype(MXU_DTYPE)

    def mm(c0, width):
        return _dot(xb, w_ref[:, c0:c0 + width])

    @pl.when(pl.program_id(1) == 0)
    def _():
        zbuf[0:SUBLANES, :] = jnp.zeros((SUBLANES, CONV_WIDTH), F32)

    def conv_block(c0):
        cs = slice(c0, c0 + MXU_COLS)
        z = mm(_OFF_C + c0, MXU_COLS) * mm(_OFF_H + c0, MXU_COLS)
        zbuf[SUBLANES:SUBLANES + tm, cs] = z
        conv = (cw_ref[2:3, cs] * z + cw_ref[1:2, cs] * zbuf[SUBLANES - 1:SUBLANES - 1 + tm, cs]
                + cw_ref[0:1, cs] * zbuf[SUBLANES - 2:SUBLANES - 2 + tm, cs])
        zbuf[0:SUBLANES, cs] = zbuf[tm:tm + SUBLANES, cs]
        y = mm(_OFF_B + c0, MXU_COLS) * conv
        y2 = y * y
        y2_hi = y2.astype(MXU_DTYPE)
        y2_lo = (y2 - y2_hi.astype(F32)).astype(MXU_DTYPE)
        ms = _dot(y2_hi, gmean_ref[...]) + _dot(y2_lo, gmean_ref[...])
        conv_ref[:, cs] = (y * lax.rsqrt(ms + 1e-6) * hg_ref[:, cs]).astype(conv_ref.dtype)

    def rope(v, partner):
        return v * cos_ref[...] + partner * rot_ref[...]

    def split_heads(v, ref, lead):
        for g in range(N_KV_GROUPS):
            ref[lead + (g,)] = v[:, g * HEAD_DIM:(g + 1) * HEAD_DIM].astype(ref.dtype)

    lane = lax.broadcasted_iota(jnp.int32, (tm, LANES), 1)
    low = lane < HEAD_DIM

    def head_pair(v):
        return v, pltpu.roll(v, HEAD_DIM, 1)

    def q_block(c):
        r4 = mm(_OFF_Q + c * MXU_COLS, MXU_COLS)
        p4 = mm(_OFF_PQ + c * MXU_COLS, MXU_COLS)
        for i in range(MXU_COLS // LANES):
            sl = slice(i * LANES, (i + 1) * LANES)
            r = rope(r4[:, sl], p4[:, sl]) * (HEAD_DIM ** -0.5 * LOG2E)
            for j, h in enumerate(head_pair(r)):
                q_ref[4 * c + 2 * i + j] = jnp.where(low, h, 0.0).astype(q_ref.dtype)

    conv_block(0)
    q_block(0)
    conv_block(MXU_COLS)
    q_block(1)
    kvc = mm(_OFF_KC, 2 * KV_WIDTH)
    split_heads(rope(kvc[:, :KV_WIDTH], mm(_OFF_PKC, KV_WIDTH)), cv_ref, (0,))
    split_heads(kvc[:, KV_WIDTH:], cv_ref, (1,))
    ksw = mm(_OFF_KSW, 2 * KV_WIDTH)
    pksw = mm(_OFF_PKSW, 2 * KV_WIDTH)
    pos = pl.program_id(1) * tm + lax.broadcasted_iota(jnp.int32, (tm, LANES), 0)
    block_tag = jnp.where(pos // SLC_LEN == lane - HEAD_DIM, NEG, 0.0)
    for g, h in enumerate(head_pair(rope(ksw[:, :KV_WIDTH], pksw[:, :KV_WIDTH]))):
        ks_ref[g] = jnp.where(low, h, block_tag).astype(ks_ref.dtype)
    split_heads(rope(ksw[:, KV_WIDTH:], pksw[:, KV_WIDTH:]), kw_ref, ())

    vt = _dot_nt(wt_ref[...], xb)
    ones = jnp.ones((HEAD_DIM, tm), vst_ref.dtype)
    for g in range(N_KV_GROUPS):
        for ref, row0 in ((vst_ref, _ROW_VS), (vwt_ref, _ROW_VW)):
            ref[g, 0:HEAD_DIM, :] = vt[row0 + g * HEAD_DIM:row0 + (g + 1) * HEAD_DIM].astype(ref.dtype)
            ref[g, HEAD_DIM:2 * HEAD_DIM, :] = ones
        gate_ref[g] = jax.nn.sigmoid(vt[_ROW_GATE + g * _GATE_ROWS:_ROW_GATE + (g + 1) * _GATE_ROWS])


def _inproj(x, w_main, w_t, conv_w, cos_t, rot_t, hg_conv, gmean, tm):
    B, T, D = x.shape
    G = N_KV_GROUPS
    kv_shape = jax.ShapeDtypeStruct((B, G, T, HEAD_DIM), MXU_DTYPE)
    kv_spec = pl.BlockSpec((None, G, tm, HEAD_DIM), lambda b, i: (b, 0, i, 0))
    vt_shape = jax.ShapeDtypeStruct((B, G, 2 * HEAD_DIM, T), MXU_DTYPE)
    vt_spec = pl.BlockSpec((None, G, 2 * HEAD_DIM, tm), lambda b, i: (b, 0, 0, i))
    tab_spec = pl.BlockSpec((tm, LANES), lambda b, i: (i, 0))
    const2 = lambda b, i: (0, 0)
    return pl.pallas_call(
        _inproj_kernel,
        grid=(B, T // tm),
        in_specs=[
            pl.BlockSpec((None, tm, D), lambda b, i: (b, i, 0)),
            pl.BlockSpec((D, _MAIN_COLS), const2),
            pl.BlockSpec((_T_ROWS, D), const2),
            pl.BlockSpec((3, CONV_WIDTH), const2),
            tab_spec, tab_spec,
            pl.BlockSpec((1, CONV_WIDTH), const2),
            pl.BlockSpec((MXU_COLS, MXU_COLS), const2),
        ],
        out_specs=[
            pl.BlockSpec((None, tm, CONV_WIDTH), lambda b, i: (b, i, 0)),
            pl.BlockSpec((None, N_Q_HEADS, tm, LANES), lambda b, i: (b, 0, i, 0)),
            pl.BlockSpec((2, None, G, tm, HEAD_DIM), lambda b, i: (0, b, 0, i, 0)),
            pl.BlockSpec((None, G, tm, LANES), lambda b, i: (b, 0, i, 0)),
            kv_spec, vt_spec, vt_spec,
            pl.BlockSpec((None, G, _GATE_ROWS, tm), lambda b, i: (b, 0, 0, i)),
        ],
        out_shape=[
            jax.ShapeDtypeStruct((B, T, CONV_WIDTH), MXU_DTYPE),
            jax.ShapeDtypeStruct((B, N_Q_HEADS, T, LANES), MXU_DTYPE),
            jax.ShapeDtypeStruct((2, B, G, T, HEAD_DIM), F32),
            jax.ShapeDtypeStruct((B, G, T, LANES), MXU_DTYPE),
            kv_shape, vt_shape, vt_shape,
            jax.ShapeDtypeStruct((B, G, _GATE_ROWS, T), F32),
        ],
        scratch_shapes=[pltpu.VMEM((tm + 2 * SUBLANES, CONV_WIDTH), F32)],
        compiler_params=pltpu.CompilerParams(
            dimension_semantics=("arbitrary", "arbitrary"), vmem_limit_bytes=VMEM_LIMIT),
    )(x, w_main, w_t, conv_w, cos_t, rot_t, hg_conv, gmean)


def _compress_kernel(a_ref, pe_ref, w1_ref, b1_ref, w2_ref, b2_ref, w2t_ref, b2t_ref, out_ref, outt_ref,
                     sub_ref, bbuf):
    nsub = sub_ref.shape[0]
    half = CMP_STRIDE * HEAD_DIM
    for j in range(CMP_STRIDE):
        sub_ref[:, j * HEAD_DIM:(j + 1) * HEAD_DIM] = (
            a_ref[pl.ds(j, nsub, stride=CMP_STRIDE), :].astype(sub_ref.dtype))
    sub = sub_ref[...]
    top = _dot(sub, w1_ref[0:half, :])
    bbuf[0:nsub, :] = _dot(sub, w1_ref[half:2 * half, :])
    bbuf[nsub:nsub + SUBLANES, :] = jnp.zeros((SUBLANES, CMP_HIDDEN), F32)
    pe_rows = jnp.broadcast_to(pe_ref[...], (SUBLANES, 2 * half)).astype(MXU_DTYPE)
    const = _dot(pe_rows, w1_ref[...])[0:1, :] + b1_ref[...]
    h = top + bbuf[1:nsub + 1, :] + const
    act = jax.nn.gelu(h).astype(MXU_DTYPE)
    out_ref[...] = (_dot(act, w2_ref[...]) + b2_ref[...]).astype(out_ref.dtype)
    outt_ref[...] = (_dot_nt(w2t_ref[...], act) + b2t_ref[...]).astype(outt_ref.dtype)


def _compress(a, pe, w1, b1, w2, b2):
    _, BG, T, _ = a.shape
    nsub = T // CMP_STRIDE
    width = CMP_STRIDE * HEAD_DIM
    sel = lambda k, n: (k, 0, 0)
    per = lambda k, n: (k, n, 0, 0)
    return pl.pallas_call(
        _compress_kernel,
        grid=(2, BG),
        in_specs=[
            pl.BlockSpec((None, None, T, HEAD_DIM), per),
            pl.BlockSpec((None, 1, 2 * width), sel),
            pl.BlockSpec((None, 2 * width, CMP_HIDDEN), sel),
            pl.BlockSpec((None, 1, CMP_HIDDEN), sel),
            pl.BlockSpec((None, CMP_HIDDEN, HEAD_DIM), sel),
            pl.BlockSpec((None, 1, HEAD_DIM), sel),
            pl.BlockSpec((None, HEAD_DIM, CMP_HIDDEN), sel),
            pl.BlockSpec((None, HEAD_DIM, 1), sel),
        ],
        out_specs=[pl.BlockSpec((None, None, nsub, HEAD_DIM), per),
                   pl.BlockSpec((None, None, HEAD_DIM, nsub), per)],
        out_shape=[jax.ShapeDtypeStruct((2, BG, nsub, HEAD_DIM), MXU_DTYPE),
                   jax.ShapeDtypeStruct((2, BG, HEAD_DIM, nsub), MXU_DTYPE)],
        scratch_shapes=[pltpu.VMEM((nsub, width), MXU_DTYPE), pltpu.VMEM((nsub + SUBLANES, CMP_HIDDEN), F32)],
        compiler_params=pltpu.CompilerParams(
            dimension_semantics=("arbitrary", "arbitrary"), vmem_limit_bytes=VMEM_LIMIT),
    )(a, pe, w1, b1[:, None, :], w2, b2[:, None, :], jnp.swapaxes(w2, 1, 2), b2[:, :, None])


def _rank_accumulate(score, ranks, i0, i1):
    tq = score.shape[1]
    local = lax.broadcasted_iota(jnp.int32, (SUBLANES, tq), 0)
    ranks = list(ranks)
    for i in range(i0, i1):
        row = jnp.broadcast_to(score[i:i + 1, :], (SUBLANES, tq))
        for j in range(len(ranks)):
            slab = score[SUBLANES * j:SUBLANES * (j + 1), :]
            ge = jnp.where(row >= slab, 1.0, 0.0)
            gt = jnp.where(row > slab, 1.0, 0.0)
            if i < SUBLANES * j:
                ahead = ge
            elif i >= SUBLANES * (j + 1):
                ahead = gt
            else:
                ahead = jnp.where(local > i - SUBLANES * j, ge, gt)
            ranks[j] = ranks[j] + ahead
    return ranks


def _nsa_kernel(q_ref, kcmp_ref, vcmpt_ref, ks_ref, vst_ref, kw_ref, vwt_ref, gate_ref, aggt_ref, hg_ref,
                o_ref, sa_ref, sb_ref, acc_ref, *, tq, kc):
    R = Q_PER_KV
    rows = R * tq
    T = ks_ref.shape[0]
    ns = T // SLC_LEN
    ncp = kcmp_ref.shape[0]
    q0 = pl.program_id(2) * tq
    t_row = q0 + lax.broadcasted_iota(jnp.int32, (1, tq), 1)
    q_pad = q_ref[...]
    q64 = q_pad.reshape(rows, LANES)[:, :HEAD_DIM]

    def all_heads(a):
        return jnp.concatenate([a] * R, axis=1)

    def head(a, r, width):
        return a[:, r * width:(r + 1) * width]

    half = tq // 2
    wl = WINDOW + half

    def window_half(h):
        t_half = t_row[:, h * half:(h + 1) * half]
        ws = pl.multiple_of(jnp.maximum(q0 + h * half - WINDOW, 0), half)
        wpos = ws + lax.broadcasted_iota(jnp.int32, (wl, half), 0)
        wbias = jnp.where((wpos <= t_half) & (wpos > t_half - WINDOW), 0.0, NEG)
        qh = q64.reshape(R, tq, HEAD_DIM)[:, h * half:(h + 1) * half, :].reshape(R * half, HEAD_DIM)
        s = _dot_nt(kw_ref[pl.ds(ws, wl), :], qh) + jnp.concatenate([wbias] * R, axis=1)
        p = jnp.exp2(s - jnp.max(s, axis=0, keepdims=True)).astype(MXU_DTYPE)
        ow = _dot(vwt_ref[:, pl.ds(ws, wl)], p)
        return ow[:HEAD_DIM] / ow[HEAD_DIM:]

    cmp_end = lax.broadcasted_iota(jnp.int32, (ncp, tq), 0) * CMP_STRIDE + (CMP_LEN - 1)
    cbias = jnp.where(cmp_end <= t_row, 0.0, NEG)
    s = _dot_nt(kcmp_ref[...], q64) + all_heads(cbias)
    p = jnp.exp2(s - jnp.max(s, axis=0, keepdims=True))
    l = jnp.sum(p, axis=0, keepdims=True)
    p = p * (all_heads(jnp.where(t_row >= CMP_LEN - 1, 1.0, 0.0)) / l)
    o_cmp = _dot(vcmpt_ref[...], p.astype(MXU_DTYPE))

    psum = head(p, 0, tq) + head(p, 1, tq) + head(p, 2, tq) + head(p, 3, tq)
    p_hi = psum.astype(MXU_DTYPE)
    p_lo = (psum - p_hi.astype(F32)).astype(MXU_DTYPE)
    imp = _dot(aggt_ref[...], p_hi) + _dot(aggt_ref[...], p_lo)
    blk = lax.broadcasted_iota(jnp.int32, (ns, tq), 0)
    tb = (q0 + lax.broadcasted_iota(jnp.int32, (ns, tq), 1)) // SLC_LEN
    forced = (blk == 0) | (blk == tb) | (blk == tb - 1)
    score = jnp.where(forced, BIG, imp)
    score = jnp.where(blk <= tb, score, -BIG)
    ranks = [jnp.zeros((SUBLANES, tq), F32) for _ in range(ns // SUBLANES)]
    ranks = _rank_accumulate(score, ranks, 0, ns // 2)
    o_win_halves = [window_half(0)]
    ranks = _rank_accumulate(score, ranks, ns // 2, ns)
    unselected = jnp.where(jnp.concatenate(ranks, axis=0) < min(SLC_TOPK, ns), 0.0, 1.0)
    tag_rows = [jnp.zeros((HEAD_DIM, tq), F32), unselected]
    if ns < LANES - HEAD_DIM:
        tag_rows.append(jnp.zeros((LANES - HEAD_DIM - ns, tq), F32))
    tag = jnp.concatenate(tag_rows, axis=0).T.astype(MXU_DTYPE)
    lane = lax.broadcasted_iota(jnp.int32, (R, tq, LANES), 2)
    q_aug = jnp.where(lane < HEAD_DIM, q_pad, tag[None]).reshape(rows, LANES)

    def scores(c):
        return _dot_nt(ks_ref[pl.ds(pl.multiple_of(c * kc, kc), kc), :], q_aug)

    def causal_bias(c):
        kpos = c * kc + lax.broadcasted_iota(jnp.int32, (kc, tq), 0)
        return all_heads(jnp.where(kpos <= t_row, 0.0, NEG))

    def consume(s, c, m):
        m_new = jnp.maximum(m, jnp.max(s, axis=0, keepdims=True))
        p = jnp.exp2(s - m_new).astype(MXU_DTYPE)
        pv = _dot(vst_ref[:, pl.ds(pl.multiple_of(c * kc, kc), kc)], p)
        acc_ref[...] = jnp.exp2(m - m_new) * acc_ref[...] + pv
        return m_new

    n_pairs = q0 // (2 * kc)
    acc_ref[...] = jnp.zeros((2 * HEAD_DIM, rows), F32)
    sa_ref[...] = scores(0)
    o_win_halves.append(window_half(1))

    def pair_step(j, m):
        sb_ref[...] = scores(2 * j + 1)
        m = consume(sa_ref[...], 2 * j, m)
        sa_ref[...] = scores(2 * j + 2)
        return consume(sb_ref[...], 2 * j + 1, m)

    m = lax.fori_loop(0, n_pairs, pair_step, jnp.full((1, rows), NEG, F32))
    c_last = 2 * n_pairs
    m = consume(sa_ref[...] + causal_bias(c_last), c_last, m)

    @pl.when(q0 + tq > (c_last + 1) * kc)
    def _():
        consume(scores(c_last + 1) + causal_bias(c_last + 1), c_last + 1, m)

    acc = acc_ref[...]
    o_slc = acc[:HEAD_DIM] / acc[HEAD_DIM:]

    g = gate_ref[...]
    outs = []
    for r in range(R):
        c = N_GATES * r
        o_win = jnp.concatenate([head(o_win_halves[0], r, half), head(o_win_halves[1], r, half)], axis=1)
        o = g[c:c + 1] * head(o_cmp, r, tq) + g[c + 1:c + 2] * head(o_slc, r, tq) + g[c + 2:c + 3] * o_win
        ms = jnp.mean(o * o, axis=0, keepdims=True)
        outs.append(o * lax.rsqrt(ms + 1e-6) * hg_ref[r * HEAD_DIM:(r + 1) * HEAD_DIM, :])
    o_ref[...] = jnp.concatenate(outs, axis=0).T.astype(o_ref.dtype)


def _nsa(q, cmp_kv, cmp_kv_t, ks, vst, kw, vwt, gates, agg_t, hg_attn, tq, kc):
    B, _, T, _ = q.shape
    G, R = N_KV_GROUPS, Q_PER_KV
    ncp = cmp_kv.shape[2]
    ns = T // SLC_LEN
    assert T % (2 * kc) == 0 and kc % tq == 0 and WINDOW % (tq // 2) == 0 and T >= WINDOW + tq
    assert ns <= LANES - HEAD_DIM, "block tags must fit in the upper lanes of a key row"
    whole = lambda b, g, i: (b, g, 0, 0)
    hg_lanes = jnp.broadcast_to(hg_attn.reshape(G, R * HEAD_DIM, 1), (G, R * HEAD_DIM, tq))
    return pl.pallas_call(
        functools.partial(_nsa_kernel, tq=tq, kc=kc),
        grid=(B, G, T // tq),
        in_specs=[
            pl.BlockSpec((None, R, tq, LANES), lambda b, g, i: (b, g, i, 0)),
            pl.BlockSpec((None, None, ncp, HEAD_DIM), lambda b, g, i: (0, b * G + g, 0, 0)),
            pl.BlockSpec((None, None, HEAD_DIM, ncp), lambda b, g, i: (1, b * G + g, 0, 0)),
            pl.BlockSpec((None, None, T, LANES), whole),
            pl.BlockSpec((None, None, 2 * HEAD_DIM, T), whole),
            pl.BlockSpec((None, None, T, HEAD_DIM), whole),
            pl.BlockSpec((None, None, 2 * HEAD_DIM, T), whole),
            pl.BlockSpec((None, None, _GATE_ROWS, tq), lambda b, g, i: (b, g, 0, i)),
            pl.BlockSpec((ns, ncp), lambda b, g, i: (0, 0)),
            pl.BlockSpec((None, R * HEAD_DIM, tq), lambda b, g, i: (g, 0, 0)),
        ],
        out_specs=pl.BlockSpec((None, tq, R * HEAD_DIM), lambda b, g, i: (b, i, g)),
        out_shape=jax.ShapeDtypeStruct((B, T, ATTN_WIDTH), MXU_DTYPE),
        scratch_shapes=[pltpu.VMEM((kc, R * tq), F32), pltpu.VMEM((kc, R * tq), F32),
                        pltpu.VMEM((2 * HEAD_DIM, R * tq), F32)],
        compiler_params=pltpu.CompilerParams(
            dimension_semantics=("arbitrary", "arbitrary", "arbitrary"), vmem_limit_bytes=VMEM_LIMIT),
    )(q, cmp_kv, cmp_kv_t, ks, vst, kw, vwt, gates, agg_t, hg_lanes)


_HALO_ROWS = 16


def _mix_ffn_kernel(conv_ref, attn_ref, x_ref, hconv_ref, hattn_ref, hx_ref, wout_ref, g1_ref, b1_ref,
                    wup_ref, cw_ref, cb_ref, wdn_ref, g_ref, b_ref, o_ref, ubuf, act_ref,
                    *, tiles_per_seq, fc):
    tm = x_ref.shape[0]

    def mixer_out(conv, attn, res):
        mix = _dot(conv, wout_ref[0:CONV_WIDTH, :]) + _dot(attn, wout_ref[CONV_WIDTH:, :])
        return _layer_norm(ALPHA * res + mix, g1_ref[...], b1_ref[...])

    x = mixer_out(conv_ref[...], attn_ref[...], x_ref[...])
    first = (pl.program_id(0) % tiles_per_seq) == 0
    halo = mixer_out(hconv_ref[...], hattn_ref[...], hx_ref[...])[_HALO_ROWS - SUBLANES:]
    halo = jnp.where(first, 0.0, halo)
    xb = jnp.concatenate([halo, x], axis=0).astype(MXU_DTYPE)

    def up_conv(col0, buf):
        buf[...] = _dot(xb, wup_ref[:, col0:col0 + fc])
        return (cw_ref[2:3, col0:col0 + fc] * buf[SUBLANES:SUBLANES + tm, :]
                + cw_ref[1:2, col0:col0 + fc] * buf[SUBLANES - 1:SUBLANES - 1 + tm, :]
                + cw_ref[0:1, col0:col0 + fc] * buf[SUBLANES - 2:SUBLANES - 2 + tm, :]
                + cb_ref[:, col0:col0 + fc])

    for c in range(D_FF // fc):
        val = up_conv(c * fc, ubuf.at[c % 2, 0])
        gate = up_conv(D_FF + c * fc, ubuf.at[c % 2, 1])
        act_ref[:, c * fc:(c + 1) * fc] = (gate * jax.nn.sigmoid(gate) * val).astype(act_ref.dtype)
    o_ref[...] = _layer_norm(ALPHA * x + _dot(act_ref[...], wdn_ref[...]), g_ref[...], b_ref[...])


def _mix_ffn(conv_n, attn_n, x, w_out, g1, b1, w_up, conv_w, conv_b, w_down, g, b, tm, tiles_per_seq, fc):
    N, D = x.shape
    row = lambda i: (i, 0)
    const = lambda i: (0, 0)
    halo = lambda i: (jnp.maximum(i * (tm // _HALO_ROWS) - 1, 0), 0)
    return pl.pallas_call(
        functools.partial(_mix_ffn_kernel, tiles_per_seq=tiles_per_seq, fc=fc),
        grid=(N // tm,),
        in_specs=[
            pl.BlockSpec((tm, CONV_WIDTH), row), pl.BlockSpec((tm, ATTN_WIDTH), row), pl.BlockSpec((tm, D), row),
            pl.BlockSpec((_HALO_ROWS, CONV_WIDTH), halo), pl.BlockSpec((_HALO_ROWS, ATTN_WIDTH), halo),
            pl.BlockSpec((_HALO_ROWS, D), halo),
            pl.BlockSpec((D, D), const, pipeline_mode=pl.Buffered(1)),
            pl.BlockSpec((1, D), const), pl.BlockSpec((1, D), const),
            pl.BlockSpec((D, 2 * D_FF), const, pipeline_mode=pl.Buffered(1)),
            pl.BlockSpec((3, 2 * D_FF), const),
            pl.BlockSpec((1, 2 * D_FF), const),
            pl.BlockSpec((D_FF, D), const, pipeline_mode=pl.Buffered(1)),
            pl.BlockSpec((1, D), const), pl.BlockSpec((1, D), const),
        ],
        out_specs=pl.BlockSpec((tm, D), row),
        out_shape=jax.ShapeDtypeStruct((N, D), F32),
        scratch_shapes=[pltpu.VMEM((2, 2, tm + SUBLANES, fc), F32), pltpu.VMEM((tm, D_FF), MXU_DTYPE)],
        compiler_params=pltpu.CompilerParams(
            dimension_semantics=("arbitrary",), vmem_limit_bytes=VMEM_LIMIT),
    )(conv_n, attn_n, x, conv_n, attn_n, x, w_out, g1, b1, w_up, conv_w, conv_b, w_down, g, b)


def _rope_lane_tables(T):
    half = ROPE_DIM // 2
    inv_freq = ROPE_THETA ** (-(np.arange(half, dtype=np.float32) * 2.0 / ROPE_DIM))
    ang = jnp.arange(T, dtype=F32)[:, None] * jnp.asarray(inv_freq, F32)[None]
    cos, sin = jnp.cos(ang), jnp.sin(ang)
    rest = HEAD_DIM - ROPE_DIM
    one, zero = jnp.ones((T, rest), F32), jnp.zeros((T, rest), F32)
    zh = jnp.zeros((T, half), F32)
    cos_t = jnp.concatenate([cos, cos, one], -1)
    rot_t = jnp.concatenate([-sin, sin, zero], -1)
    rep = LANES // HEAD_DIM
    return tuple(jnp.tile(t, (1, rep)) for t in (cos_t, rot_t))


def _group_mean_matrix():
    grp = np.arange(MXU_COLS) // HEAD_DIM
    return jnp.asarray((grp[:, None] == grp[None, :]) / HEAD_DIM, dtype=MXU_DTYPE)


def _slc_aggregation_t(T):
    nc = T // CMP_STRIDE - CMP_LEN // CMP_STRIDE + 1
    ns = T // SLC_LEN
    sc = np.arange(nc)[None, :] * CMP_STRIDE
    ss = np.arange(ns)[:, None] * SLC_LEN
    ov = np.clip(np.minimum(sc + CMP_LEN, ss + SLC_LEN) - np.maximum(sc, ss), 0, None) / CMP_LEN
    ov = np.pad(ov, ((0, 0), (0, T // CMP_STRIDE - nc)))
    return jnp.asarray(ov, dtype=MXU_DTYPE)


def _split_in_weights(w):
    w = w.astype(MXU_DTYPE)
    plain = jnp.concatenate([w[..., :_W_VS], w[..., _W_KW:_W_VW]], axis=-1)
    half = ROPE_DIM // 2
    zeros = jnp.zeros(w.shape[:-1] + (HEAD_DIM - ROPE_DIM,), w.dtype)
    partners = []
    for c0, width in ((_OFF_Q, ATTN_WIDTH), (_OFF_KSW, 2 * KV_WIDTH), (_OFF_KC, KV_WIDTH)):
        for h0 in range(c0, c0 + width, HEAD_DIM):
            partners += [plain[..., h0 + half:h0 + ROPE_DIM], plain[..., h0:h0 + half], zeros]
    w_main = jnp.concatenate([plain] + partners, axis=-1)
    per_group = Q_PER_KV * N_GATES
    pad = jnp.zeros(w.shape[:-1] + (_GATE_ROWS - per_group,), w.dtype)
    cols = [w[..., _W_VS:_W_KW], w[..., _W_VW:_W_GATE]]
    for g in range(N_KV_GROUPS):
        cols += [w[..., _W_GATE + g * per_group:_W_GATE + (g + 1) * per_group], pad]
    return w_main, jnp.swapaxes(jnp.concatenate(cols, axis=-1), -1, -2)


def kernel(x, w_in, short_conv_w, cmp_pe, cmp_w1, cmp_b1, cmp_w2, cmp_b2, head_norm_g, w_out, ln1_g, ln1_b,
           w_up, ffn_conv_w, ffn_conv_b, w_down, ln2_g, ln2_b):
    B, T, D = x.shape
    tm = min(512, T)
    cos_t, rot_t = _rope_lane_tables(T)
    gmean = _group_mean_matrix()
    agg_t = _slc_aggregation_t(T)
    w_main, w_t = _split_in_weights(w_in)
    cmp_w1_b, cmp_w2_b = cmp_w1.astype(MXU_DTYPE), cmp_w2.astype(MXU_DTYPE)
    w_out_b, w_up_b, w_down_b = w_out.astype(MXU_DTYPE), w_up.astype(MXU_DTYPE), w_down.astype(MXU_DTYPE)
    cmp_pe_flat = cmp_pe.reshape(DEPTH, 2, 1, CMP_LEN * HEAD_DIM)
    for i in range(DEPTH):
        hg = head_norm_g[i]
        conv_n, q, cv, ks, kw, vst, vwt, gates = _inproj(
            x, w_main[i], w_t[i], short_conv_w[i], cos_t, rot_t, hg[None, :CONV_WIDTH], gmean, tm)
        cmp_kv, cmp_kv_t = _compress(
            cv.reshape(2, B * N_KV_GROUPS, T, HEAD_DIM), cmp_pe_flat[i], cmp_w1_b[i], cmp_b1[i], cmp_w2_b[i],
            cmp_b2[i])
        attn_n = _nsa(q, cmp_kv, cmp_kv_t, ks, vst, kw, vwt, gates, agg_t, hg[CONV_WIDTH:], tq=256, kc=512)
        x = _mix_ffn(conv_n.reshape(B * T, CONV_WIDTH), attn_n.reshape(B * T, ATTN_WIDTH), x.reshape(B * T, D),
                     w_out_b[i], ln1_g[i][None], ln1_b[i][None], w_up_b[i], ffn_conv_w[i], ffn_conv_b[i][None],
                     w_down_b[i], ln2_g[i][None], ln2_b[i][None], tm, T // tm, 256).reshape(B, T, D)
    return x
```

```python
import functools

import numpy as np
import jax
import jax.numpy as jnp
from jax import lax
from jax.experimental import pallas as pl
from jax.experimental.pallas import tpu as pltpu

D_MODEL = 1024
DEPTH = 2
HEAD_DIM = 64
CONV_WIDTH = 512
ATTN_WIDTH = 512
N_Q_HEADS = 8
N_KV_GROUPS = 2
Q_PER_KV = 4
KV_WIDTH = N_KV_GROUPS * HEAD_DIM
N_GATES = 3
CMP_LEN = 32
CMP_STRIDE = 16
CMP_HIDDEN = 256
SLC_LEN = 64
SLC_TOPK = 16
WINDOW = 512
ROPE_THETA = 500000.0
ROPE_DIM = 16
D_FF = 2816
ALPHA = (2.0 * DEPTH) ** 0.25
NEG = -1e30
BIG = 1e9

F32 = jnp.float32
MXU_DTYPE = jnp.bfloat16
LANES = 128
SUBLANES = 8
MXU_COLS = 256
VMEM_LIMIT = 56 * 1024 * 1024

_W_VS, _W_KW, _W_VW, _W_GATE, _W_END = 2432, 2560, 2688, 2816, 2840
_OFF_B, _OFF_C, _OFF_H, _OFF_Q, _OFF_KC, _OFF_KSW = 0, 512, 1024, 1536, 2048, 2304
_OFF_PQ, _OFF_PKSW, _OFF_PKC = 2560, 3072, 3328
_MAIN_COLS = 3456
LOG2E = 1.4426950408889634
_GATE_ROWS = 16
_ROW_VS, _ROW_VW, _ROW_GATE = 0, KV_WIDTH, 2 * KV_WIDTH
_T_ROWS = 2 * KV_WIDTH + N_KV_GROUPS * _GATE_ROWS

_NT = (((1,), (1,)), ((), ()))


def _dot(a, b):
    return jnp.dot(a, b, preferred_element_type=F32)


def _dot_nt(a, b):
    return lax.dot_general(a, b, _NT, preferred_element_type=F32)


def _layer_norm(y, g, b):
    mu = jnp.mean(y, axis=-1, keepdims=True)
    d = y - mu
    var = jnp.mean(d * d, axis=-1, keepdims=True)
    return d * lax.rsqrt(var + 1e-5) * g + b


def _inproj_kernel(x_ref, w_ref, wt_ref, cw_ref, cos_ref, rot_ref, hg_ref, gmean_ref,
                   conv_ref, q_ref, cv_ref, ks_ref, kw_ref, vst_ref, vwt_ref, gate_ref, zbuf):
    tm = x_ref.shape[0]
    xb = x_ref[...].astype(MXU_DTYPE)

    def mm(c0, width):
        return _dot(xb, w_ref[:, c0:c0 + width])

    @pl.when(pl.program_id(1) == 0)
    def _():
        zbuf[0:SUBLANES, :] = jnp.zeros((SUBLANES, CONV_WIDTH), F32)

    def conv_block(c0):
        cs = slice(c0, c0 + MXU_COLS)
        z = mm(_OFF_C + c0, MXU_COLS) * mm(_OFF_H + c0, MXU_COLS)
        zbuf[SUBLANES:SUBLANES + tm, cs] = z
        conv = (cw_ref[2:3, cs] * z + cw_ref[1:2, cs] * zbuf[SUBLANES - 1:SUBLANES - 1 + tm, cs]
                + cw_ref[0:1, cs] * zbuf[SUBLANES - 2:SUBLANES - 2 + tm, cs])
        zbuf[0:SUBLANES, cs] = zbuf[tm:tm + SUBLANES, cs]
        y = mm(_OFF_B + c0, MXU_COLS) * conv
        y2 = y * y
        y2_hi = y2.astype(MXU_DTYPE)
        y2_lo = (y2 - y2_hi.astype(F32)).astype(MXU_DTYPE)
        ms = _dot(y2_hi, gmean_ref[...]) + _dot(y2_lo, gmean_ref[...])
        conv_ref[:, cs] = (y * lax.rsqrt(ms + 1e-6) * hg_ref[:, cs]).astype(conv_ref.dtype)

    def rope(v, partner):
        return v * cos_ref[...] + partner * rot_ref[...]

    def split_heads(v, ref, lead):
        for g in range(N_KV_GROUPS):
            ref[lead + (g,)] = v[:, g * HEAD_DIM:(g + 1) * HEAD_DIM].astype(ref.dtype)

    lane = lax.broadcasted_iota(jnp.int32, (tm, LANES), 1)
    low = lane < HEAD_DIM

    def head_pair(v):
        return v, pltpu.roll(v, HEAD_DIM, 1)

    def q_block(c):
        r4 = mm(_OFF_Q + c * MXU_COLS, MXU_COLS)
        p4 = mm(_OFF_PQ + c * MXU_COLS, MXU_COLS)
        for i in range(MXU_COLS // LANES):
            sl = slice(i * LANES, (i + 1) * LANES)
            r = rope(r4[:, sl], p4[:, sl]) * (HEAD_DIM ** -0.5 * LOG2E)
            for j, h in enumerate(head_pair(r)):
                q_ref[4 * c + 2 * i + j] = jnp.where(low, h, 0.0).astype(q_ref.dtype)

    conv_block(0)
    q_block(0)
    conv_block(MXU_COLS)
    q_block(1)
    kvc = mm(_OFF_KC, 2 * KV_WIDTH)
    split_heads(rope(kvc[:, :KV_WIDTH], mm(_OFF_PKC, KV_WIDTH)), cv_ref, (0,))
    split_heads(kvc[:, KV_WIDTH:], cv_ref, (1,))
    ksw = mm(_OFF_KSW, 2 * KV_WIDTH)
    pksw = mm(_OFF_PKSW, 2 * KV_WIDTH)
    pos = pl.program_id(1) * tm + lax.broadcasted_iota(jnp.int32, (tm, LANES), 0)
    block_tag = jnp.where(pos // SLC_LEN == lane - HEAD_DIM, NEG, 0.0)
    for g, h in enumerate(head_pair(rope(ksw[:, :KV_WIDTH], pksw[:, :KV_WIDTH]))):
        ks_ref[g] = jnp.where(low, h, block_tag).astype(ks_ref.dtype)
    split_heads(rope(ksw[:, KV_WIDTH:], pksw[:, KV_WIDTH:]), kw_ref, ())

    vt = _dot_nt(wt_ref[...], xb)
    ones = jnp.ones((HEAD_DIM, tm), vst_ref.dtype)
    for g in range(N_KV_GROUPS):
        for ref, row0 in ((vst_ref, _ROW_VS), (vwt_ref, _ROW_VW)):
            ref[g, 0:HEAD_DIM, :] = vt[row0 + g * HEAD_DIM:row0 + (g + 1) * HEAD_DIM].astype(ref.dtype)
            ref[g, HEAD_DIM:2 * HEAD_DIM, :] = ones
        gate_ref[g] = jax.nn.sigmoid(vt[_ROW_GATE + g * _GATE_ROWS:_ROW_GATE + (g + 1) * _GATE_ROWS])


def _inproj(x, w_main, w_t, conv_w, cos_t, rot_t, hg_conv, gmean, tm):
    B, T, D = x.shape
    G = N_KV_GROUPS
    kv_shape = jax.ShapeDtypeStruct((B, G, T, HEAD_DIM), MXU_DTYPE)
    kv_spec = pl.BlockSpec((None, G, tm, HEAD_DIM), lambda b, i: (b, 0, i, 0))
    vt_shape = jax.ShapeDtypeStruct((B, G, 2 * HEAD_DIM, T), MXU_DTYPE)
    vt_spec = pl.BlockSpec((None, G, 2 * HEAD_DIM, tm), lambda b, i: (b, 0, 0, i))
    tab_spec = pl.BlockSpec((tm, LANES), lambda b, i: (i, 0))
    const2 = lambda b, i: (0, 0)
    return pl.pallas_call(
        _inproj_kernel,
        grid=(B, T // tm),
        in_specs=[
            pl.BlockSpec((None, tm, D), lambda b, i: (b, i, 0)),
            pl.BlockSpec((D, _MAIN_COLS), const2),
            pl.BlockSpec((_T_ROWS, D), const2),
            pl.BlockSpec((3, CONV_WIDTH), const2),
            tab_spec, tab_spec,
            pl.BlockSpec((1, CONV_WIDTH), const2),
            pl.BlockSpec((MXU_COLS, MXU_COLS), const2),
        ],
        out_specs=[
            pl.BlockSpec((None, tm, CONV_WIDTH), lambda b, i: (b, i, 0)),
            pl.BlockSpec((None, N_Q_HEADS, tm, LANES), lambda b, i: (b, 0, i, 0)),
            pl.BlockSpec((2, None, G, tm, HEAD_DIM), lambda b, i: (0, b, 0, i, 0)),
            pl.BlockSpec((None, G, tm, LANES), lambda b, i: (b, 0, i, 0)),
            kv_spec, vt_spec, vt_spec,
            pl.BlockSpec((None, G, _GATE_ROWS, tm), lambda b, i: (b, 0, 0, i)),
        ],
        out_shape=[
            jax.ShapeDtypeStruct((B, T, CONV_WIDTH), MXU_DTYPE),
            jax.ShapeDtypeStruct((B, N_Q_HEADS, T, LANES), MXU_DTYPE),
            jax.ShapeDtypeStruct((2, B, G, T, HEAD_DIM), F32),
            jax.ShapeDtypeStruct((B, G, T, LANES), MXU_DTYPE),
            kv_shape, vt_shape, vt_shape,
            jax.ShapeDtypeStruct((B, G, _GATE_ROWS, T), F32),
        ],
        scratch_shapes=[pltpu.VMEM((tm + 2 * SUBLANES, CONV_WIDTH), F32)],
        compiler_params=pltpu.CompilerParams(
            dimension_semantics=("arbitrary", "arbitrary"), vmem_limit_bytes=VMEM_LIMIT),
    )(x, w_main, w_t, conv_w, cos_t, rot_t, hg_conv, gmean)


def _compress_kernel(a_ref, pe_ref, w1_ref, b1_ref, w2_ref, b2_ref, w2t_ref, b2t_ref, out_ref, outt_ref,
                     sub_ref, bbuf):
    nsub = sub_ref.shape[0]
    half = CMP_STRIDE * HEAD_DIM
    for j in range(CMP_STRIDE):
        sub_ref[:, j * HEAD_DIM:(j + 1) * HEAD_DIM] = (
            a_ref[pl.ds(j, nsub, stride=CMP_STRIDE), :].astype(sub_ref.dtype))
    sub = sub_ref[...]
    top = _dot(sub, w1_ref[0:half, :])
    bbuf[0:nsub, :] = _dot(sub, w1_ref[half:2 * half, :])
    bbuf[nsub:nsub + SUBLANES, :] = jnp.zeros((SUBLANES, CMP_HIDDEN), F32)
    pe_rows = jnp.broadcast_to(pe_ref[...], (SUBLANES, 2 * half)).astype(MXU_DTYPE)
    const = _dot(pe_rows, w1_ref[...])[0:1, :] + b1_ref[...]
    h = top + bbuf[1:nsub + 1, :] + const
    act = jax.nn.gelu(h).astype(MXU_DTYPE)
    out_ref[...] = (_dot(act, w2_ref[...]) + b2_ref[...]).astype(out_ref.dtype)
    outt_ref[...] = (_dot_nt(w2t_ref[...], act) + b2t_ref[...]).astype(outt_ref.dtype)


def _compress(a, pe, w1, b1, w2, b2):
    _, BG, T, _ = a.shape
    nsub = T // CMP_STRIDE
    width = CMP_STRIDE * HEAD_DIM
    sel = lambda k, n: (k, 0, 0)
    per = lambda k, n: (k, n, 0, 0)
    return pl.pallas_call(
        _compress_kernel,
        grid=(2, BG),
        in_specs=[
            pl.BlockSpec((None, None, T, HEAD_DIM), per),
            pl.BlockSpec((None, 1, 2 * width), sel),
            pl.BlockSpec((None, 2 * width, CMP_HIDDEN), sel),
            pl.BlockSpec((None, 1, CMP_HIDDEN), sel),
            pl.BlockSpec((None, CMP_HIDDEN, HEAD_DIM), sel),
            pl.BlockSpec((None, 1, HEAD_DIM), sel),
            pl.BlockSpec((None, HEAD_DIM, CMP_HIDDEN), sel),
            pl.BlockSpec((None, HEAD_DIM, 1), sel),
        ],
        out_specs=[pl.BlockSpec((None, None, nsub, HEAD_DIM), per),
                   pl.BlockSpec((None, None, HEAD_DIM, nsub), per)],
        out_shape=[jax.ShapeDtypeStruct((2, BG, nsub, HEAD_DIM), MXU_DTYPE),
                   jax.ShapeDtypeStruct((2, BG, HEAD_DIM, nsub), MXU_DTYPE)],
        scratch_shapes=[pltpu.VMEM((nsub, width), MXU_DTYPE), pltpu.VMEM((nsub + SUBLANES, CMP_HIDDEN), F32)],
        compiler_params=pltpu.CompilerParams(
            dimension_semantics=("arbitrary", "arbitrary"), vmem_limit_bytes=VMEM_LIMIT),
    )(a, pe, w1, b1[:, None, :], w2, b2[:, None, :], jnp.swapaxes(w2, 1, 2), b2[:, :, None])


def _rank_accumulate(score, ranks, i0, i1):
    tq = score.shape[1]
    local = lax.broadcasted_iota(jnp.int32, (SUBLANES, tq), 0)
    ranks = list(ranks)
    for i in range(i0, i1):
        row = jnp.broadcast_to(score[i:i + 1, :], (SUBLANES, tq))
        for j in range(len(ranks)):
            slab = score[SUBLANES * j:SUBLANES * (j + 1), :]
            ge = jnp.where(row >= slab, 1.0, 0.0)
            gt = jnp.where(row > slab, 1.0, 0.0)
            if i < SUBLANES * j:
                ahead = ge
            elif i >= SUBLANES * (j + 1):
                ahead = gt
            else:
                ahead = jnp.where(local > i - SUBLANES * j, ge, gt)
            ranks[j] = ranks[j] + ahead
    return ranks


def _nsa_kernel(q_ref, kcmp_ref, vcmpt_ref, ks_ref, vst_ref, kw_ref, vwt_ref, gate_ref, aggt_ref, hg_ref,
                o_ref, sa_ref, sb_ref, acc_ref, *, tq, kc):
    R = Q_PER_KV
    rows = R * tq
    T = ks_ref.shape[0]
    ns = T // SLC_LEN
    ncp = kcmp_ref.shape[0]
    q0 = pl.program_id(2) * tq
    t_row = q0 + lax.broadcasted_iota(jnp.int32, (1, tq), 1)
    q_pad = q_ref[...]
    q64 = q_pad.reshape(rows, LANES)[:, :HEAD_DIM]

    def all_heads(a):
        return jnp.concatenate([a] * R, axis=1)

    def head(a, r, width):
        return a[:, r * width:(r + 1) * width]

    half = tq // 2
    wl = WINDOW + half

    def window_half(h):
        t_half = t_row[:, h * half:(h + 1) * half]
        ws = pl.multiple_of(jnp.maximum(q0 + h * half - WINDOW, 0), half)
        wpos = ws + lax.broadcasted_iota(jnp.int32, (wl, half), 0)
        wbias = jnp.where((wpos <= t_half) & (wpos > t_half - WINDOW), 0.0, NEG)
        qh = q64.reshape(R, tq, HEAD_DIM)[:, h * half:(h + 1) * half, :].reshape(R * half, HEAD_DIM)
        s = _dot_nt(kw_ref[pl.ds(ws, wl), :], qh) + jnp.concatenate([wbias] * R, axis=1)
        p = jnp.exp2(s - jnp.max(s, axis=0, keepdims=True)).astype(MXU_DTYPE)
        ow = _dot(vwt_ref[:, pl.ds(ws, wl)], p)
        return ow[:HEAD_DIM] / ow[HEAD_DIM:]

    cmp_end = lax.broadcasted_iota(jnp.int32, (ncp, tq), 0) * CMP_STRIDE + (CMP_LEN - 1)
    cbias = jnp.where(cmp_end <= t_row, 0.0, NEG)
    s = _dot_nt(kcmp_ref[...], q64) + all_heads(cbias)
    p = jnp.exp2(s - jnp.max(s, axis=0, keepdims=True))
    l = jnp.sum(p, axis=0, keepdims=True)
    p = p * (all_heads(jnp.where(t_row >= CMP_LEN - 1, 1.0, 0.0)) / l)
    o_cmp = _dot(vcmpt_ref[...], p.astype(MXU_DTYPE))

    psum = head(p, 0, tq) + head(p, 1, tq) + head(p, 2, tq) + head(p, 3, tq)
    p_hi = psum.astype(MXU_DTYPE)
    p_lo = (psum - p_hi.astype(F32)).astype(MXU_DTYPE)
    imp = _dot(aggt_ref[...], p_hi) + _dot(aggt_ref[...], p_lo)
    blk = lax.broadcasted_iota(jnp.int32, (ns, tq), 0)
    tb = (q0 + lax.broadcasted_iota(jnp.int32, (ns, tq), 1)) // SLC_LEN
    forced = (blk == 0) | (blk == tb) | (blk == tb - 1)
    score = jnp.where(forced, BIG, imp)
    score = jnp.where(blk <= tb, score, -BIG)
    ranks = [jnp.zeros((SUBLANES, tq), F32) for _ in range(ns // SUBLANES)]
    ranks = _rank_accumulate(score, ranks, 0, ns // 2)
    o_win_halves = [window_half(0)]
    ranks = _rank_accumulate(score, ranks, ns // 2, ns)
    unselected = jnp.where(jnp.concatenate(ranks, axis=0) < min(SLC_TOPK, ns), 0.0, 1.0)
    tag_rows = [jnp.zeros((HEAD_DIM, tq), F32), unselected]
    if ns < LANES - HEAD_DIM:
        tag_rows.append(jnp.zeros((LANES - HEAD_DIM - ns, tq), F32))
    tag = jnp.concatenate(tag_rows, axis=0).T.astype(MXU_DTYPE)
    lane = lax.broadcasted_iota(jnp.int32, (R, tq, LANES), 2)
    q_aug = jnp.where(lane < HEAD_DIM, q_pad, tag[None]).reshape(rows, LANES)

    def scores(c):
        return _dot_nt(ks_ref[pl.ds(pl.multiple_of(c * kc, kc), kc), :], q_aug)

    def causal_bias(c):
        kpos = c * kc + lax.broadcasted_iota(jnp.int32, (kc, tq), 0)
        return all_heads(jnp.where(kpos <= t_row, 0.0, NEG))

    def consume(s, c, m):
        m_new = jnp.maximum(m, jnp.max(s, axis=0, keepdims=True))
        p = jnp.exp2(s - m_new).astype(MXU_DTYPE)
        pv = _dot(vst_ref[:, pl.ds(pl.multiple_of(c * kc, kc), kc)], p)
        acc_ref[...] = jnp.exp2(m - m_new) * acc_ref[...] + pv
        return m_new

    n_pairs = q0 // (2 * kc)
    acc_ref[...] = jnp.zeros((2 * HEAD_DIM, rows), F32)
    sa_ref[...] = scores(0)
    o_win_halves.append(window_half(1))

    def pair_step(j, m):
        sb_ref[...] = scores(2 * j + 1)
        m = consume(sa_ref[...], 2 * j, m)
        sa_ref[...] = scores(2 * j + 2)
        return consume(sb_ref[...], 2 * j + 1, m)

    m = lax.fori_loop(0, n_pairs, pair_step, jnp.full((1, rows), NEG, F32))
    c_last = 2 * n_pairs
    m = consume(sa_ref[...] + causal_bias(c_last), c_last, m)

    @pl.when(q0 + tq > (c_last + 1) * kc)
    def _():
        consume(scores(c_last + 1) + causal_bias(c_last + 1), c_last + 1, m)

    acc = acc_ref[...]
    o_slc = acc[:HEAD_DIM] / acc[HEAD_DIM:]

    g = gate_ref[...]
    outs = []
    for r in range(R):
        c = N_GATES * r
        o_win = jnp.concatenate([head(o_win_halves[0], r, half), head(o_win_halves[1], r, half)], axis=1)
        o = g[c:c + 1] * head(o_cmp, r, tq) + g[c + 1:c + 2] * head(o_slc, r, tq) + g[c + 2:c + 3] * o_win
        ms = jnp.mean(o * o, axis=0, keepdims=True)
        outs.append(o * lax.rsqrt(ms + 1e-6) * hg_ref[r * HEAD_DIM:(r + 1) * HEAD_DIM, :])
    o_ref[...] = jnp.concatenate(outs, axis=0).T.astype(o_ref.dtype)


def _nsa(q, cmp_kv, cmp_kv_t, ks, vst, kw, vwt, gates, agg_t, hg_attn, tq, kc):
    B, _, T, _ = q.shape
    G, R = N_KV_GROUPS, Q_PER_KV
    ncp = cmp_kv.shape[2]
    ns = T // SLC_LEN
    assert T % (2 * kc) == 0 and kc % tq == 0 and WINDOW % (tq // 2) == 0 and T >= WINDOW + tq
    assert ns <= LANES - HEAD_DIM, "block tags must fit in the upper lanes of a key row"
    whole = lambda b, g, i: (b, g, 0, 0)
    hg_lanes = jnp.broadcast_to(hg_attn.reshape(G, R * HEAD_DIM, 1), (G, R * HEAD_DIM, tq))
    return pl.pallas_call(
        functools.partial(_nsa_kernel, tq=tq, kc=kc),
        grid=(B, G, T // tq),
        in_specs=[
            pl.BlockSpec((None, R, tq, LANES), lambda b, g, i: (b, g, i, 0)),
            pl.BlockSpec((None, None, ncp, HEAD_DIM), lambda b, g, i: (0, b * G + g, 0, 0)),
            pl.BlockSpec((None, None, HEAD_DIM, ncp), lambda b, g, i: (1, b * G + g, 0, 0)),
            pl.BlockSpec((None, None, T, LANES), whole),
            pl.BlockSpec((None, None, 2 * HEAD_DIM, T), whole),
            pl.BlockSpec((None, None, T, HEAD_DIM), whole),
            pl.BlockSpec((None, None, 2 * HEAD_DIM, T), whole),
            pl.BlockSpec((None, None, _GATE_ROWS, tq), lambda b, g, i: (b, g, 0, i)),
            pl.BlockSpec((ns, ncp), lambda b, g, i: (0, 0)),
            pl.BlockSpec((None, R * HEAD_DIM, tq), lambda b, g, i: (g, 0, 0)),
        ],
        out_specs=pl.BlockSpec((None, tq, R * HEAD_DIM), lambda b, g, i: (b, i, g)),
        out_shape=jax.ShapeDtypeStruct((B, T, ATTN_WIDTH), MXU_DTYPE),
        scratch_shapes=[pltpu.VMEM((kc, R * tq), F32), pltpu.VMEM((kc, R * tq), F32),
                        pltpu.VMEM((2 * HEAD_DIM, R * tq), F32)],
        compiler_params=pltpu.CompilerParams(
            dimension_semantics=("arbitrary", "arbitrary", "arbitrary"), vmem_limit_bytes=VMEM_LIMIT),
    )(q, cmp_kv, cmp_kv_t, ks, vst, kw, vwt, gates, agg_t, hg_lanes)


_HALO_ROWS = 16


def _mix_ffn_kernel(conv_ref, attn_ref, x_ref, hconv_ref, hattn_ref, hx_ref, wout_ref, g1_ref, b1_ref,
                    wup_ref, cw_ref, cb_ref, wdn_ref, g_ref, b_ref, o_ref, xs_ref, ds_ref, act_ref,
                    *, tiles_per_seq, fc):
    tm = x_ref.shape[0]
    seg = tm // SUBLANES
    per = seg // SUBLANES

    def mixer_out(conv, attn, res):
        mix = _dot(conv, wout_ref[0:CONV_WIDTH, :]) + _dot(attn, wout_ref[CONV_WIDTH:, :])
        return _layer_norm(ALPHA * res + mix, g1_ref[...], b1_ref[...])

    def restride(ref, a, starts, stride):
        nl = a.shape[1] // LANES
        for k in range(nl):
            ref[k] = a[:, k * LANES:(k + 1) * LANES]
        return jnp.concatenate(
            [jnp.concatenate([ref[k, pl.ds(r0, SUBLANES, stride=stride), :] for k in range(nl)], axis=1)
             for r0 in starts], axis=0)

    sub0 = lax.broadcasted_iota(jnp.int32, (SUBLANES, fc), 0) == 0

    x = mixer_out(conv_ref[...], attn_ref[...], x_ref[...])
    first = (pl.program_id(0) % tiles_per_seq) == 0
    halo = mixer_out(hconv_ref[...], hattn_ref[...], hx_ref[...])[_HALO_ROWS - SUBLANES:]
    halo = jnp.where(first, 0.0, halo)
    xb = jnp.concatenate([halo, restride(xs_ref, x, range(seg), seg)], axis=0).astype(MXU_DTYPE)

    def up_conv(col0):
        u_all = _dot(xb, wup_ref[:, col0:col0 + fc])
        u_halo, u = u_all[:SUBLANES], u_all[SUBLANES:]
        last = jnp.where(sub0, pltpu.roll(u_halo, 1, 0), pltpu.roll(u[tm - SUBLANES:], 1, 0))
        last2 = jnp.where(sub0, pltpu.roll(u_halo, 2, 0), pltpu.roll(u[tm - 2 * SUBLANES:tm - SUBLANES], 1, 0))
        u1 = jnp.concatenate([last, u[:tm - SUBLANES]], axis=0)
        u2 = jnp.concatenate([last2, last, u[:tm - 2 * SUBLANES]], axis=0)
        return (cw_ref[2:3, col0:col0 + fc] * u + cw_ref[1:2, col0:col0 + fc] * u1
                + cw_ref[0:1, col0:col0 + fc] * u2 + cb_ref[:, col0:col0 + fc])

    for c in range(D_FF // fc):
        val = up_conv(c * fc)
        gate = up_conv(D_FF + c * fc)
        act_ref[:, c * fc:(c + 1) * fc] = (gate * jax.nn.sigmoid(gate) * val).astype(act_ref.dtype)
    down = restride(ds_ref, _dot(act_ref[...], wdn_ref[...]),
                    [SUBLANES * SUBLANES * (g % per) + g // per for g in range(seg)], SUBLANES)
    o_ref[...] = _layer_norm(ALPHA * x + down, g_ref[...], b_ref[...])


def _mix_ffn(conv_n, attn_n, x, w_out, g1, b1, w_up, conv_w, conv_b, w_down, g, b, tm, tiles_per_seq, fc):
    N, D = x.shape
    row = lambda i: (i, 0)
    const = lambda i: (0, 0)
    halo = lambda i: (jnp.maximum(i * (tm // _HALO_ROWS) - 1, 0), 0)
    return pl.pallas_call(
        functools.partial(_mix_ffn_kernel, tiles_per_seq=tiles_per_seq, fc=fc),
        grid=(N // tm,),
        in_specs=[
            pl.BlockSpec((tm, CONV_WIDTH), row), pl.BlockSpec((tm, ATTN_WIDTH), row), pl.BlockSpec((tm, D), row),
            pl.BlockSpec((_HALO_ROWS, CONV_WIDTH), halo), pl.BlockSpec((_HALO_ROWS, ATTN_WIDTH), halo),
            pl.BlockSpec((_HALO_ROWS, D), halo),
            pl.BlockSpec((D, D), const, pipeline_mode=pl.Buffered(1)),
            pl.BlockSpec((1, D), const), pl.BlockSpec((1, D), const),
            pl.BlockSpec((D, 2 * D_FF), const, pipeline_mode=pl.Buffered(1)),
            pl.BlockSpec((3, 2 * D_FF), const),
            pl.BlockSpec((1, 2 * D_FF), const),
            pl.BlockSpec((D_FF, D), const, pipeline_mode=pl.Buffered(1)),
            pl.BlockSpec((1, D), const), pl.BlockSpec((1, D), const),
        ],
        out_specs=pl.BlockSpec((tm, D), row),
        out_shape=jax.ShapeDtypeStruct((N, D), F32),
        scratch_shapes=[pltpu.VMEM((D // LANES, tm, LANES), F32), pltpu.VMEM((D // LANES, tm, LANES), F32),
                        pltpu.VMEM((tm, D_FF), MXU_DTYPE)],
        compiler_params=pltpu.CompilerParams(
            dimension_semantics=("arbitrary",), vmem_limit_bytes=VMEM_LIMIT),
    )(conv_n, attn_n, x, conv_n, attn_n, x, w_out, g1, b1, w_up, conv_w, conv_b, w_down, g, b)


def _rope_lane_tables(T):
    half = ROPE_DIM // 2
    inv_freq = ROPE_THETA ** (-(np.arange(half, dtype=np.float32) * 2.0 / ROPE_DIM))
    ang = jnp.arange(T, dtype=F32)[:, None] * jnp.asarray(inv_freq, F32)[None]
    cos, sin = jnp.cos(ang), jnp.sin(ang)
    rest = HEAD_DIM - ROPE_DIM
    one, zero = jnp.ones((T, rest), F32), jnp.zeros((T, rest), F32)
    zh = jnp.zeros((T, half), F32)
    cos_t = jnp.concatenate([cos, cos, one], -1)
    rot_t = jnp.concatenate([-sin, sin, zero], -1)
    rep = LANES // HEAD_DIM
    return tuple(jnp.tile(t, (1, rep)) for t in (cos_t, rot_t))


def _group_mean_matrix():
    grp = np.arange(MXU_COLS) // HEAD_DIM
    return jnp.asarray((grp[:, None] == grp[None, :]) / HEAD_DIM, dtype=MXU_DTYPE)


def _slc_aggregation_t(T):
    nc = T // CMP_STRIDE - CMP_LEN // CMP_STRIDE + 1
    ns = T // SLC_LEN
    sc = np.arange(nc)[None, :] * CMP_STRIDE
    ss = np.arange(ns)[:, None] * SLC_LEN
    ov = np.clip(np.minimum(sc + CMP_LEN, ss + SLC_LEN) - np.maximum(sc, ss), 0, None) / CMP_LEN
    ov = np.pad(ov, ((0, 0), (0, T // CMP_STRIDE - nc)))
    return jnp.asarray(ov, dtype=MXU_DTYPE)


def _split_in_weights(w):
    w = w.astype(MXU_DTYPE)
    plain = jnp.concatenate([w[..., :_W_VS], w[..., _W_KW:_W_VW]], axis=-1)
    half = ROPE_DIM // 2
    zeros = jnp.zeros(w.shape[:-1] + (HEAD_DIM - ROPE_DIM,), w.dtype)
    partners = []
    for c0, width in ((_OFF_Q, ATTN_WIDTH), (_OFF_KSW, 2 * KV_WIDTH), (_OFF_KC, KV_WIDTH)):
        for h0 in range(c0, c0 + width, HEAD_DIM):
            partners += [plain[..., h0 + half:h0 + ROPE_DIM], plain[..., h0:h0 + half], zeros]
    w_main = jnp.concatenate([plain] + partners, axis=-1)
    per_group = Q_PER_KV * N_GATES
    pad = jnp.zeros(w.shape[:-1] + (_GATE_ROWS - per_group,), w.dtype)
    cols = [w[..., _W_VS:_W_KW], w[..., _W_VW:_W_GATE]]
    for g in range(N_KV_GROUPS):
        cols += [w[..., _W_GATE + g * per_group:_W_GATE + (g + 1) * per_group], pad]
    return w_main, jnp.swapaxes(jnp.concatenate(cols, axis=-1), -1, -2)


def kernel(x, w_in, short_conv_w, cmp_pe, cmp_w1, cmp_b1, cmp_w2, cmp_b2, head_norm_g, w_out, ln1_g, ln1_b,
           w_up, ffn_conv_w, ffn_conv_b, w_down, ln2_g, ln2_b):
    B, T, D = x.shape
    tm = min(512, T)
    cos_t, rot_t = _rope_lane_tables(T)
    gmean = _group_mean_matrix()
    agg_t = _slc_aggregation_t(T)
    w_main, w_t = _split_in_weights(w_in)
    cmp_w1_b, cmp_w2_b = cmp_w1.astype(MXU_DTYPE), cmp_w2.astype(MXU_DTYPE)
    w_out_b, w_up_b, w_down_b = w_out.astype(MXU_DTYPE), w_up.astype(MXU_DTYPE), w_down.astype(MXU_DTYPE)
    cmp_pe_flat = cmp_pe.reshape(DEPTH, 2, 1, CMP_LEN * HEAD_DIM)
    for i in range(DEPTH):
        hg = head_norm_g[i]
        conv_n, q, cv, ks, kw, vst, vwt, gates = _inproj(
            x, w_main[i], w_t[i], short_conv_w[i], cos_t, rot_t, hg[None, :CONV_WIDTH], gmean, tm)
        cmp_kv, cmp_kv_t = _compress(
            cv.reshape(2, B * N_KV_GROUPS, T, HEAD_DIM), cmp_pe_flat[i], cmp_w1_b[i], cmp_b1[i], cmp_w2_b[i],
            cmp_b2[i])
        attn_n = _nsa(q, cmp_kv, cmp_kv_t, ks, vst, kw, vwt, gates, agg_t, hg[CONV_WIDTH:], tq=256, kc=512)
        x = _mix_ffn(conv_n.reshape(B * T, CONV_WIDTH), attn_n.reshape(B * T, ATTN_WIDTH), x.reshape(B * T, D),
                     w_out_b[i], ln1_g[i][None], ln1_b[i][None], w_up_b[i], ffn_conv_w[i], ffn_conv_b[i][None],
                     w_down_b[i], ln2_g[i][None], ln2_b[i][None], tm, T // tm, 256).reshape(B, T, D)
    return x
```

```python
import functools

import numpy as np
import jax
import jax.numpy as jnp
from jax import lax
from jax.experimental import pallas as pl
from jax.experimental.pallas import tpu as pltpu

D_MODEL = 1024
DEPTH = 2
HEAD_DIM = 64
CONV_WIDTH = 512
ATTN_WIDTH = 512
N_Q_HEADS = 8
N_KV_GROUPS = 2
Q_PER_KV = 4
KV_WIDTH = N_KV_GROUPS * HEAD_DIM
N_GATES = 3
CMP_LEN = 32
CMP_STRIDE = 16
CMP_HIDDEN = 256
SLC_LEN = 64
SLC_TOPK = 16
WINDOW = 512
ROPE_THETA = 500000.0
ROPE_DIM = 16
D_FF = 2816
ALPHA = (2.0 * DEPTH) ** 0.25
NEG = -1e30
BIG = 1e9

F32 = jnp.float32
MXU_DTYPE = jnp.bfloat16
LANES = 128
SUBLANES = 8
MXU_COLS = 256
VMEM_LIMIT = 56 * 1024 * 1024

_W_VS, _W_KW, _W_VW, _W_GATE, _W_END = 2432, 2560, 2688, 2816, 2840
_OFF_B, _OFF_C, _OFF_H, _OFF_Q, _OFF_KC, _OFF_KSW = 0, 512, 1024, 1536, 2048, 2304
_OFF_PQ, _OFF_PKSW, _OFF_PKC = 2560, 3072, 3328
_MAIN_COLS = 3456
LOG2E = 1.4426950408889634
_GATE_ROWS = 16
_ROW_VS, _ROW_VW, _ROW_GATE = 0, KV_WIDTH, 2 * KV_WIDTH
_T_ROWS = 2 * KV_WIDTH + N_KV_GROUPS * _GATE_ROWS

_NT = (((1,), (1,)), ((), ()))


def _dot(a, b):
    return jnp.dot(a, b, preferred_element_type=F32)


def _dot_nt(a, b):
    return lax.dot_general(a, b, _NT, preferred_element_type=F32)


def _layer_norm(y, g, b):
    mu = jnp.mean(y, axis=-1, keepdims=True)
    d = y - mu
    var = jnp.mean(d * d, axis=-1, keepdims=True)
    return d * lax.rsqrt(var + 1e-5) * g + b


def _inproj_kernel(x_ref, w_ref, wt_ref, cw_ref, cos_ref, rot_ref, hg_ref, gmean_ref,
                   conv_ref, q_ref, cv_ref, ks_ref, kw_ref, vst_ref, vwt_ref, gate_ref, zbuf):
    tm = x_ref.shape[0]
    xb = x_ref[...].astype(MXU_DTYPE)

    def mm(c0, width):
        return _dot(xb, w_ref[:, c0:c0 + width])

    @pl.when(pl.program_id(1) == 0)
    def _():
        zbuf[0:SUBLANES, :] = jnp.zeros((SUBLANES, CONV_WIDTH), F32)

    def conv_block(c0):
        cs = slice(c0, c0 + MXU_COLS)
        z = mm(_OFF_C + c0, MXU_COLS) * mm(_OFF_H + c0, MXU_COLS)
        zbuf[SUBLANES:SUBLANES + tm, cs] = z
        conv = (cw_ref[2:3, cs] * z + cw_ref[1:2, cs] * zbuf[SUBLANES - 1:SUBLANES - 1 + tm, cs]
                + cw_ref[0:1, cs] * zbuf[SUBLANES - 2:SUBLANES - 2 + tm, cs])
        zbuf[0:SUBLANES, cs] = zbuf[tm:tm + SUBLANES, cs]
        y = mm(_OFF_B + c0, MXU_COLS) * conv
        y2 = y * y
        y2_hi = y2.astype(MXU_DTYPE)
        y2_lo = (y2 - y2_hi.astype(F32)).astype(MXU_DTYPE)
        ms = _dot(y2_hi, gmean_ref[...]) + _dot(y2_lo, gmean_ref[...])
        conv_ref[:, cs] = (y * lax.rsqrt(ms + 1e-6) * hg_ref[:, cs]).astype(conv_ref.dtype)

    def rope(v, partner):
        return v * cos_ref[...] + partner * rot_ref[...]

    def split_heads(v, ref, lead):
        for g in range(N_KV_GROUPS):
            ref[lead + (g,)] = v[:, g * HEAD_DIM:(g + 1) * HEAD_DIM].astype(ref.dtype)

    lane = lax.broadcasted_iota(jnp.int32, (tm, LANES), 1)
    low = lane < HEAD_DIM

    def head_pair(v):
        return v, pltpu.roll(v, HEAD_DIM, 1)

    def q_block(c):
        r4 = mm(_OFF_Q + c * MXU_COLS, MXU_COLS)
        p4 = mm(_OFF_PQ + c * MXU_COLS, MXU_COLS)
        for i in range(MXU_COLS // LANES):
            sl = slice(i * LANES, (i + 1) * LANES)
            r = rope(r4[:, sl], p4[:, sl]) * (HEAD_DIM ** -0.5 * LOG2E)
            for j, h in enumerate(head_pair(r)):
                q_ref[4 * c + 2 * i + j] = jnp.where(low, h, 0.0).astype(q_ref.dtype)

    conv_block(0)
    q_block(0)
    conv_block(MXU_COLS)
    q_block(1)
    kvc = mm(_OFF_KC, 2 * KV_WIDTH)
    split_heads(rope(kvc[:, :KV_WIDTH], mm(_OFF_PKC, KV_WIDTH)), cv_ref, (0,))
    split_heads(kvc[:, KV_WIDTH:], cv_ref, (1,))
    ksw = mm(_OFF_KSW, 2 * KV_WIDTH)
    pksw = mm(_OFF_PKSW, 2 * KV_WIDTH)
    pos = pl.program_id(1) * tm + lax.broadcasted_iota(jnp.int32, (tm, LANES), 0)
    block_tag = jnp.where(pos // SLC_LEN == lane - HEAD_DIM, NEG, 0.0)
    for g, h in enumerate(head_pair(rope(ksw[:, :KV_WIDTH], pksw[:, :KV_WIDTH]))):
        ks_ref[g] = jnp.where(low, h, block_tag).astype(ks_ref.dtype)
    split_heads(rope(ksw[:, KV_WIDTH:], pksw[:, KV_WIDTH:]), kw_ref, ())

    vt = _dot_nt(wt_ref[...], xb)
    ones = jnp.ones((HEAD_DIM, tm), vst_ref.dtype)
    for g in range(N_KV_GROUPS):
        for ref, row0 in ((vst_ref, _ROW_VS), (vwt_ref, _ROW_VW)):
            ref[g, 0:HEAD_DIM, :] = vt[row0 + g * HEAD_DIM:row0 + (g + 1) * HEAD_DIM].astype(ref.dtype)
            ref[g, HEAD_DIM:2 * HEAD_DIM, :] = ones
        gate_ref[g] = jax.nn.sigmoid(vt[_ROW_GATE + g * _GATE_ROWS:_ROW_GATE + (g + 1) * _GATE_ROWS])


def _inproj(x, w_main, w_t, conv_w, cos_t, rot_t, hg_conv, gmean, tm):
    B, T, D = x.shape
    G = N_KV_GROUPS
    kv_shape = jax.ShapeDtypeStruct((B, G, T, HEAD_DIM), MXU_DTYPE)
    kv_spec = pl.BlockSpec((None, G, tm, HEAD_DIM), lambda b, i: (b, 0, i, 0))
    vt_shape = jax.ShapeDtypeStruct((B, G, 2 * HEAD_DIM, T), MXU_DTYPE)
    vt_spec = pl.BlockSpec((None, G, 2 * HEAD_DIM, tm), lambda b, i: (b, 0, 0, i))
    tab_spec = pl.BlockSpec((tm, LANES), lambda b, i: (i, 0))
    const2 = lambda b, i: (0, 0)
    return pl.pallas_call(
        _inproj_kernel,
        grid=(B, T // tm),
        in_specs=[
            pl.BlockSpec((None, tm, D), lambda b, i: (b, i, 0)),
            pl.BlockSpec((D, _MAIN_COLS), const2),
            pl.BlockSpec((_T_ROWS, D), const2),
            pl.BlockSpec((3, CONV_WIDTH), const2),
            tab_spec, tab_spec,
            pl.BlockSpec((1, CONV_WIDTH), const2),
            pl.BlockSpec((MXU_COLS, MXU_COLS), const2),
        ],
        out_specs=[
            pl.BlockSpec((None, tm, CONV_WIDTH), lambda b, i: (b, i, 0)),
            pl.BlockSpec((None, N_Q_HEADS, tm, LANES), lambda b, i: (b, 0, i, 0)),
            pl.BlockSpec((2, None, G, tm, HEAD_DIM), lambda b, i: (0, b, 0, i, 0)),
            pl.BlockSpec((None, G, tm, LANES), lambda b, i: (b, 0, i, 0)),
            kv_spec, vt_spec, vt_spec,
            pl.BlockSpec((None, G, _GATE_ROWS, tm), lambda b, i: (b, 0, 0, i)),
        ],
        out_shape=[
            jax.ShapeDtypeStruct((B, T, CONV_WIDTH), MXU_DTYPE),
            jax.ShapeDtypeStruct((B, N_Q_HEADS, T, LANES), MXU_DTYPE),
            jax.ShapeDtypeStruct((2, B, G, T, HEAD_DIM), F32),
            jax.ShapeDtypeStruct((B, G, T, LANES), MXU_DTYPE),
            kv_shape, vt_shape, vt_shape,
            jax.ShapeDtypeStruct((B, G, _GATE_ROWS, T), F32),
        ],
        scratch_shapes=[pltpu.VMEM((tm + 2 * SUBLANES, CONV_WIDTH), F32)],
        compiler_params=pltpu.CompilerParams(
            dimension_semantics=("arbitrary", "arbitrary"), vmem_limit_bytes=VMEM_LIMIT),
    )(x, w_main, w_t, conv_w, cos_t, rot_t, hg_conv, gmean)


def _compress_kernel(a_ref, pe_ref, w1_ref, b1_ref, w2_ref, b2_ref, w2t_ref, b2t_ref, out_ref, outt_ref,
                     sub_ref, bbuf):
    nsub = sub_ref.shape[0]
    half = CMP_STRIDE * HEAD_DIM
    for j in range(CMP_STRIDE):
        sub_ref[:, j * HEAD_DIM:(j + 1) * HEAD_DIM] = (
            a_ref[pl.ds(j, nsub, stride=CMP_STRIDE), :].astype(sub_ref.dtype))
    sub = sub_ref[...]
    top = _dot(sub, w1_ref[0:half, :])
    bbuf[0:nsub, :] = _dot(sub, w1_ref[half:2 * half, :])
    bbuf[nsub:nsub + SUBLANES, :] = jnp.zeros((SUBLANES, CMP_HIDDEN), F32)
    pe_rows = jnp.broadcast_to(pe_ref[...], (SUBLANES, 2 * half)).astype(MXU_DTYPE)
    const = _dot(pe_rows, w1_ref[...])[0:1, :] + b1_ref[...]
    h = top + bbuf[1:nsub + 1, :] + const
    act = jax.nn.gelu(h).astype(MXU_DTYPE)
    out_ref[...] = (_dot(act, w2_ref[...]) + b2_ref[...]).astype(out_ref.dtype)
    outt_ref[...] = (_dot_nt(w2t_ref[...], act) + b2t_ref[...]).astype(outt_ref.dtype)


def _compress(a, pe, w1, b1, w2, b2):
    _, BG, T, _ = a.shape
    nsub = T // CMP_STRIDE
    width = CMP_STRIDE * HEAD_DIM
    sel = lambda k, n: (k, 0, 0)
    per = lambda k, n: (k, n, 0, 0)
    return pl.pallas_call(
        _compress_kernel,
        grid=(2, BG),
        in_specs=[
            pl.BlockSpec((None, None, T, HEAD_DIM), per),
            pl.BlockSpec((None, 1, 2 * width), sel),
            pl.BlockSpec((None, 2 * width, CMP_HIDDEN), sel),
            pl.BlockSpec((None, 1, CMP_HIDDEN), sel),
            pl.BlockSpec((None, CMP_HIDDEN, HEAD_DIM), sel),
            pl.BlockSpec((None, 1, HEAD_DIM), sel),
            pl.BlockSpec((None, HEAD_DIM, CMP_HIDDEN), sel),
            pl.BlockSpec((None, HEAD_DIM, 1), sel),
        ],
        out_specs=[pl.BlockSpec((None, None, nsub, HEAD_DIM), per),
                   pl.BlockSpec((None, None, HEAD_DIM, nsub), per)],
        out_shape=[jax.ShapeDtypeStruct((2, BG, nsub, HEAD_DIM), MXU_DTYPE),
                   jax.ShapeDtypeStruct((2, BG, HEAD_DIM, nsub), MXU_DTYPE)],
        scratch_shapes=[pltpu.VMEM((nsub, width), MXU_DTYPE), pltpu.VMEM((nsub + SUBLANES, CMP_HIDDEN), F32)],
        compiler_params=pltpu.CompilerParams(
            dimension_semantics=("arbitrary", "arbitrary"), vmem_limit_bytes=VMEM_LIMIT),
    )(a, pe, w1, b1[:, None, :], w2, b2[:, None, :], jnp.swapaxes(w2, 1, 2), b2[:, :, None])


def _rank_accumulate(score, ranks, i0, i1):
    tq = score.shape[1]
    local = lax.broadcasted_iota(jnp.int32, (SUBLANES, tq), 0)
    ranks = list(ranks)
    for i in range(i0, i1):
        row = jnp.broadcast_to(score[i:i + 1, :], (SUBLANES, tq))
        for j in range(len(ranks)):
            slab = score[SUBLANES * j:SUBLANES * (j + 1), :]
            ge = jnp.where(row >= slab, 1.0, 0.0)
            gt = jnp.where(row > slab, 1.0, 0.0)
            if i < SUBLANES * j:
                ahead = ge
            elif i >= SUBLANES * (j + 1):
                ahead = gt
            else:
                ahead = jnp.where(local > i - SUBLANES * j, ge, gt)
            ranks[j] = ranks[j] + ahead
    return ranks


def _nsa_kernel(q_ref, kcmp_ref, vcmpt_ref, ks_ref, vst_ref, kw_ref, vwt_ref, gate_ref, aggt_ref, hg_ref,
                o_ref, sa_ref, sb_ref, acc_ref, *, tq, kc, wq):
    R = Q_PER_KV
    rows = R * tq
    T = ks_ref.shape[0]
    ns = T // SLC_LEN
    ncp = kcmp_ref.shape[0]
    q0 = pl.program_id(2) * tq
    t_row = q0 + lax.broadcasted_iota(jnp.int32, (1, tq), 1)
    q_pad = q_ref[...]
    q64 = q_pad.reshape(rows, LANES)[:, :HEAD_DIM]

    def all_heads(a):
        return jnp.concatenate([a] * R, axis=1)

    def head(a, r, width):
        return a[:, r * width:(r + 1) * width]

    n_win = tq // wq
    wl = WINDOW + wq

    def window_piece(h):
        t_piece = t_row[:, h * wq:(h + 1) * wq]
        ws = pl.multiple_of(jnp.maximum(q0 + h * wq - WINDOW, 0), wq)
        wpos = ws + lax.broadcasted_iota(jnp.int32, (wl, wq), 0)
        wbias = jnp.where((wpos <= t_piece) & (wpos > t_piece - WINDOW), 0.0, NEG)
        qh = q64.reshape(R, tq, HEAD_DIM)[:, h * wq:(h + 1) * wq, :].reshape(R * wq, HEAD_DIM)
        s = _dot_nt(kw_ref[pl.ds(ws, wl), :], qh) + jnp.concatenate([wbias] * R, axis=1)
        p = jnp.exp2(s - jnp.max(s, axis=0, keepdims=True)).astype(MXU_DTYPE)
        ow = _dot(vwt_ref[:, pl.ds(ws, wl)], p)
        return ow[:HEAD_DIM] / ow[HEAD_DIM:]

    cmp_end = lax.broadcasted_iota(jnp.int32, (ncp, tq), 0) * CMP_STRIDE + (CMP_LEN - 1)
    cbias = jnp.where(cmp_end <= t_row, 0.0, NEG)
    s = _dot_nt(kcmp_ref[...], q64) + all_heads(cbias)
    p = jnp.exp2(s - jnp.max(s, axis=0, keepdims=True))
    l = jnp.sum(p, axis=0, keepdims=True)
    p = p * (all_heads(jnp.where(t_row >= CMP_LEN - 1, 1.0, 0.0)) / l)
    o_cmp = _dot(vcmpt_ref[...], p.astype(MXU_DTYPE))

    psum = head(p, 0, tq) + head(p, 1, tq) + head(p, 2, tq) + head(p, 3, tq)
    p_hi = psum.astype(MXU_DTYPE)
    p_lo = (psum - p_hi.astype(F32)).astype(MXU_DTYPE)
    imp = _dot(aggt_ref[...], p_hi) + _dot(aggt_ref[...], p_lo)
    blk = lax.broadcasted_iota(jnp.int32, (ns, tq), 0)
    tb = (q0 + lax.broadcasted_iota(jnp.int32, (ns, tq), 1)) // SLC_LEN
    forced = (blk == 0) | (blk == tb) | (blk == tb - 1)
    score = jnp.where(forced, BIG, imp)
    score = jnp.where(blk <= tb, score, -BIG)
    ranks = [jnp.zeros((SUBLANES, tq), F32) for _ in range(ns // SUBLANES)]
    ranks = _rank_accumulate(score, ranks, 0, ns // 2)
    o_win_pieces = [window_piece(h) for h in range(n_win // 2)]
    ranks = _rank_accumulate(score, ranks, ns // 2, ns)
    unselected = jnp.where(jnp.concatenate(ranks, axis=0) < min(SLC_TOPK, ns), 0.0, 1.0)
    tag_rows = [jnp.zeros((HEAD_DIM, tq), F32), unselected]
    if ns < LANES - HEAD_DIM:
        tag_rows.append(jnp.zeros((LANES - HEAD_DIM - ns, tq), F32))
    tag = jnp.concatenate(tag_rows, axis=0).T.astype(MXU_DTYPE)
    lane = lax.broadcasted_iota(jnp.int32, (R, tq, LANES), 2)
    q_aug = jnp.where(lane < HEAD_DIM, q_pad, tag[None]).reshape(rows, LANES)

    def scores(c):
        return _dot_nt(ks_ref[pl.ds(pl.multiple_of(c * kc, kc), kc), :], q_aug)

    def causal_bias(c):
        kpos = c * kc + lax.broadcasted_iota(jnp.int32, (kc, tq), 0)
        return all_heads(jnp.where(kpos <= t_row, 0.0, NEG))

    def consume(s, c, m):
        m_new = jnp.maximum(m, jnp.max(s, axis=0, keepdims=True))
        p = jnp.exp2(s - m_new).astype(MXU_DTYPE)
        pv = _dot(vst_ref[:, pl.ds(pl.multiple_of(c * kc, kc), kc)], p)
        acc_ref[...] = jnp.exp2(m - m_new) * acc_ref[...] + pv
        return m_new

    n_pairs = q0 // (2 * kc)
    acc_ref[...] = jnp.zeros((2 * HEAD_DIM, rows), F32)
    sa_ref[...] = scores(0)
    o_win_pieces += [window_piece(h) for h in range(n_win // 2, n_win)]

    def pair_step(j, m):
        sb_ref[...] = scores(2 * j + 1)
        m = consume(sa_ref[...], 2 * j, m)
        sa_ref[...] = scores(2 * j + 2)
        return consume(sb_ref[...], 2 * j + 1, m)

    m = lax.fori_loop(0, n_pairs, pair_step, jnp.full((1, rows), NEG, F32))
    c_last = 2 * n_pairs
    m = consume(sa_ref[...] + causal_bias(c_last), c_last, m)

    @pl.when(q0 + tq > (c_last + 1) * kc)
    def _():
        consume(scores(c_last + 1) + causal_bias(c_last + 1), c_last + 1, m)

    acc = acc_ref[...]
    o_slc = acc[:HEAD_DIM] / acc[HEAD_DIM:]

    g = gate_ref[...]
    outs = []
    for r in range(R):
        c = N_GATES * r
        o_win = jnp.concatenate([head(piece, r, wq) for piece in o_win_pieces], axis=1)
        o = g[c:c + 1] * head(o_cmp, r, tq) + g[c + 1:c + 2] * head(o_slc, r, tq) + g[c + 2:c + 3] * o_win
        ms = jnp.mean(o * o, axis=0, keepdims=True)
        outs.append(o * lax.rsqrt(ms + 1e-6) * hg_ref[r * HEAD_DIM:(r + 1) * HEAD_DIM, :])
    o_ref[...] = jnp.concatenate(outs, axis=0).T.astype(o_ref.dtype)


def _nsa(q, cmp_kv, cmp_kv_t, ks, vst, kw, vwt, gates, agg_t, hg_attn, tq, kc, wq):
    B, _, T, _ = q.shape
    G, R = N_KV_GROUPS, Q_PER_KV
    ncp = cmp_kv.shape[2]
    ns = T // SLC_LEN
    assert T % (2 * kc) == 0 and kc % tq == 0 and tq % (2 * wq) == 0 and WINDOW % wq == 0 and T >= WINDOW + tq
    assert ns <= LANES - HEAD_DIM, "block tags must fit in the upper lanes of a key row"
    whole = lambda b, g, i: (b, g, 0, 0)
    hg_lanes = jnp.broadcast_to(hg_attn.reshape(G, R * HEAD_DIM, 1), (G, R * HEAD_DIM, tq))
    return pl.pallas_call(
        functools.partial(_nsa_kernel, tq=tq, kc=kc, wq=wq),
        grid=(B, G, T // tq),
        in_specs=[
            pl.BlockSpec((None, R, tq, LANES), lambda b, g, i: (b, g, i, 0)),
            pl.BlockSpec((None, None, ncp, HEAD_DIM), lambda b, g, i: (0, b * G + g, 0, 0)),
            pl.BlockSpec((None, None, HEAD_DIM, ncp), lambda b, g, i: (1, b * G + g, 0, 0)),
            pl.BlockSpec((None, None, T, LANES), whole),
            pl.BlockSpec((None, None, 2 * HEAD_DIM, T), whole),
            pl.BlockSpec((None, None, T, HEAD_DIM), whole),
            pl.BlockSpec((None, None, 2 * HEAD_DIM, T), whole),
            pl.BlockSpec((None, None, _GATE_ROWS, tq), lambda b, g, i: (b, g, 0, i)),
            pl.BlockSpec((ns, ncp), lambda b, g, i: (0, 0)),
            pl.BlockSpec((None, R * HEAD_DIM, tq), lambda b, g, i: (g, 0, 0)),
        ],
        out_specs=pl.BlockSpec((None, tq, R * HEAD_DIM), lambda b, g, i: (b, i, g)),
        out_shape=jax.ShapeDtypeStruct((B, T, ATTN_WIDTH), MXU_DTYPE),
        scratch_shapes=[pltpu.VMEM((kc, R * tq), F32), pltpu.VMEM((kc, R * tq), F32),
                        pltpu.VMEM((2 * HEAD_DIM, R * tq), F32)],
        compiler_params=pltpu.CompilerParams(
            dimension_semantics=("arbitrary", "arbitrary", "arbitrary"), vmem_limit_bytes=VMEM_LIMIT),
    )(q, cmp_kv, cmp_kv_t, ks, vst, kw, vwt, gates, agg_t, hg_lanes)


_HALO_ROWS = 16


def _mix_ffn_kernel(conv_ref, attn_ref, x_ref, hconv_ref, hattn_ref, hx_ref, wout_ref, g1_ref, b1_ref,
                    wup_ref, cw_ref, cb_ref, wdn_ref, g_ref, b_ref, o_ref, ubuf, act_ref,
                    *, tiles_per_seq, fc):
    tm = x_ref.shape[0]

    def mixer_out(conv, attn, res):
        mix = _dot(conv, wout_ref[0:CONV_WIDTH, :]) + _dot(attn, wout_ref[CONV_WIDTH:, :])
        return _layer_norm(ALPHA * res + mix, g1_ref[...], b1_ref[...])

    x = mixer_out(conv_ref[...], attn_ref[...], x_ref[...])
    first = (pl.program_id(0) % tiles_per_seq) == 0
    halo = mixer_out(hconv_ref[...], hattn_ref[...], hx_ref[...])[_HALO_ROWS - SUBLANES:]
    halo = jnp.where(first, 0.0, halo)
    xb = jnp.concatenate([halo, x], axis=0).astype(MXU_DTYPE)

    def up_conv(col0, buf):
        buf[...] = _dot(xb, wup_ref[:, col0:col0 + fc])
        return (cw_ref[2:3, col0:col0 + fc] * buf[SUBLANES:SUBLANES + tm, :]
                + cw_ref[1:2, col0:col0 + fc] * buf[SUBLANES - 1:SUBLANES - 1 + tm, :]
                + cw_ref[0:1, col0:col0 + fc] * buf[SUBLANES - 2:SUBLANES - 2 + tm, :]
                + cb_ref[:, col0:col0 + fc])

    for c in range(D_FF // fc):
        val = up_conv(c * fc, ubuf.at[c % 2, 0])
        gate = up_conv(D_FF + c * fc, ubuf.at[c % 2, 1])
        act_ref[:, c * fc:(c + 1) * fc] = (gate * jax.nn.sigmoid(gate) * val).astype(act_ref.dtype)
    o_ref[...] = _layer_norm(ALPHA * x + _dot(act_ref[...], wdn_ref[...]), g_ref[...], b_ref[...])


def _mix_ffn(conv_n, attn_n, x, w_out, g1, b1, w_up, conv_w, conv_b, w_down, g, b, tm, tiles_per_seq, fc):
    N, D = x.shape
    row = lambda i: (i, 0)
    const = lambda i: (0, 0)
    halo = lambda i: (jnp.maximum(i * (tm // _HALO_ROWS) - 1, 0), 0)
    return pl.pallas_call(
        functools.partial(_mix_ffn_kernel, tiles_per_seq=tiles_per_seq, fc=fc),
        grid=(N // tm,),
        in_specs=[
            pl.BlockSpec((tm, CONV_WIDTH), row), pl.BlockSpec((tm, ATTN_WIDTH), row), pl.BlockSpec((tm, D), row),
            pl.BlockSpec((_HALO_ROWS, CONV_WIDTH), halo), pl.BlockSpec((_HALO_ROWS, ATTN_WIDTH), halo),
            pl.BlockSpec((_HALO_ROWS, D), halo),
            pl.BlockSpec((D, D), const, pipeline_mode=pl.Buffered(1)),
            pl.BlockSpec((1, D), const), pl.BlockSpec((1, D), const),
            pl.BlockSpec((D, 2 * D_FF), const, pipeline_mode=pl.Buffered(1)),
            pl.BlockSpec((3, 2 * D_FF), const),
            pl.BlockSpec((1, 2 * D_FF), const),
            pl.BlockSpec((D_FF, D), const, pipeline_mode=pl.Buffered(1)),
            pl.BlockSpec((1, D), const), pl.BlockSpec((1, D), const),
        ],
        out_specs=pl.BlockSpec((tm, D), row),
        out_shape=jax.ShapeDtypeStruct((N, D), F32),
        scratch_shapes=[pltpu.VMEM((2, 2, tm + SUBLANES, fc), F32), pltpu.VMEM((tm, D_FF), MXU_DTYPE)],
        compiler_params=pltpu.CompilerParams(
            dimension_semantics=("arbitrary",), vmem_limit_bytes=VMEM_LIMIT),
    )(conv_n, attn_n, x, conv_n, attn_n, x, w_out, g1, b1, w_up, conv_w, conv_b, w_down, g, b)


def _rope_tables_kernel(freq_hi_ref, freq_lo_ref, sign_ref, cos_ref, rot_ref):
    tr = cos_ref.shape[0]
    pos = (pl.program_id(0) * tr + lax.broadcasted_iota(jnp.int32, (tr, LANES), 0)).astype(F32)
    ang = pos * freq_hi_ref[...] + pos * freq_lo_ref[...]
    cos_ref[...] = jnp.cos(ang)
    rot_ref[...] = jnp.sin(ang) * sign_ref[...]


def _rope_lane_tables(T, tr):
    half = ROPE_DIM // 2
    inv_freq = ROPE_THETA ** (-(np.arange(half, dtype=np.float64) * 2.0 / ROPE_DIM))
    lane = np.arange(LANES) % HEAD_DIM
    freq = np.where(lane < ROPE_DIM, inv_freq[lane % half], 0.0)
    freq_hi = freq.astype(np.float32)
    freq_lo = (freq - freq_hi.astype(np.float64)).astype(np.float32)
    sign = np.where(lane < half, -1.0, np.where(lane < ROPE_DIM, 1.0, 0.0)).astype(np.float32)
    lanes_spec = pl.BlockSpec((1, LANES), lambda i: (0, 0))
    rows_spec = pl.BlockSpec((tr, LANES), lambda i: (i, 0))
    return pl.pallas_call(
        _rope_tables_kernel,
        grid=(T // tr,),
        in_specs=[lanes_spec, lanes_spec, lanes_spec],
        out_specs=[rows_spec, rows_spec],
        out_shape=[jax.ShapeDtypeStruct((T, LANES), F32)] * 2,
    )(jnp.asarray(freq_hi)[None], jnp.asarray(freq_lo)[None], jnp.asarray(sign)[None])


def _group_mean_matrix():
    grp = np.arange(MXU_COLS) // HEAD_DIM
    return jnp.asarray((grp[:, None] == grp[None, :]) / HEAD_DIM, dtype=MXU_DTYPE)


def _slc_aggregation_t(T):
    nc = T // CMP_STRIDE - CMP_LEN // CMP_STRIDE + 1
    ns = T // SLC_LEN
    sc = np.arange(nc)[None, :] * CMP_STRIDE
    ss = np.arange(ns)[:, None] * SLC_LEN
    ov = np.clip(np.minimum(sc + CMP_LEN, ss + SLC_LEN) - np.maximum(sc, ss), 0, None) / CMP_LEN
    ov = np.pad(ov, ((0, 0), (0, T // CMP_STRIDE - nc)))
    return jnp.asarray(ov, dtype=MXU_DTYPE)


def _split_in_weights(w):
    w = w.astype(MXU_DTYPE)
    plain = jnp.concatenate([w[..., :_W_VS], w[..., _W_KW:_W_VW]], axis=-1)
    half = ROPE_DIM // 2
    zeros = jnp.zeros(w.shape[:-1] + (HEAD_DIM - ROPE_DIM,), w.dtype)
    partners = []
    for c0, width in ((_OFF_Q, ATTN_WIDTH), (_OFF_KSW, 2 * KV_WIDTH), (_OFF_KC, KV_WIDTH)):
        for h0 in range(c0, c0 + width, HEAD_DIM):
            partners += [plain[..., h0 + half:h0 + ROPE_DIM], plain[..., h0:h0 + half], zeros]
    w_main = jnp.concatenate([plain] + partners, axis=-1)
    per_group = Q_PER_KV * N_GATES
    pad = jnp.zeros(w.shape[:-1] + (_GATE_ROWS - per_group,), w.dtype)
    cols = [w[..., _W_VS:_W_KW], w[..., _W_VW:_W_GATE]]
    for g in range(N_KV_GROUPS):
        cols += [w[..., _W_GATE + g * per_group:_W_GATE + (g + 1) * per_group], pad]
    return w_main, jnp.swapaxes(jnp.concatenate(cols, axis=-1), -1, -2)


def kernel(x, w_in, short_conv_w, cmp_pe, cmp_w1, cmp_b1, cmp_w2, cmp_b2, head_norm_g, w_out, ln1_g, ln1_b,
           w_up, ffn_conv_w, ffn_conv_b, w_down, ln2_g, ln2_b):
    B, T, D = x.shape
    tm = min(512, T)
    cos_t, rot_t = _rope_lane_tables(T, tm)
    gmean = _group_mean_matrix()
    agg_t = _slc_aggregation_t(T)
    w_main, w_t = _split_in_weights(w_in)
    cmp_w1_b, cmp_w2_b = cmp_w1.astype(MXU_DTYPE), cmp_w2.astype(MXU_DTYPE)
    w_out_b, w_up_b, w_down_b = w_out.astype(MXU_DTYPE), w_up.astype(MXU_DTYPE), w_down.astype(MXU_DTYPE)
    cmp_pe_flat = cmp_pe.reshape(DEPTH, 2, 1, CMP_LEN * HEAD_DIM)
    for i in range(DEPTH):
        hg = head_norm_g[i]
        conv_n, q, cv, ks, kw, vst, vwt, gates = _inproj(
            x, w_main[i], w_t[i], short_conv_w[i], cos_t, rot_t, hg[None, :CONV_WIDTH], gmean, tm)
        cmp_kv, cmp_kv_t = _compress(
            cv.reshape(2, B * N_KV_GROUPS, T, HEAD_DIM), cmp_pe_flat[i], cmp_w1_b[i], cmp_b1[i], cmp_w2_b[i],
            cmp_b2[i])
        attn_n = _nsa(q, cmp_kv, cmp_kv_t, ks, vst, kw, vwt, gates, agg_t, hg[CONV_WIDTH:], tq=512, kc=512,
                      wq=128)
        x = _mix_ffn(conv_n.reshape(B * T, CONV_WIDTH), attn_n.reshape(B * T, ATTN_WIDTH), x.reshape(B * T, D),
                     w_out_b[i], ln1_g[i][None], ln1_b[i][None], w_up_b[i], ffn_conv_w[i], ffn_conv_b[i][None],
                     w_down_b[i], ln2_g[i][None], ln2_b[i][None], tm, T // tm, 256).reshape(B, T, D)
    return x
```

```python
import functools

import numpy as np
import jax
import jax.numpy as jnp
from jax import lax
from jax.experimental import pallas as pl
from jax.experimental.pallas import tpu as pltpu

D_MODEL = 1024
DEPTH = 2
HEAD_DIM = 64
CONV_WIDTH = 512
ATTN_WIDTH = 512
N_Q_HEADS = 8
N_KV_GROUPS = 2
Q_PER_KV = 4
KV_WIDTH = N_KV_GROUPS * HEAD_DIM
N_GATES = 3
CMP_LEN = 32
CMP_STRIDE = 16
CMP_HIDDEN = 256
SLC_LEN = 64
SLC_TOPK = 16
WINDOW = 512
ROPE_THETA = 500000.0
ROPE_DIM = 16
D_FF = 2816
ALPHA = (2.0 * DEPTH) ** 0.25
NEG = -1e30
BIG = 1e9

F32 = jnp.float32
MXU_DTYPE = jnp.bfloat16
LANES = 128
SUBLANES = 8
MXU_COLS = 256
VMEM_LIMIT = 56 * 1024 * 1024

_W_VS, _W_KW, _W_VW, _W_GATE, _W_END = 2432, 2560, 2688, 2816, 2840
_OFF_B, _OFF_C, _OFF_H, _OFF_Q, _OFF_KC, _OFF_KSW = 0, 512, 1024, 1536, 2048, 2304
_OFF_PQ, _OFF_PKSW, _OFF_PKC = 2560, 3072, 3328
_MAIN_COLS = 3456
LOG2E = 1.4426950408889634
_GATE_ROWS = 16
_ROW_VS, _ROW_VW, _ROW_GATE = 0, KV_WIDTH, 2 * KV_WIDTH
_T_ROWS = 2 * KV_WIDTH + N_KV_GROUPS * _GATE_ROWS

_NT = (((1,), (1,)), ((), ()))


def _dot(a, b):
    return jnp.dot(a, b, preferred_element_type=F32)


def _dot_nt(a, b):
    return lax.dot_general(a, b, _NT, preferred_element_type=F32)


def _layer_norm(y, g, b):
    mu = jnp.mean(y, axis=-1, keepdims=True)
    d = y - mu
    var = jnp.mean(d * d, axis=-1, keepdims=True)
    return d * lax.rsqrt(var + 1e-5) * g + b


def _inproj_kernel(x_ref, w_ref, wt_ref, cw_ref, cos_ref, rot_ref, hg_ref, gmean_ref,
                   conv_ref, q_ref, cv_ref, ks_ref, kw_ref, vst_ref, vwt_ref, gate_ref, zbuf):
    tm = x_ref.shape[0]
    xb = x_ref[...].astype(MXU_DTYPE)

    def mm(c0, width):
        return _dot(xb, w_ref[:, c0:c0 + width])

    @pl.when(pl.program_id(1) == 0)
    def _():
        zbuf[0:SUBLANES, :] = jnp.zeros((SUBLANES, CONV_WIDTH), F32)

    def conv_block(c0):
        cs = slice(c0, c0 + MXU_COLS)
        z = mm(_OFF_C + c0, MXU_COLS) * mm(_OFF_H + c0, MXU_COLS)
        zbuf[SUBLANES:SUBLANES + tm, cs] = z
        conv = (cw_ref[2:3, cs] * z + cw_ref[1:2, cs] * zbuf[SUBLANES - 1:SUBLANES - 1 + tm, cs]
                + cw_ref[0:1, cs] * zbuf[SUBLANES - 2:SUBLANES - 2 + tm, cs])
        zbuf[0:SUBLANES, cs] = zbuf[tm:tm + SUBLANES, cs]
        y = mm(_OFF_B + c0, MXU_COLS) * conv
        y2 = y * y
        y2_hi = y2.astype(MXU_DTYPE)
        y2_lo = (y2 - y2_hi.astype(F32)).astype(MXU_DTYPE)
        ms = _dot(y2_hi, gmean_ref[...]) + _dot(y2_lo, gmean_ref[...])
        conv_ref[:, cs] = (y * lax.rsqrt(ms + 1e-6) * hg_ref[:, cs]).astype(conv_ref.dtype)

    def rope(v, partner):
        return v * cos_ref[...] + partner * rot_ref[...]

    def split_heads(v, ref, lead):
        for g in range(N_KV_GROUPS):
            ref[lead + (g,)] = v[:, g * HEAD_DIM:(g + 1) * HEAD_DIM].astype(ref.dtype)

    lane = lax.broadcasted_iota(jnp.int32, (tm, LANES), 1)
    low = lane < HEAD_DIM

    def head_pair(v):
        return v, pltpu.roll(v, HEAD_DIM, 1)

    def q_block(c):
        r4 = mm(_OFF_Q + c * MXU_COLS, MXU_COLS)
        p4 = mm(_OFF_PQ + c * MXU_COLS, MXU_COLS)
        for i in range(MXU_COLS // LANES):
            sl = slice(i * LANES, (i + 1) * LANES)
            r = rope(r4[:, sl], p4[:, sl]) * (HEAD_DIM ** -0.5 * LOG2E)
            for j, h in enumerate(head_pair(r)):
                q_ref[4 * c + 2 * i + j] = jnp.where(low, h, 0.0).astype(q_ref.dtype)

    conv_block(0)
    q_block(0)
    conv_block(MXU_COLS)
    q_block(1)
    kvc = mm(_OFF_KC, 2 * KV_WIDTH)
    split_heads(rope(kvc[:, :KV_WIDTH], mm(_OFF_PKC, KV_WIDTH)), cv_ref, (0,))
    split_heads(kvc[:, KV_WIDTH:], cv_ref, (1,))
    ksw = mm(_OFF_KSW, 2 * KV_WIDTH)
    pksw = mm(_OFF_PKSW, 2 * KV_WIDTH)
    pos = pl.program_id(1) * tm + lax.broadcasted_iota(jnp.int32, (tm, LANES), 0)
    block_tag = jnp.where(pos // SLC_LEN == lane - HEAD_DIM, NEG, 0.0)
    for g, h in enumerate(head_pair(rope(ksw[:, :KV_WIDTH], pksw[:, :KV_WIDTH]))):
        ks_ref[g] = jnp.where(low, h, block_tag).astype(ks_ref.dtype)
    split_heads(rope(ksw[:, KV_WIDTH:], pksw[:, KV_WIDTH:]), kw_ref, ())

    vt = _dot_nt(wt_ref[...], xb)
    ones = jnp.ones((HEAD_DIM, tm), vst_ref.dtype)
    for g in range(N_KV_GROUPS):
        for ref, row0 in ((vst_ref, _ROW_VS), (vwt_ref, _ROW_VW)):
            ref[g, 0:HEAD_DIM, :] = vt[row0 + g * HEAD_DIM:row0 + (g + 1) * HEAD_DIM].astype(ref.dtype)
            ref[g, HEAD_DIM:2 * HEAD_DIM, :] = ones
        gate_ref[g] = jax.nn.sigmoid(vt[_ROW_GATE + g * _GATE_ROWS:_ROW_GATE + (g + 1) * _GATE_ROWS])


def _inproj(x, w_main, w_t, conv_w, cos_t, rot_t, hg_conv, gmean, tm):
    B, T, D = x.shape
    G = N_KV_GROUPS
    kv_shape = jax.ShapeDtypeStruct((B, G, T, HEAD_DIM), MXU_DTYPE)
    kv_spec = pl.BlockSpec((None, G, tm, HEAD_DIM), lambda b, i: (b, 0, i, 0))
    vt_shape = jax.ShapeDtypeStruct((B, G, 2 * HEAD_DIM, T), MXU_DTYPE)
    vt_spec = pl.BlockSpec((None, G, 2 * HEAD_DIM, tm), lambda b, i: (b, 0, 0, i))
    tab_spec = pl.BlockSpec((tm, LANES), lambda b, i: (i, 0))
    const2 = lambda b, i: (0, 0)
    return pl.pallas_call(
        _inproj_kernel,
        grid=(B, T // tm),
        in_specs=[
            pl.BlockSpec((None, tm, D), lambda b, i: (b, i, 0)),
            pl.BlockSpec((D, _MAIN_COLS), const2),
            pl.BlockSpec((_T_ROWS, D), const2),
            pl.BlockSpec((3, CONV_WIDTH), const2),
            tab_spec, tab_spec,
            pl.BlockSpec((1, CONV_WIDTH), const2),
            pl.BlockSpec((MXU_COLS, MXU_COLS), const2),
        ],
        out_specs=[
            pl.BlockSpec((None, tm, CONV_WIDTH), lambda b, i: (b, i, 0)),
            pl.BlockSpec((None, N_Q_HEADS, tm, LANES), lambda b, i: (b, 0, i, 0)),
            pl.BlockSpec((2, None, G, tm, HEAD_DIM), lambda b, i: (0, b, 0, i, 0)),
            pl.BlockSpec((None, G, tm, LANES), lambda b, i: (b, 0, i, 0)),
            kv_spec, vt_spec, vt_spec,
            pl.BlockSpec((None, G, _GATE_ROWS, tm), lambda b, i: (b, 0, 0, i)),
        ],
        out_shape=[
            jax.ShapeDtypeStruct((B, T, CONV_WIDTH), MXU_DTYPE),
            jax.ShapeDtypeStruct((B, N_Q_HEADS, T, LANES), MXU_DTYPE),
            jax.ShapeDtypeStruct((2, B, G, T, HEAD_DIM), F32),
            jax.ShapeDtypeStruct((B, G, T, LANES), MXU_DTYPE),
            kv_shape, vt_shape, vt_shape,
            jax.ShapeDtypeStruct((B, G, _GATE_ROWS, T), F32),
        ],
        scratch_shapes=[pltpu.VMEM((tm + 2 * SUBLANES, CONV_WIDTH), F32)],
        compiler_params=pltpu.CompilerParams(
            dimension_semantics=("arbitrary", "arbitrary"), vmem_limit_bytes=VMEM_LIMIT),
    )(x, w_main, w_t, conv_w, cos_t, rot_t, hg_conv, gmean)


def _compress_kernel(a_ref, pe_ref, w1_ref, b1_ref, w2_ref, b2_ref, w2t_ref, b2t_ref, out_ref, outt_ref,
                     sub_ref, bbuf):
    nsub = sub_ref.shape[0]
    half = CMP_STRIDE * HEAD_DIM
    for j in range(CMP_STRIDE):
        sub_ref[:, j * HEAD_DIM:(j + 1) * HEAD_DIM] = (
            a_ref[pl.ds(j, nsub, stride=CMP_STRIDE), :].astype(sub_ref.dtype))
    sub = sub_ref[...]
    top = _dot(sub, w1_ref[0:half, :])
    bbuf[0:nsub, :] = _dot(sub, w1_ref[half:2 * half, :])
    bbuf[nsub:nsub + SUBLANES, :] = jnp.zeros((SUBLANES, CMP_HIDDEN), F32)
    pe_rows = jnp.broadcast_to(pe_ref[...], (SUBLANES, 2 * half)).astype(MXU_DTYPE)
    const = _dot(pe_rows, w1_ref[...])[0:1, :] + b1_ref[...]
    h = top + bbuf[1:nsub + 1, :] + const
    act = jax.nn.gelu(h).astype(MXU_DTYPE)
    out_ref[...] = (_dot(act, w2_ref[...]) + b2_ref[...]).astype(out_ref.dtype)
    outt_ref[...] = (_dot_nt(w2t_ref[...], act) + b2t_ref[...]).astype(outt_ref.dtype)


def _compress(a, pe, w1, b1, w2, b2):
    _, BG, T, _ = a.shape
    nsub = T // CMP_STRIDE
    width = CMP_STRIDE * HEAD_DIM
    sel = lambda k, n: (k, 0, 0)
    per = lambda k, n: (k, n, 0, 0)
    return pl.pallas_call(
        _compress_kernel,
        grid=(2, BG),
        in_specs=[
            pl.BlockSpec((None, None, T, HEAD_DIM), per),
            pl.BlockSpec((None, 1, 2 * width), sel),
            pl.BlockSpec((None, 2 * width, CMP_HIDDEN), sel),
            pl.BlockSpec((None, 1, CMP_HIDDEN), sel),
            pl.BlockSpec((None, CMP_HIDDEN, HEAD_DIM), sel),
            pl.BlockSpec((None, 1, HEAD_DIM), sel),
            pl.BlockSpec((None, HEAD_DIM, CMP_HIDDEN), sel),
            pl.BlockSpec((None, HEAD_DIM, 1), sel),
        ],
        out_specs=[pl.BlockSpec((None, None, nsub, HEAD_DIM), per),
                   pl.BlockSpec((None, None, HEAD_DIM, nsub), per)],
        out_shape=[jax.ShapeDtypeStruct((2, BG, nsub, HEAD_DIM), MXU_DTYPE),
                   jax.ShapeDtypeStruct((2, BG, HEAD_DIM, nsub), MXU_DTYPE)],
        scratch_shapes=[pltpu.VMEM((nsub, width), MXU_DTYPE), pltpu.VMEM((nsub + SUBLANES, CMP_HIDDEN), F32)],
        compiler_params=pltpu.CompilerParams(
            dimension_semantics=("arbitrary", "arbitrary"), vmem_limit_bytes=VMEM_LIMIT),
    )(a, pe, w1, b1[:, None, :], w2, b2[:, None, :], jnp.swapaxes(w2, 1, 2), b2[:, :, None])


def _rank_accumulate(score, ranks, i0, i1):
    tq = score.shape[1]
    local = lax.broadcasted_iota(jnp.int32, (SUBLANES, tq), 0)
    ranks = list(ranks)
    for i in range(i0, i1):
        row = jnp.broadcast_to(score[i:i + 1, :], (SUBLANES, tq))
        for j in range(len(ranks)):
            slab = score[SUBLANES * j:SUBLANES * (j + 1), :]
            ge = jnp.where(row >= slab, 1.0, 0.0)
            gt = jnp.where(row > slab, 1.0, 0.0)
            if i < SUBLANES * j:
                ahead = ge
            elif i >= SUBLANES * (j + 1):
                ahead = gt
            else:
                ahead = jnp.where(local > i - SUBLANES * j, ge, gt)
            ranks[j] = ranks[j] + ahead
    return ranks


def _nsa_kernel(q_ref, kcmp_ref, vcmpt_ref, ks_ref, vst_ref, kw_ref, vwt_ref, gate_ref, aggt_ref, hg_ref,
                o_ref, sa_ref, sb_ref, acc_ref, *, tq, kc, wq):
    R = Q_PER_KV
    rows = R * tq
    T = ks_ref.shape[0]
    ns = T // SLC_LEN
    ncp = kcmp_ref.shape[0]
    q0 = pl.program_id(2) * tq
    t_row = q0 + lax.broadcasted_iota(jnp.int32, (1, tq), 1)
    q_pad = q_ref[...]
    q64 = q_pad.reshape(rows, LANES)[:, :HEAD_DIM]

    def all_heads(a):
        return jnp.concatenate([a] * R, axis=1)

    def head(a, r, width):
        return a[:, r * width:(r + 1) * width]

    n_win = tq // wq
    wl = WINDOW + wq

    def window_piece(h):
        t_piece = t_row[:, h * wq:(h + 1) * wq]
        ws = pl.multiple_of(jnp.maximum(q0 + h * wq - WINDOW, 0), wq)
        wpos = ws + lax.broadcasted_iota(jnp.int32, (wl, wq), 0)
        wbias = jnp.where((wpos <= t_piece) & (wpos > t_piece - WINDOW), 0.0, NEG)
        qh = q64.reshape(R, tq, HEAD_DIM)[:, h * wq:(h + 1) * wq, :].reshape(R * wq, HEAD_DIM)
        s = _dot_nt(kw_ref[pl.ds(ws, wl), :], qh) + jnp.concatenate([wbias] * R, axis=1)
        p = jnp.exp2(s - jnp.max(s, axis=0, keepdims=True)).astype(MXU_DTYPE)
        ow = _dot(vwt_ref[:, pl.ds(ws, wl)], p)
        return ow[:HEAD_DIM] / ow[HEAD_DIM:]

    cmp_end = lax.broadcasted_iota(jnp.int32, (ncp, tq), 0) * CMP_STRIDE + (CMP_LEN - 1)
    cbias = jnp.where(cmp_end <= t_row, 0.0, NEG)
    s = _dot_nt(kcmp_ref[...], q64) + all_heads(cbias)
    p = jnp.exp2(s - jnp.max(s, axis=0, keepdims=True))
    l = jnp.sum(p, axis=0, keepdims=True)
    p = p * (all_heads(jnp.where(t_row >= CMP_LEN - 1, 1.0, 0.0)) / l)
    o_cmp = _dot(vcmpt_ref[...], p.astype(MXU_DTYPE))

    psum = head(p, 0, tq) + head(p, 1, tq) + head(p, 2, tq) + head(p, 3, tq)
    p_hi = psum.astype(MXU_DTYPE)
    p_lo = (psum - p_hi.astype(F32)).astype(MXU_DTYPE)
    imp = _dot(aggt_ref[...], p_hi) + _dot(aggt_ref[...], p_lo)
    blk = lax.broadcasted_iota(jnp.int32, (ns, tq), 0)
    tb = (q0 + lax.broadcasted_iota(jnp.int32, (ns, tq), 1)) // SLC_LEN
    forced = (blk == 0) | (blk == tb) | (blk == tb - 1)
    score = jnp.where(forced, BIG, imp)
    score = jnp.where(blk <= tb, score, -BIG)
    ranks = [jnp.zeros((SUBLANES, tq), F32) for _ in range(ns // SUBLANES)]
    ranks = _rank_accumulate(score, ranks, 0, ns // 2)
    o_win_pieces = [window_piece(h) for h in range(n_win // 2)]
    ranks = _rank_accumulate(score, ranks, ns // 2, ns)
    unselected = jnp.where(jnp.concatenate(ranks, axis=0) < min(SLC_TOPK, ns), 0.0, 1.0)
    tag_rows = [jnp.zeros((HEAD_DIM, tq), F32), unselected]
    if ns < LANES - HEAD_DIM:
        tag_rows.append(jnp.zeros((LANES - HEAD_DIM - ns, tq), F32))
    tag = jnp.concatenate(tag_rows, axis=0).T.astype(MXU_DTYPE)
    lane = lax.broadcasted_iota(jnp.int32, (R, tq, LANES), 2)
    q_aug = jnp.where(lane < HEAD_DIM, q_pad, tag[None]).reshape(rows, LANES)

    def scores(c):
        return _dot_nt(ks_ref[pl.ds(pl.multiple_of(c * kc, kc), kc), :], q_aug)

    def causal_bias(c):
        kpos = c * kc + lax.broadcasted_iota(jnp.int32, (kc, tq), 0)
        return all_heads(jnp.where(kpos <= t_row, 0.0, NEG))

    def consume(s, c, m):
        m_new = jnp.maximum(m, jnp.max(s, axis=0, keepdims=True))
        p = jnp.exp2(s - m_new).astype(MXU_DTYPE)
        pv = _dot(vst_ref[:, pl.ds(pl.multiple_of(c * kc, kc), kc)], p)
        acc_ref[...] = jnp.exp2(m - m_new) * acc_ref[...] + pv
        return m_new

    n_pairs = q0 // (2 * kc)
    acc_ref[...] = jnp.zeros((2 * HEAD_DIM, rows), F32)
    sa_ref[...] = scores(0)
    o_win_pieces += [window_piece(h) for h in range(n_win // 2, n_win)]

    def pair_step(j, m):
        sb_ref[...] = scores(2 * j + 1)
        m = consume(sa_ref[...], 2 * j, m)
        sa_ref[...] = scores(2 * j + 2)
        return consume(sb_ref[...], 2 * j + 1, m)

    m = lax.fori_loop(0, n_pairs, pair_step, jnp.full((1, rows), NEG, F32))
    c_last = 2 * n_pairs
    m = consume(sa_ref[...] + causal_bias(c_last), c_last, m)

    @pl.when(q0 + tq > (c_last + 1) * kc)
    def _():
        consume(scores(c_last + 1) + causal_bias(c_last + 1), c_last + 1, m)

    acc = acc_ref[...]
    o_slc = acc[:HEAD_DIM] / acc[HEAD_DIM:]

    g = gate_ref[...]
    outs = []
    for r in range(R):
        c = N_GATES * r
        o_win = jnp.concatenate([head(piece, r, wq) for piece in o_win_pieces], axis=1)
        o = g[c:c + 1] * head(o_cmp, r, tq) + g[c + 1:c + 2] * head(o_slc, r, tq) + g[c + 2:c + 3] * o_win
        ms = jnp.mean(o * o, axis=0, keepdims=True)
        outs.append(o * lax.rsqrt(ms + 1e-6) * hg_ref[r * HEAD_DIM:(r + 1) * HEAD_DIM, :])
    o_ref[...] = jnp.concatenate(outs, axis=0).T.astype(o_ref.dtype)


def _nsa(q, cmp_kv, cmp_kv_t, ks, vst, kw, vwt, gates, agg_t, hg_attn, tq, kc, wq):
    B, _, T, _ = q.shape
    G, R = N_KV_GROUPS, Q_PER_KV
    ncp = cmp_kv.shape[2]
    ns = T // SLC_LEN
    assert T % (2 * kc) == 0 and kc % tq == 0 and tq % (2 * wq) == 0 and WINDOW % wq == 0 and T >= WINDOW + tq
    assert ns <= LANES - HEAD_DIM, "block tags must fit in the upper lanes of a key row"
    whole = lambda b, g, i: (b, g, 0, 0)
    hg_lanes = jnp.broadcast_to(hg_attn.reshape(G, R * HEAD_DIM, 1), (G, R * HEAD_DIM, tq))
    return pl.pallas_call(
        functools.partial(_nsa_kernel, tq=tq, kc=kc, wq=wq),
        grid=(B, G, T // tq),
        in_specs=[
            pl.BlockSpec((None, R, tq, LANES), lambda b, g, i: (b, g, i, 0)),
            pl.BlockSpec((None, None, ncp, HEAD_DIM), lambda b, g, i: (0, b * G + g, 0, 0)),
            pl.BlockSpec((None, None, HEAD_DIM, ncp), lambda b, g, i: (1, b * G + g, 0, 0)),
            pl.BlockSpec((None, None, T, LANES), whole),
            pl.BlockSpec((None, None, 2 * HEAD_DIM, T), whole),
            pl.BlockSpec((None, None, T, HEAD_DIM), whole),
            pl.BlockSpec((None, None, 2 * HEAD_DIM, T), whole),
            pl.BlockSpec((None, None, _GATE_ROWS, tq), lambda b, g, i: (b, g, 0, i)),
            pl.BlockSpec((ns, ncp), lambda b, g, i: (0, 0)),
            pl.BlockSpec((None, R * HEAD_DIM, tq), lambda b, g, i: (g, 0, 0)),
        ],
        out_specs=pl.BlockSpec((None, tq, R * HEAD_DIM), lambda b, g, i: (b, i, g)),
        out_shape=jax.ShapeDtypeStruct((B, T, ATTN_WIDTH), MXU_DTYPE),
        scratch_shapes=[pltpu.VMEM((kc, R * tq), F32), pltpu.VMEM((kc, R * tq), F32),
                        pltpu.VMEM((2 * HEAD_DIM, R * tq), F32)],
        compiler_params=pltpu.CompilerParams(
            dimension_semantics=("arbitrary", "arbitrary", "arbitrary"), vmem_limit_bytes=VMEM_LIMIT),
    )(q, cmp_kv, cmp_kv_t, ks, vst, kw, vwt, gates, agg_t, hg_lanes)


def _mix_ffn_kernel(conv_ref, attn_ref, x_ref, wout_ref, g1_ref, b1_ref,
                    wup_ref, cw_ref, cb_ref, wdn_ref, g_ref, b_ref, o_ref, tail_ref, ubuf, act_ref,
                    *, tiles_per_seq, fc):
    tm = x_ref.shape[0]

    @pl.when(pl.program_id(0) % tiles_per_seq == 0)
    def _():
        tail_ref[...] = jnp.zeros(tail_ref.shape, F32)

    mix = _dot(conv_ref[...], wout_ref[0:CONV_WIDTH, :]) + _dot(attn_ref[...], wout_ref[CONV_WIDTH:, :])
    x = _layer_norm(ALPHA * x_ref[...] + mix, g1_ref[...], b1_ref[...])
    xb = jnp.concatenate([tail_ref[...], x], axis=0).astype(MXU_DTYPE)
    tail_ref[...] = x[tm - SUBLANES:]

    def up_conv(col0, buf):
        buf[...] = _dot(xb, wup_ref[:, col0:col0 + fc])
        return (cw_ref[2:3, col0:col0 + fc] * buf[SUBLANES:SUBLANES + tm, :]
                + cw_ref[1:2, col0:col0 + fc] * buf[SUBLANES - 1:SUBLANES - 1 + tm, :]
                + cw_ref[0:1, col0:col0 + fc] * buf[SUBLANES - 2:SUBLANES - 2 + tm, :]
                + cb_ref[:, col0:col0 + fc])

    for c in range(D_FF // fc):
        val = up_conv(c * fc, ubuf.at[c % 2, 0])
        gate = up_conv(D_FF + c * fc, ubuf.at[c % 2, 1])
        act_ref[:, c * fc:(c + 1) * fc] = (gate * jax.nn.sigmoid(gate) * val).astype(act_ref.dtype)
    o_ref[...] = _layer_norm(ALPHA * x + _dot(act_ref[...], wdn_ref[...]), g_ref[...], b_ref[...])


def _mix_ffn(conv_n, attn_n, x, w_out, g1, b1, w_up, conv_w, conv_b, w_down, g, b, tm, tiles_per_seq, fc):
    N, D = x.shape
    row = lambda i: (i, 0)
    const = lambda i: (0, 0)
    return pl.pallas_call(
        functools.partial(_mix_ffn_kernel, tiles_per_seq=tiles_per_seq, fc=fc),
        grid=(N // tm,),
        in_specs=[
            pl.BlockSpec((tm, CONV_WIDTH), row), pl.BlockSpec((tm, ATTN_WIDTH), row), pl.BlockSpec((tm, D), row),
            pl.BlockSpec((D, D), const, pipeline_mode=pl.Buffered(1)),
            pl.BlockSpec((1, D), const), pl.BlockSpec((1, D), const),
            pl.BlockSpec((D, 2 * D_FF), const, pipeline_mode=pl.Buffered(1)),
            pl.BlockSpec((3, 2 * D_FF), const),
            pl.BlockSpec((1, 2 * D_FF), const),
            pl.BlockSpec((D_FF, D), const, pipeline_mode=pl.Buffered(1)),
            pl.BlockSpec((1, D), const), pl.BlockSpec((1, D), const),
        ],
        out_specs=pl.BlockSpec((tm, D), row),
        out_shape=jax.ShapeDtypeStruct((N, D), F32),
        scratch_shapes=[pltpu.VMEM((SUBLANES, D), F32), pltpu.VMEM((2, 2, tm + SUBLANES, fc), F32),
                        pltpu.VMEM((tm, D_FF), MXU_DTYPE)],
        compiler_params=pltpu.CompilerParams(
            dimension_semantics=("arbitrary",), vmem_limit_bytes=VMEM_LIMIT),
    )(conv_n, attn_n, x, w_out, g1, b1, w_up, conv_w, conv_b, w_down, g, b)


def _rope_tables_kernel(freq_hi_ref, freq_lo_ref, sign_ref, cos_ref, rot_ref):
    tr = cos_ref.shape[0]
    pos = (pl.program_id(0) * tr + lax.broadcasted_iota(jnp.int32, (tr, LANES), 0)).astype(F32)
    ang = pos * freq_hi_ref[...] + pos * freq_lo_ref[...]
    cos_ref[...] = jnp.cos(ang)
    rot_ref[...] = jnp.sin(ang) * sign_ref[...]


def _rope_lane_tables(T, tr):
    half = ROPE_DIM // 2
    inv_freq = ROPE_THETA ** (-(np.arange(half, dtype=np.float64) * 2.0 / ROPE_DIM))
    lane = np.arange(LANES) % HEAD_DIM
    freq = np.where(lane < ROPE_DIM, inv_freq[lane % half], 0.0)
    freq_hi = freq.astype(np.float32)
    freq_lo = (freq - freq_hi.astype(np.float64)).astype(np.float32)
    sign = np.where(lane < half, -1.0, np.where(lane < ROPE_DIM, 1.0, 0.0)).astype(np.float32)
    lanes_spec = pl.BlockSpec((1, LANES), lambda i: (0, 0))
    rows_spec = pl.BlockSpec((tr, LANES), lambda i: (i, 0))
    return pl.pallas_call(
        _rope_tables_kernel,
        grid=(T // tr,),
        in_specs=[lanes_spec, lanes_spec, lanes_spec],
        out_specs=[rows_spec, rows_spec],
        out_shape=[jax.ShapeDtypeStruct((T, LANES), F32)] * 2,
    )(jnp.asarray(freq_hi)[None], jnp.asarray(freq_lo)[None], jnp.asarray(sign)[None])


def _group_mean_matrix():
    grp = np.arange(MXU_COLS) // HEAD_DIM
    return jnp.asarray((grp[:, None] == grp[None, :]) / HEAD_DIM, dtype=MXU_DTYPE)


def _slc_aggregation_t(T):
    nc = T // CMP_STRIDE - CMP_LEN // CMP_STRIDE + 1
    ns = T // SLC_LEN
    sc = np.arange(nc)[None, :] * CMP_STRIDE
    ss = np.arange(ns)[:, None] * SLC_LEN
    ov = np.clip(np.minimum(sc + CMP_LEN, ss + SLC_LEN) - np.maximum(sc, ss), 0, None) / CMP_LEN
    ov = np.pad(ov, ((0, 0), (0, T // CMP_STRIDE - nc)))
    return jnp.asarray(ov, dtype=MXU_DTYPE)


def _split_in_weights(w):
    w = w.astype(MXU_DTYPE)
    plain = jnp.concatenate([w[..., :_W_VS], w[..., _W_KW:_W_VW]], axis=-1)
    half = ROPE_DIM // 2
    zeros = jnp.zeros(w.shape[:-1] + (HEAD_DIM - ROPE_DIM,), w.dtype)
    partners = []
    for c0, width in ((_OFF_Q, ATTN_WIDTH), (_OFF_KSW, 2 * KV_WIDTH), (_OFF_KC, KV_WIDTH)):
        for h0 in range(c0, c0 + width, HEAD_DIM):
            partners += [plain[..., h0 + half:h0 + ROPE_DIM], plain[..., h0:h0 + half], zeros]
    w_main = jnp.concatenate([plain] + partners, axis=-1)
    per_group = Q_PER_KV * N_GATES
    pad = jnp.zeros(w.shape[:-1] + (_GATE_ROWS - per_group,), w.dtype)
    cols = [w[..., _W_VS:_W_KW], w[..., _W_VW:_W_GATE]]
    for g in range(N_KV_GROUPS):
        cols += [w[..., _W_GATE + g * per_group:_W_GATE + (g + 1) * per_group], pad]
    return w_main, jnp.swapaxes(jnp.concatenate(cols, axis=-1), -1, -2)


def kernel(x, w_in, short_conv_w, cmp_pe, cmp_w1, cmp_b1, cmp_w2, cmp_b2, head_norm_g, w_out, ln1_g, ln1_b,
           w_up, ffn_conv_w, ffn_conv_b, w_down, ln2_g, ln2_b):
    B, T, D = x.shape
    tm = min(512, T)
    cos_t, rot_t = _rope_lane_tables(T, tm)
    gmean = _group_mean_matrix()
    agg_t = _slc_aggregation_t(T)
    w_main, w_t = _split_in_weights(w_in)
    cmp_w1_b, cmp_w2_b = cmp_w1.astype(MXU_DTYPE), cmp_w2.astype(MXU_DTYPE)
    w_out_b, w_up_b, w_down_b = w_out.astype(MXU_DTYPE), w_up.astype(MXU_DTYPE), w_down.astype(MXU_DTYPE)
    cmp_pe_flat = cmp_pe.reshape(DEPTH, 2, 1, CMP_LEN * HEAD_DIM)
    for i in range(DEPTH):
        hg = head_norm_g[i]
        conv_n, q, cv, ks, kw, vst, vwt, gates = _inproj(
            x, w_main[i], w_t[i], short_conv_w[i], cos_t, rot_t, hg[None, :CONV_WIDTH], gmean, tm)
        cmp_kv, cmp_kv_t = _compress(
            cv.reshape(2, B * N_KV_GROUPS, T, HEAD_DIM), cmp_pe_flat[i], cmp_w1_b[i], cmp_b1[i], cmp_w2_b[i],
            cmp_b2[i])
        attn_n = _nsa(q, cmp_kv, cmp_kv_t, ks, vst, kw, vwt, gates, agg_t, hg[CONV_WIDTH:], tq=512, kc=512,
                      wq=128)
        x = _mix_ffn(conv_n.reshape(B * T, CONV_WIDTH), attn_n.reshape(B * T, ATTN_WIDTH), x.reshape(B * T, D),
                     w_out_b[i], ln1_g[i][None], ln1_b[i][None], w_up_b[i], ffn_conv_w[i], ffn_conv_b[i][None],
                     w_down_b[i], ln2_g[i][None], ln2_b[i][None], tm, T // tm, 256).reshape(B, T, D)
    return x
```

```python
import functools

import numpy as np
import jax
import jax.numpy as jnp
from jax import lax
from jax.experimental import pallas as pl
from jax.experimental.pallas import tpu as pltpu

D_MODEL = 1024
DEPTH = 2
HEAD_DIM = 64
CONV_WIDTH = 512
ATTN_WIDTH = 512
N_Q_HEADS = 8
N_KV_GROUPS = 2
Q_PER_KV = 4
KV_WIDTH = N_KV_GROUPS * HEAD_DIM
N_GATES = 3
CMP_LEN = 32
CMP_STRIDE = 16
CMP_HIDDEN = 256
SLC_LEN = 64
SLC_TOPK = 16
WINDOW = 512
ROPE_THETA = 500000.0
ROPE_DIM = 16
D_FF = 2816
ALPHA = (2.0 * DEPTH) ** 0.25
NEG = -1e30
BIG = 1e9

F32 = jnp.float32
MXU_DTYPE = jnp.bfloat16
LANES = 128
SUBLANES = 8
MXU_COLS = 256
VMEM_LIMIT = 56 * 1024 * 1024

_W_VS, _W_KW, _W_VW, _W_GATE, _W_END = 2432, 2560, 2688, 2816, 2840
_OFF_B, _OFF_C, _OFF_H, _OFF_Q, _OFF_KC, _OFF_KSW = 0, 512, 1024, 1536, 2048, 2304
_OFF_PQ, _OFF_PKSW, _OFF_PKC = 2560, 3072, 3328
_MAIN_COLS = 3456
LOG2E = 1.4426950408889634
_GATE_ROWS = 16
_ROW_VS, _ROW_VW, _ROW_GATE = 0, KV_WIDTH, 2 * KV_WIDTH
_T_ROWS = 2 * KV_WIDTH + N_KV_GROUPS * _GATE_ROWS

_NT = (((1,), (1,)), ((), ()))


def _dot(a, b):
    return jnp.dot(a, b, preferred_element_type=F32)


def _dot_nt(a, b):
    return lax.dot_general(a, b, _NT, preferred_element_type=F32)


def _layer_norm(y, g, b):
    mu = jnp.mean(y, axis=-1, keepdims=True)
    d = y - mu
    var = jnp.mean(d * d, axis=-1, keepdims=True)
    return d * lax.rsqrt(var + 1e-5) * g + b


def _inproj_kernel(x_ref, w_ref, wt_ref, cw_ref, cos_ref, rot_ref, hg_ref, gmean_ref,
                   conv_ref, q_ref, cv_ref, ks_ref, kw_ref, vst_ref, vwt_ref, gate_ref, zbuf):
    tm = x_ref.shape[0]
    xb = x_ref[...].astype(MXU_DTYPE)

    def mm(c0, width):
        return _dot(xb, w_ref[:, c0:c0 + width])

    @pl.when(pl.program_id(1) == 0)
    def _():
        zbuf[0:SUBLANES, :] = jnp.zeros((SUBLANES, CONV_WIDTH), F32)

    def conv_block(c0):
        cs = slice(c0, c0 + MXU_COLS)
        z = mm(_OFF_C + c0, MXU_COLS) * mm(_OFF_H + c0, MXU_COLS)
        zbuf[SUBLANES:SUBLANES + tm, cs] = z
        conv = (cw_ref[2:3, cs] * z + cw_ref[1:2, cs] * zbuf[SUBLANES - 1:SUBLANES - 1 + tm, cs]
                + cw_ref[0:1, cs] * zbuf[SUBLANES - 2:SUBLANES - 2 + tm, cs])
        zbuf[0:SUBLANES, cs] = zbuf[tm:tm + SUBLANES, cs]
        y = mm(_OFF_B + c0, MXU_COLS) * conv
        y2 = y * y
        y2_hi = y2.astype(MXU_DTYPE)
        y2_lo = (y2 - y2_hi.astype(F32)).astype(MXU_DTYPE)
        ms = _dot(y2_hi, gmean_ref[...]) + _dot(y2_lo, gmean_ref[...])
        conv_ref[:, cs] = (y * lax.rsqrt(ms + 1e-6) * hg_ref[:, cs]).astype(conv_ref.dtype)

    def rope(v, partner):
        return v * cos_ref[...] + partner * rot_ref[...]

    def split_heads(v, ref, lead):
        for g in range(N_KV_GROUPS):
            ref[lead + (g,)] = v[:, g * HEAD_DIM:(g + 1) * HEAD_DIM].astype(ref.dtype)

    lane = lax.broadcasted_iota(jnp.int32, (tm, LANES), 1)
    low = lane < HEAD_DIM

    def head_pair(v):
        return v, pltpu.roll(v, HEAD_DIM, 1)

    def q_block(c):
        r4 = mm(_OFF_Q + c * MXU_COLS, MXU_COLS)
        p4 = mm(_OFF_PQ + c * MXU_COLS, MXU_COLS)
        for i in range(MXU_COLS // LANES):
            sl = slice(i * LANES, (i + 1) * LANES)
            r = rope(r4[:, sl], p4[:, sl]) * (HEAD_DIM ** -0.5 * LOG2E)
            for j, h in enumerate(head_pair(r)):
                q_ref[4 * c + 2 * i + j] = jnp.where(low, h, 0.0).astype(q_ref.dtype)

    conv_block(0)
    q_block(0)
    conv_block(MXU_COLS)
    q_block(1)
    kvc = mm(_OFF_KC, 2 * KV_WIDTH)
    split_heads(rope(kvc[:, :KV_WIDTH], mm(_OFF_PKC, KV_WIDTH)), cv_ref, (0,))
    split_heads(kvc[:, KV_WIDTH:], cv_ref, (1,))
    ksw = mm(_OFF_KSW, 2 * KV_WIDTH)
    pksw = mm(_OFF_PKSW, 2 * KV_WIDTH)
    pos = pl.program_id(1) * tm + lax.broadcasted_iota(jnp.int32, (tm, LANES), 0)
    block_tag = jnp.where(pos // SLC_LEN == lane - HEAD_DIM, NEG, 0.0)
    for g, h in enumerate(head_pair(rope(ksw[:, :KV_WIDTH], pksw[:, :KV_WIDTH]))):
        ks_ref[g] = jnp.where(low, h, block_tag).astype(ks_ref.dtype)
    split_heads(rope(ksw[:, KV_WIDTH:], pksw[:, KV_WIDTH:]), kw_ref, ())

    vt = _dot_nt(wt_ref[...], xb)
    ones = jnp.ones((HEAD_DIM, tm), vst_ref.dtype)
    for g in range(N_KV_GROUPS):
        for ref, row0 in ((vst_ref, _ROW_VS), (vwt_ref, _ROW_VW)):
            ref[g, 0:HEAD_DIM, :] = vt[row0 + g * HEAD_DIM:row0 + (g + 1) * HEAD_DIM].astype(ref.dtype)
            ref[g, HEAD_DIM:2 * HEAD_DIM, :] = ones
        gate_ref[g] = jax.nn.sigmoid(vt[_ROW_GATE + g * _GATE_ROWS:_ROW_GATE + (g + 1) * _GATE_ROWS])


def _inproj(x, w_main, w_t, conv_w, cos_t, rot_t, hg_conv, gmean, tm):
    B, T, D = x.shape
    G = N_KV_GROUPS
    kv_shape = jax.ShapeDtypeStruct((B, G, T, HEAD_DIM), MXU_DTYPE)
    kv_spec = pl.BlockSpec((None, G, tm, HEAD_DIM), lambda b, i: (b, 0, i, 0))
    vt_shape = jax.ShapeDtypeStruct((B, G, 2 * HEAD_DIM, T), MXU_DTYPE)
    vt_spec = pl.BlockSpec((None, G, 2 * HEAD_DIM, tm), lambda b, i: (b, 0, 0, i))
    tab_spec = pl.BlockSpec((tm, LANES), lambda b, i: (i, 0))
    const2 = lambda b, i: (0, 0)
    return pl.pallas_call(
        _inproj_kernel,
        grid=(B, T // tm),
        in_specs=[
            pl.BlockSpec((None, tm, D), lambda b, i: (b, i, 0)),
            pl.BlockSpec((D, _MAIN_COLS), const2),
            pl.BlockSpec((_T_ROWS, D), const2),
            pl.BlockSpec((3, CONV_WIDTH), const2),
            tab_spec, tab_spec,
            pl.BlockSpec((1, CONV_WIDTH), const2),
            pl.BlockSpec((MXU_COLS, MXU_COLS), const2),
        ],
        out_specs=[
            pl.BlockSpec((None, tm, CONV_WIDTH), lambda b, i: (b, i, 0)),
            pl.BlockSpec((None, N_Q_HEADS, tm, LANES), lambda b, i: (b, 0, i, 0)),
            pl.BlockSpec((2, None, G, tm, HEAD_DIM), lambda b, i: (0, b, 0, i, 0)),
            pl.BlockSpec((None, G, tm, LANES), lambda b, i: (b, 0, i, 0)),
            kv_spec, vt_spec, vt_spec,
            pl.BlockSpec((None, G, _GATE_ROWS, tm), lambda b, i: (b, 0, 0, i)),
        ],
        out_shape=[
            jax.ShapeDtypeStruct((B, T, CONV_WIDTH), MXU_DTYPE),
            jax.ShapeDtypeStruct((B, N_Q_HEADS, T, LANES), MXU_DTYPE),
            jax.ShapeDtypeStruct((2, B, G, T, HEAD_DIM), F32),
            jax.ShapeDtypeStruct((B, G, T, LANES), MXU_DTYPE),
            kv_shape, vt_shape, vt_shape,
            jax.ShapeDtypeStruct((B, G, _GATE_ROWS, T), F32),
        ],
        scratch_shapes=[pltpu.VMEM((tm + 2 * SUBLANES, CONV_WIDTH), F32)],
        compiler_params=pltpu.CompilerParams(
            dimension_semantics=("arbitrary", "arbitrary"), vmem_limit_bytes=VMEM_LIMIT),
    )(x, w_main, w_t, conv_w, cos_t, rot_t, hg_conv, gmean)


def _compress_kernel(a_ref, pe_ref, w1_ref, b1_ref, w2_ref, b2_ref, w2t_ref, b2t_ref, out_ref, outt_ref,
                     sub_ref, bbuf):
    nsub = sub_ref.shape[0]
    half = CMP_STRIDE * HEAD_DIM
    for j in range(CMP_STRIDE):
        sub_ref[:, j * HEAD_DIM:(j + 1) * HEAD_DIM] = (
            a_ref[pl.ds(j, nsub, stride=CMP_STRIDE), :].astype(sub_ref.dtype))
    sub = sub_ref[...]
    top = _dot(sub, w1_ref[0:half, :])
    bbuf[0:nsub, :] = _dot(sub, w1_ref[half:2 * half, :])
    bbuf[nsub:nsub + SUBLANES, :] = jnp.zeros((SUBLANES, CMP_HIDDEN), F32)
    pe_rows = jnp.broadcast_to(pe_ref[...], (SUBLANES, 2 * half)).astype(MXU_DTYPE)
    const = _dot(pe_rows, w1_ref[...])[0:1, :] + b1_ref[...]
    h = top + bbuf[1:nsub + 1, :] + const
    act = jax.nn.gelu(h).astype(MXU_DTYPE)
    out_ref[...] = (_dot(act, w2_ref[...]) + b2_ref[...]).astype(out_ref.dtype)
    outt_ref[...] = (_dot_nt(w2t_ref[...], act) + b2t_ref[...]).astype(outt_ref.dtype)


def _compress(a, pe, w1, b1, w2, b2):
    _, BG, T, _ = a.shape
    nsub = T // CMP_STRIDE
    width = CMP_STRIDE * HEAD_DIM
    sel = lambda k, n: (k, 0, 0)
    per = lambda k, n: (k, n, 0, 0)
    return pl.pallas_call(
        _compress_kernel,
        grid=(2, BG),
        in_specs=[
            pl.BlockSpec((None, None, T, HEAD_DIM), per),
            pl.BlockSpec((None, 1, 2 * width), sel),
            pl.BlockSpec((None, 2 * width, CMP_HIDDEN), sel),
            pl.BlockSpec((None, 1, CMP_HIDDEN), sel),
            pl.BlockSpec((None, CMP_HIDDEN, HEAD_DIM), sel),
            pl.BlockSpec((None, 1, HEAD_DIM), sel),
            pl.BlockSpec((None, HEAD_DIM, CMP_HIDDEN), sel),
            pl.BlockSpec((None, HEAD_DIM, 1), sel),
        ],
        out_specs=[pl.BlockSpec((None, None, nsub, HEAD_DIM), per),
                   pl.BlockSpec((None, None, HEAD_DIM, nsub), per)],
        out_shape=[jax.ShapeDtypeStruct((2, BG, nsub, HEAD_DIM), MXU_DTYPE),
                   jax.ShapeDtypeStruct((2, BG, HEAD_DIM, nsub), MXU_DTYPE)],
        scratch_shapes=[pltpu.VMEM((nsub, width), MXU_DTYPE), pltpu.VMEM((nsub + SUBLANES, CMP_HIDDEN), F32)],
        compiler_params=pltpu.CompilerParams(
            dimension_semantics=("arbitrary", "arbitrary"), vmem_limit_bytes=VMEM_LIMIT),
    )(a, pe, w1, b1[:, None, :], w2, b2[:, None, :], jnp.swapaxes(w2, 1, 2), b2[:, :, None])


def _rank_accumulate(score, ranks, i0, i1):
    tq = score.shape[1]
    local = lax.broadcasted_iota(jnp.int32, (SUBLANES, tq), 0)
    ranks = list(ranks)
    for i in range(i0, i1):
        row = jnp.broadcast_to(score[i:i + 1, :], (SUBLANES, tq))
        for j in range(len(ranks)):
            slab = score[SUBLANES * j:SUBLANES * (j + 1), :]
            ge = jnp.where(row >= slab, 1.0, 0.0)
            gt = jnp.where(row > slab, 1.0, 0.0)
            if i < SUBLANES * j:
                ahead = ge
            elif i >= SUBLANES * (j + 1):
                ahead = gt
            else:
                ahead = jnp.where(local > i - SUBLANES * j, ge, gt)
            ranks[j] = ranks[j] + ahead
    return ranks


def _nsa_kernel(q_ref, kcmp_ref, vcmpt_ref, ks_ref, vst_ref, kw_ref, vwt_ref, gate_ref, aggt_ref, hg_ref,
                o_ref, sa_ref, sb_ref, acc_ref, *, tq, kc, wq):
    R = Q_PER_KV
    rows = R * tq
    T = ks_ref.shape[0]
    ns = T // SLC_LEN
    ncp = kcmp_ref.shape[0]
    q0 = pl.program_id(2) * tq
    t_row = q0 + lax.broadcasted_iota(jnp.int32, (1, tq), 1)
    q_pad = q_ref[...]
    q64 = q_pad.reshape(rows, LANES)[:, :HEAD_DIM]

    def all_heads(a):
        return jnp.concatenate([a] * R, axis=1)

    def head(a, r, width):
        return a[:, r * width:(r + 1) * width]

    n_win = tq // wq
    wl = WINDOW + wq

    def window_piece(h):
        t_piece = t_row[:, h * wq:(h + 1) * wq]
        ws = pl.multiple_of(jnp.maximum(q0 + h * wq - WINDOW, 0), wq)
        wpos = ws + lax.broadcasted_iota(jnp.int32, (wl, wq), 0)
        wbias = jnp.where((wpos <= t_piece) & (wpos > t_piece - WINDOW), 0.0, NEG)
        qh = q64.reshape(R, tq, HEAD_DIM)[:, h * wq:(h + 1) * wq, :].reshape(R * wq, HEAD_DIM)
        s = _dot_nt(kw_ref[pl.ds(ws, wl), :], qh) + jnp.concatenate([wbias] * R, axis=1)
        p = jnp.exp2(s - jnp.max(s, axis=0, keepdims=True)).astype(MXU_DTYPE)
        ow = _dot(vwt_ref[:, pl.ds(ws, wl)], p)
        return ow[:HEAD_DIM] / ow[HEAD_DIM:]

    cmp_end = lax.broadcasted_iota(jnp.int32, (ncp, tq), 0) * CMP_STRIDE + (CMP_LEN - 1)
    cbias = jnp.where(cmp_end <= t_row, 0.0, NEG)
    s = _dot_nt(kcmp_ref[...], q64) + all_heads(cbias)
    p = jnp.exp2(s - jnp.max(s, axis=0, keepdims=True))
    l = jnp.sum(p, axis=0, keepdims=True)
    p = p * (all_heads(jnp.where(t_row >= CMP_LEN - 1, 1.0, 0.0)) / l)
    o_cmp = _dot(vcmpt_ref[...], p.astype(MXU_DTYPE))

    psum = head(p, 0, tq) + head(p, 1, tq) + head(p, 2, tq) + head(p, 3, tq)
    p_hi = psum.astype(MXU_DTYPE)
    p_lo = (psum - p_hi.astype(F32)).astype(MXU_DTYPE)
    imp = _dot(aggt_ref[...], p_hi) + _dot(aggt_ref[...], p_lo)
    blk = lax.broadcasted_iota(jnp.int32, (ns, tq), 0)
    tb = (q0 + lax.broadcasted_iota(jnp.int32, (ns, tq), 1)) // SLC_LEN
    forced = (blk == 0) | (blk == tb) | (blk == tb - 1)
    score = jnp.where(forced, BIG, imp)
    score = jnp.where(blk <= tb, score, -BIG)
    ranks = [jnp.zeros((SUBLANES, tq), F32) for _ in range(ns // SUBLANES)]
    ranks = _rank_accumulate(score, ranks, 0, ns // 2)
    o_win_pieces = [window_piece(h) for h in range(n_win // 2)]
    ranks = _rank_accumulate(score, ranks, ns // 2, ns)
    unselected = jnp.where(jnp.concatenate(ranks, axis=0) < min(SLC_TOPK, ns), 0.0, 1.0)
    tag_rows = [jnp.zeros((HEAD_DIM, tq), F32), unselected]
    if ns < LANES - HEAD_DIM:
        tag_rows.append(jnp.zeros((LANES - HEAD_DIM - ns, tq), F32))
    tag = jnp.concatenate(tag_rows, axis=0).T.astype(MXU_DTYPE)
    lane = lax.broadcasted_iota(jnp.int32, (R, tq, LANES), 2)
    q_aug = jnp.where(lane < HEAD_DIM, q_pad, tag[None]).reshape(rows, LANES)

    def scores(c):
        return _dot_nt(ks_ref[pl.ds(pl.multiple_of(c * kc, kc), kc), :], q_aug)

    def causal_bias(c):
        kpos = c * kc + lax.broadcasted_iota(jnp.int32, (kc, tq), 0)
        return all_heads(jnp.where(kpos <= t_row, 0.0, NEG))

    def consume(s, c, m):
        m_new = jnp.maximum(m, jnp.max(s, axis=0, keepdims=True))
        p = jnp.exp2(s - m_new).astype(MXU_DTYPE)
        pv = _dot(vst_ref[:, pl.ds(pl.multiple_of(c * kc, kc), kc)], p)
        acc_ref[...] = jnp.exp2(m - m_new) * acc_ref[...] + pv
        return m_new

    n_pairs = q0 // (2 * kc)
    acc_ref[...] = jnp.zeros((2 * HEAD_DIM, rows), F32)
    sa_ref[...] = scores(0)
    o_win_pieces += [window_piece(h) for h in range(n_win // 2, n_win)]

    def pair_step(j, m):
        sb_ref[...] = scores(2 * j + 1)
        m = consume(sa_ref[...], 2 * j, m)
        sa_ref[...] = scores(2 * j + 2)
        return consume(sb_ref[...], 2 * j + 1, m)

    m = lax.fori_loop(0, n_pairs, pair_step, jnp.full((1, rows), NEG, F32))
    c_last = 2 * n_pairs
    m = consume(sa_ref[...] + causal_bias(c_last), c_last, m)

    @pl.when(q0 + tq > (c_last + 1) * kc)
    def _():
        consume(scores(c_last + 1) + causal_bias(c_last + 1), c_last + 1, m)

    acc = acc_ref[...]
    o_slc = acc[:HEAD_DIM] / acc[HEAD_DIM:]

    g = gate_ref[...]
    outs = []
    for r in range(R):
        c = N_GATES * r
        o_win = jnp.concatenate([head(piece, r, wq) for piece in o_win_pieces], axis=1)
        o = g[c:c + 1] * head(o_cmp, r, tq) + g[c + 1:c + 2] * head(o_slc, r, tq) + g[c + 2:c + 3] * o_win
        ms = jnp.mean(o * o, axis=0, keepdims=True)
        outs.append(o * lax.rsqrt(ms + 1e-6) * hg_ref[r * HEAD_DIM:(r + 1) * HEAD_DIM, :])
    o_ref[...] = jnp.concatenate(outs, axis=0).T.astype(o_ref.dtype)


def _nsa(q, cmp_kv, cmp_kv_t, ks, vst, kw, vwt, gates, agg_t, hg_attn, tq, kc, wq):
    B, _, T, _ = q.shape
    G, R = N_KV_GROUPS, Q_PER_KV
    ncp = cmp_kv.shape[2]
    ns = T // SLC_LEN
    assert T % (2 * kc) == 0 and kc % tq == 0 and tq % (2 * wq) == 0 and WINDOW % wq == 0 and T >= WINDOW + tq
    assert ns <= LANES - HEAD_DIM, "block tags must fit in the upper lanes of a key row"
    whole = lambda b, g, i: (b, g, 0, 0)
    hg_lanes = jnp.broadcast_to(hg_attn.reshape(G, R * HEAD_DIM, 1), (G, R * HEAD_DIM, tq))
    return pl.pallas_call(
        functools.partial(_nsa_kernel, tq=tq, kc=kc, wq=wq),
        grid=(B, G, T // tq),
        in_specs=[
            pl.BlockSpec((None, R, tq, LANES), lambda b, g, i: (b, g, i, 0)),
            pl.BlockSpec((None, None, ncp, HEAD_DIM), lambda b, g, i: (0, b * G + g, 0, 0)),
            pl.BlockSpec((None, None, HEAD_DIM, ncp), lambda b, g, i: (1, b * G + g, 0, 0)),
            pl.BlockSpec((None, None, T, LANES), whole),
            pl.BlockSpec((None, None, 2 * HEAD_DIM, T), whole),
            pl.BlockSpec((None, None, T, HEAD_DIM), whole),
            pl.BlockSpec((None, None, 2 * HEAD_DIM, T), whole),
            pl.BlockSpec((None, None, _GATE_ROWS, tq), lambda b, g, i: (b, g, 0, i)),
            pl.BlockSpec((ns, ncp), lambda b, g, i: (0, 0)),
            pl.BlockSpec((None, R * HEAD_DIM, tq), lambda b, g, i: (g, 0, 0)),
        ],
        out_specs=pl.BlockSpec((None, tq, R * HEAD_DIM), lambda b, g, i: (b, i, g)),
        out_shape=jax.ShapeDtypeStruct((B, T, ATTN_WIDTH), MXU_DTYPE),
        scratch_shapes=[pltpu.VMEM((kc, R * tq), F32), pltpu.VMEM((kc, R * tq), F32),
                        pltpu.VMEM((2 * HEAD_DIM, R * tq), F32)],
        compiler_params=pltpu.CompilerParams(
            dimension_semantics=("arbitrary", "arbitrary", "arbitrary"), vmem_limit_bytes=VMEM_LIMIT),
    )(q, cmp_kv, cmp_kv_t, ks, vst, kw, vwt, gates, agg_t, hg_lanes)


def _mix_ffn_kernel(conv_ref, attn_ref, x_ref, wout_ref, g1_ref, b1_ref,
                    wup_ref, cw_ref, cb_ref, wdn_ref, g_ref, b_ref, o_ref, tail_ref, ubuf, act_ref,
                    *, tiles_per_seq, fc):
    tm = x_ref.shape[0]

    @pl.when(pl.program_id(0) % tiles_per_seq == 0)
    def _():
        tail_ref[...] = jnp.zeros(tail_ref.shape, F32)

    mix = _dot(conv_ref[...], wout_ref[0:CONV_WIDTH, :]) + _dot(attn_ref[...], wout_ref[CONV_WIDTH:, :])
    x = _layer_norm(ALPHA * x_ref[...] + mix, g1_ref[...], b1_ref[...])
    xb = jnp.concatenate([tail_ref[...], x], axis=0).astype(MXU_DTYPE)
    tail_ref[...] = x[tm - SUBLANES:]

    def up_conv(col0, buf):
        buf[...] = _dot(xb, wup_ref[:, col0:col0 + fc])
        return (cw_ref[2:3, col0:col0 + fc] * buf[SUBLANES:SUBLANES + tm, :]
                + cw_ref[1:2, col0:col0 + fc] * buf[SUBLANES - 1:SUBLANES - 1 + tm, :]
                + cw_ref[0:1, col0:col0 + fc] * buf[SUBLANES - 2:SUBLANES - 2 + tm, :]
                + cb_ref[:, col0:col0 + fc])

    for c in range(D_FF // fc):
        val = up_conv(c * fc, ubuf.at[c % 2, 0])
        gate = up_conv(D_FF + c * fc, ubuf.at[c % 2, 1])
        act_ref[:, c * fc:(c + 1) * fc] = (gate * jax.nn.sigmoid(gate) * val).astype(act_ref.dtype)
    o_ref[...] = _layer_norm(ALPHA * x + _dot(act_ref[...], wdn_ref[...]), g_ref[...], b_ref[...])


def _mix_ffn(conv_n, attn_n, x, w_out, g1, b1, w_up, conv_w, conv_b, w_down, g, b, tm, tiles_per_seq, fc):
    N, D = x.shape
    row = lambda i: (i, 0)
    const = lambda i: (0, 0)
    return pl.pallas_call(
        functools.partial(_mix_ffn_kernel, tiles_per_seq=tiles_per_seq, fc=fc),
        grid=(N // tm,),
        in_specs=[
            pl.BlockSpec((tm, CONV_WIDTH), row), pl.BlockSpec((tm, ATTN_WIDTH), row), pl.BlockSpec((tm, D), row),
            pl.BlockSpec((D, D), const, pipeline_mode=pl.Buffered(1)),
            pl.BlockSpec((1, D), const), pl.BlockSpec((1, D), const),
            pl.BlockSpec((D, 2 * D_FF), const, pipeline_mode=pl.Buffered(1)),
            pl.BlockSpec((3, 2 * D_FF), const),
            pl.BlockSpec((1, 2 * D_FF), const),
            pl.BlockSpec((D_FF, D), const, pipeline_mode=pl.Buffered(1)),
            pl.BlockSpec((1, D), const), pl.BlockSpec((1, D), const),
        ],
        out_specs=pl.BlockSpec((tm, D), row),
        out_shape=jax.ShapeDtypeStruct((N, D), F32),
        scratch_shapes=[pltpu.VMEM((SUBLANES, D), F32), pltpu.VMEM((2, 2, tm + SUBLANES, fc), F32),
                        pltpu.VMEM((tm, D_FF), MXU_DTYPE)],
        compiler_params=pltpu.CompilerParams(
            dimension_semantics=("arbitrary",), vmem_limit_bytes=VMEM_LIMIT),
    )(conv_n, attn_n, x, w_out, g1, b1, w_up, conv_w, conv_b, w_down, g, b)


def _rope_tables_kernel(freq_hi_ref, freq_lo_ref, sign_ref, cos_ref, rot_ref):
    tr = cos_ref.shape[0]
    pos = (pl.program_id(0) * tr + lax.broadcasted_iota(jnp.int32, (tr, LANES), 0)).astype(F32)
    ang = pos * freq_hi_ref[...] + pos * freq_lo_ref[...]
    cos_ref[...] = jnp.cos(ang)
    rot_ref[...] = jnp.sin(ang) * sign_ref[...]


def _rope_lane_tables(T, tr):
    half = ROPE_DIM // 2
    inv_freq = ROPE_THETA ** (-(np.arange(half, dtype=np.float64) * 2.0 / ROPE_DIM))
    lane = np.arange(LANES) % HEAD_DIM
    freq = np.where(lane < ROPE_DIM, inv_freq[lane % half], 0.0)
    freq_hi = freq.astype(np.float32)
    freq_lo = (freq - freq_hi.astype(np.float64)).astype(np.float32)
    sign = np.where(lane < half, -1.0, np.where(lane < ROPE_DIM, 1.0, 0.0)).astype(np.float32)
    lanes_spec = pl.BlockSpec((1, LANES), lambda i: (0, 0))
    rows_spec = pl.BlockSpec((tr, LANES), lambda i: (i, 0))
    return pl.pallas_call(
        _rope_tables_kernel,
        grid=(T // tr,),
        in_specs=[lanes_spec, lanes_spec, lanes_spec],
        out_specs=[rows_spec, rows_spec],
        out_shape=[jax.ShapeDtypeStruct((T, LANES), F32)] * 2,
    )(jnp.asarray(freq_hi)[None], jnp.asarray(freq_lo)[None], jnp.asarray(sign)[None])


def _group_mean_matrix():
    grp = np.arange(MXU_COLS) // HEAD_DIM
    return jnp.asarray((grp[:, None] == grp[None, :]) / HEAD_DIM, dtype=MXU_DTYPE)


def _slc_aggregation_t(T):
    nc = T // CMP_STRIDE - CMP_LEN // CMP_STRIDE + 1
    ns = T // SLC_LEN
    sc = np.arange(nc)[None, :] * CMP_STRIDE
    ss = np.arange(ns)[:, None] * SLC_LEN
    ov = np.clip(np.minimum(sc + CMP_LEN, ss + SLC_LEN) - np.maximum(sc, ss), 0, None) / CMP_LEN
    ov = np.pad(ov, ((0, 0), (0, T // CMP_STRIDE - nc)))
    return jnp.asarray(ov, dtype=MXU_DTYPE)


def _split_in_weights(w):
    w = w.astype(MXU_DTYPE)
    plain = jnp.concatenate([w[..., :_W_VS], w[..., _W_KW:_W_VW]], axis=-1)
    half = ROPE_DIM // 2
    zeros = jnp.zeros(w.shape[:-1] + (HEAD_DIM - ROPE_DIM,), w.dtype)
    partners = []
    for c0, width in ((_OFF_Q, ATTN_WIDTH), (_OFF_KSW, 2 * KV_WIDTH), (_OFF_KC, KV_WIDTH)):
        for h0 in range(c0, c0 + width, HEAD_DIM):
            partners += [plain[..., h0 + half:h0 + ROPE_DIM], plain[..., h0:h0 + half], zeros]
    w_main = jnp.concatenate([plain] + partners, axis=-1)
    per_group = Q_PER_KV * N_GATES
    pad = jnp.zeros(w.shape[:-1] + (_GATE_ROWS - per_group,), w.dtype)
    cols = [w[..., _W_VS:_W_KW], w[..., _W_VW:_W_GATE]]
    for g in range(N_KV_GROUPS):
        cols += [w[..., _W_GATE + g * per_group:_W_GATE + (g + 1) * per_group], pad]
    return w_main, jnp.swapaxes(jnp.concatenate(cols, axis=-1), -1, -2)


def kernel(x, w_in, short_conv_w, cmp_pe, cmp_w1, cmp_b1, cmp_w2, cmp_b2, head_norm_g, w_out, ln1_g, ln1_b,
           w_up, ffn_conv_w, ffn_conv_b, w_down, ln2_g, ln2_b):
    B, T, D = x.shape
    tm = min(512, T)
    tm_in = min(1024, T)
    cos_t, rot_t = _rope_lane_tables(T, tm_in)
    gmean = _group_mean_matrix()
    agg_t = _slc_aggregation_t(T)
    w_main, w_t = _split_in_weights(w_in)
    cmp_w1_b, cmp_w2_b = cmp_w1.astype(MXU_DTYPE), cmp_w2.astype(MXU_DTYPE)
    w_out_b, w_up_b, w_down_b = w_out.astype(MXU_DTYPE), w_up.astype(MXU_DTYPE), w_down.astype(MXU_DTYPE)
    cmp_pe_flat = cmp_pe.reshape(DEPTH, 2, 1, CMP_LEN * HEAD_DIM)
    for i in range(DEPTH):
        hg = head_norm_g[i]
        conv_n, q, cv, ks, kw, vst, vwt, gates = _inproj(
            x, w_main[i], w_t[i], short_conv_w[i], cos_t, rot_t, hg[None, :CONV_WIDTH], gmean, tm_in)
        cmp_kv, cmp_kv_t = _compress(
            cv.reshape(2, B * N_KV_GROUPS, T, HEAD_DIM), cmp_pe_flat[i], cmp_w1_b[i], cmp_b1[i], cmp_w2_b[i],
            cmp_b2[i])
        attn_n = _nsa(q, cmp_kv, cmp_kv_t, ks, vst, kw, vwt, gates, agg_t, hg[CONV_WIDTH:], tq=512, kc=512,
                      wq=128)
        x = _mix_ffn(conv_n.reshape(B * T, CONV_WIDTH), attn_n.reshape(B * T, ATTN_WIDTH), x.reshape(B * T, D),
                     w_out_b[i], ln1_g[i][None], ln1_b[i][None], w_up_b[i], ffn_conv_w[i], ffn_conv_b[i][None],
                     w_down_b[i], ln2_g[i][None], ln2_b[i][None], tm, T // tm, 256).reshape(B, T, D)
    return x
```

```python
import functools

import numpy as np
import jax
import jax.numpy as jnp
from jax import lax
from jax.experimental import pallas as pl
from jax.experimental.pallas import tpu as pltpu

D_MODEL = 1024
DEPTH = 2
HEAD_DIM = 64
CONV_WIDTH = 512
ATTN_WIDTH = 512
N_Q_HEADS = 8
N_KV_GROUPS = 2
Q_PER_KV = 4
KV_WIDTH = N_KV_GROUPS * HEAD_DIM
N_GATES = 3
CMP_LEN = 32
CMP_STRIDE = 16
CMP_HIDDEN = 256
SLC_LEN = 64
SLC_TOPK = 16
WINDOW = 512
ROPE_THETA = 500000.0
ROPE_DIM = 16
D_FF = 2816
ALPHA = (2.0 * DEPTH) ** 0.25
NEG = -1e30
BIG = 1e9

F32 = jnp.float32
MXU_DTYPE = jnp.bfloat16
LANES = 128
SUBLANES = 8
MXU_COLS = 256
VMEM_LIMIT = 56 * 1024 * 1024

_W_VS, _W_KW, _W_VW, _W_GATE = 2432, 2560, 2688, 2816
_OFF_B, _OFF_C, _OFF_H, _OFF_Q, _OFF_KC, _OFF_KSW = 0, 512, 1024, 1536, 2048, 2304
_OFF_PQ, _OFF_PKSW, _OFF_PKC = 2560, 3072, 3328
_MAIN_COLS = 3456
LOG2E = 1.4426950408889634
_GATE_ROWS = 16
_ROW_VS, _ROW_VW, _ROW_GATE = 0, KV_WIDTH, 2 * KV_WIDTH
_T_ROWS = 2 * KV_WIDTH + N_KV_GROUPS * _GATE_ROWS

_NT = (((1,), (1,)), ((), ()))


def _dot(a, b):
    return jnp.dot(a, b, preferred_element_type=F32)


def _dot_nt(a, b):
    return lax.dot_general(a, b, _NT, preferred_element_type=F32)


def _layer_norm(y, g, b):
    mu = jnp.mean(y, axis=-1, keepdims=True)
    d = y - mu
    var = jnp.mean(d * d, axis=-1, keepdims=True)
    return d * lax.rsqrt(var + 1e-5) * g + b


def _inproj_kernel(x_ref, w_ref, wt_ref, cw_ref, cos_ref, rot_ref, hg_ref, gmean_ref,
                   conv_ref, q_ref, cv_ref, ks_ref, kw_ref, vst_ref, vwt_ref, gate_ref, zbuf):
    tm = x_ref.shape[0]
    xb = x_ref[...].astype(MXU_DTYPE)

    def mm(c0, width):
        return _dot(xb, w_ref[:, c0:c0 + width])

    @pl.when(pl.program_id(1) == 0)
    def _():
        zbuf[0:SUBLANES, :] = jnp.zeros((SUBLANES, CONV_WIDTH), F32)

    def conv_block(c0):
        cs = slice(c0, c0 + MXU_COLS)
        z = mm(_OFF_C + c0, MXU_COLS) * mm(_OFF_H + c0, MXU_COLS)
        zbuf[SUBLANES:SUBLANES + tm, cs] = z
        conv = (cw_ref[2:3, cs] * z + cw_ref[1:2, cs] * zbuf[SUBLANES - 1:SUBLANES - 1 + tm, cs]
                + cw_ref[0:1, cs] * zbuf[SUBLANES - 2:SUBLANES - 2 + tm, cs])
        zbuf[0:SUBLANES, cs] = zbuf[tm:tm + SUBLANES, cs]
        y = mm(_OFF_B + c0, MXU_COLS) * conv
        y2 = y * y
        y2_hi = y2.astype(MXU_DTYPE)
        y2_lo = (y2 - y2_hi.astype(F32)).astype(MXU_DTYPE)
        ms = _dot(y2_hi, gmean_ref[...]) + _dot(y2_lo, gmean_ref[...])
        conv_ref[:, cs] = (y * lax.rsqrt(ms + 1e-6) * hg_ref[:, cs]).astype(conv_ref.dtype)

    def rope(v, partner):
        return v * cos_ref[...] + partner * rot_ref[...]

    def split_heads(v, ref, lead):
        for g in range(N_KV_GROUPS):
            ref[lead + (g,)] = v[:, g * HEAD_DIM:(g + 1) * HEAD_DIM].astype(ref.dtype)

    lane = lax.broadcasted_iota(jnp.int32, (tm, LANES), 1)
    low = lane < HEAD_DIM

    def head_pair(v):
        return v, pltpu.roll(v, HEAD_DIM, 1)

    def q_block(c):
        r4 = mm(_OFF_Q + c * MXU_COLS, MXU_COLS)
        p4 = mm(_OFF_PQ + c * MXU_COLS, MXU_COLS)
        for i in range(MXU_COLS // LANES):
            sl = slice(i * LANES, (i + 1) * LANES)
            r = rope(r4[:, sl], p4[:, sl]) * (HEAD_DIM ** -0.5 * LOG2E)
            for j, h in enumerate(head_pair(r)):
                q_ref[4 * c + 2 * i + j] = jnp.where(low, h, 0.0).astype(q_ref.dtype)

    conv_block(0)
    q_block(0)
    conv_block(MXU_COLS)
    q_block(1)
    kvc = mm(_OFF_KC, 2 * KV_WIDTH)
    split_heads(rope(kvc[:, :KV_WIDTH], mm(_OFF_PKC, KV_WIDTH)), cv_ref, (0,))
    split_heads(kvc[:, KV_WIDTH:], cv_ref, (1,))
    ksw = mm(_OFF_KSW, 2 * KV_WIDTH)
    pksw = mm(_OFF_PKSW, 2 * KV_WIDTH)
    pos = pl.program_id(1) * tm + lax.broadcasted_iota(jnp.int32, (tm, LANES), 0)
    block_tag = jnp.where(pos // SLC_LEN == lane - HEAD_DIM, NEG, 0.0)
    for g, h in enumerate(head_pair(rope(ksw[:, :KV_WIDTH], pksw[:, :KV_WIDTH]))):
        ks_ref[g] = jnp.where(low, h, block_tag).astype(ks_ref.dtype)
    split_heads(rope(ksw[:, KV_WIDTH:], pksw[:, KV_WIDTH:]), kw_ref, ())

    vt = _dot_nt(wt_ref[...], xb)
    ones = jnp.ones((HEAD_DIM, tm), vst_ref.dtype)
    for g in range(N_KV_GROUPS):
        for ref, row0 in ((vst_ref, _ROW_VS), (vwt_ref, _ROW_VW)):
            ref[g, 0:HEAD_DIM, :] = vt[row0 + g * HEAD_DIM:row0 + (g + 1) * HEAD_DIM].astype(ref.dtype)
            ref[g, HEAD_DIM:2 * HEAD_DIM, :] = ones
        gate_ref[g] = jax.nn.sigmoid(vt[_ROW_GATE + g * _GATE_ROWS:_ROW_GATE + (g + 1) * _GATE_ROWS])


def _inproj(x, w_main, w_t, conv_w, cos_t, rot_t, hg_conv, gmean, tm):
    B, T, D = x.shape
    G = N_KV_GROUPS
    kv_shape = jax.ShapeDtypeStruct((B, G, T, HEAD_DIM), MXU_DTYPE)
    kv_spec = pl.BlockSpec((None, G, tm, HEAD_DIM), lambda b, i: (b, 0, i, 0))
    vt_shape = jax.ShapeDtypeStruct((B, G, 2 * HEAD_DIM, T), MXU_DTYPE)
    vt_spec = pl.BlockSpec((None, G, 2 * HEAD_DIM, tm), lambda b, i: (b, 0, 0, i))
    tab_spec = pl.BlockSpec((tm, LANES), lambda b, i: (i, 0))
    const2 = lambda b, i: (0, 0)
    return pl.pallas_call(
        _inproj_kernel,
        grid=(B, T // tm),
        in_specs=[
            pl.BlockSpec((None, tm, D), lambda b, i: (b, i, 0)),
            pl.BlockSpec((D, _MAIN_COLS), const2),
            pl.BlockSpec((_T_ROWS, D), const2),
            pl.BlockSpec((3, CONV_WIDTH), const2),
            tab_spec, tab_spec,
            pl.BlockSpec((1, CONV_WIDTH), const2),
            pl.BlockSpec((MXU_COLS, MXU_COLS), const2),
        ],
        out_specs=[
            pl.BlockSpec((None, tm, CONV_WIDTH), lambda b, i: (b, i, 0)),
            pl.BlockSpec((None, N_Q_HEADS, tm, LANES), lambda b, i: (b, 0, i, 0)),
            pl.BlockSpec((2, None, G, tm, HEAD_DIM), lambda b, i: (0, b, 0, i, 0)),
            pl.BlockSpec((None, G, tm, LANES), lambda b, i: (b, 0, i, 0)),
            kv_spec, vt_spec, vt_spec,
            pl.BlockSpec((None, G, _GATE_ROWS, tm), lambda b, i: (b, 0, 0, i)),
        ],
        out_shape=[
            jax.ShapeDtypeStruct((B, T, CONV_WIDTH), MXU_DTYPE),
            jax.ShapeDtypeStruct((B, N_Q_HEADS, T, LANES), MXU_DTYPE),
            jax.ShapeDtypeStruct((2, B, G, T, HEAD_DIM), F32),
            jax.ShapeDtypeStruct((B, G, T, LANES), MXU_DTYPE),
            kv_shape, vt_shape, vt_shape,
            jax.ShapeDtypeStruct((B, G, _GATE_ROWS, T), F32),
        ],
        scratch_shapes=[pltpu.VMEM((tm + 2 * SUBLANES, CONV_WIDTH), F32)],
        compiler_params=pltpu.CompilerParams(
            dimension_semantics=("arbitrary", "arbitrary"), vmem_limit_bytes=VMEM_LIMIT),
    )(x, w_main, w_t, conv_w, cos_t, rot_t, hg_conv, gmean)


def _compress_kernel(a_ref, pe_ref, w1_ref, b1_ref, w2_ref, b2_ref, w2t_ref, b2t_ref, out_ref, outt_ref,
                     sub_ref, bbuf):
    nsub = sub_ref.shape[0]
    half = CMP_STRIDE * HEAD_DIM
    for j in range(CMP_STRIDE):
        sub_ref[:, j * HEAD_DIM:(j + 1) * HEAD_DIM] = (
            a_ref[pl.ds(j, nsub, stride=CMP_STRIDE), :].astype(sub_ref.dtype))
    sub = sub_ref[...]
    top = _dot(sub, w1_ref[0:half, :])
    bbuf[0:nsub, :] = _dot(sub, w1_ref[half:2 * half, :])
    bbuf[nsub:nsub + SUBLANES, :] = jnp.zeros((SUBLANES, CMP_HIDDEN), F32)
    pe_rows = jnp.broadcast_to(pe_ref[...], (SUBLANES, 2 * half)).astype(MXU_DTYPE)
    const = _dot(pe_rows, w1_ref[...])[0:1, :] + b1_ref[...]
    h = top + bbuf[1:nsub + 1, :] + const
    act = jax.nn.gelu(h).astype(MXU_DTYPE)
    out_ref[...] = (_dot(act, w2_ref[...]) + b2_ref[...]).astype(out_ref.dtype)
    outt_ref[...] = (_dot_nt(w2t_ref[...], act) + b2t_ref[...]).astype(outt_ref.dtype)


def _compress(a, pe, w1, b1, w2, b2):
    _, BG, T, _ = a.shape
    nsub = T // CMP_STRIDE
    width = CMP_STRIDE * HEAD_DIM
    sel = lambda k, n: (k, 0, 0)
    per = lambda k, n: (k, n, 0, 0)
    return pl.pallas_call(
        _compress_kernel,
        grid=(2, BG),
        in_specs=[
            pl.BlockSpec((None, None, T, HEAD_DIM), per),
            pl.BlockSpec((None, 1, 2 * width), sel),
            pl.BlockSpec((None, 2 * width, CMP_HIDDEN), sel),
            pl.BlockSpec((None, 1, CMP_HIDDEN), sel),
            pl.BlockSpec((None, CMP_HIDDEN, HEAD_DIM), sel),
            pl.BlockSpec((None, 1, HEAD_DIM), sel),
            pl.BlockSpec((None, HEAD_DIM, CMP_HIDDEN), sel),
            pl.BlockSpec((None, HEAD_DIM, 1), sel),
        ],
        out_specs=[pl.BlockSpec((None, None, nsub, HEAD_DIM), per),
                   pl.BlockSpec((None, None, HEAD_DIM, nsub), per)],
        out_shape=[jax.ShapeDtypeStruct((2, BG, nsub, HEAD_DIM), MXU_DTYPE),
                   jax.ShapeDtypeStruct((2, BG, HEAD_DIM, nsub), MXU_DTYPE)],
        scratch_shapes=[pltpu.VMEM((nsub, width), MXU_DTYPE), pltpu.VMEM((nsub + SUBLANES, CMP_HIDDEN), F32)],
        compiler_params=pltpu.CompilerParams(
            dimension_semantics=("arbitrary", "arbitrary"), vmem_limit_bytes=VMEM_LIMIT),
    )(a, pe, w1, b1[:, None, :], w2, b2[:, None, :], jnp.swapaxes(w2, 1, 2), b2[:, :, None])


def _rank_accumulate(score, ranks, i0, i1):
    tq = score.shape[1]
    local = lax.broadcasted_iota(jnp.int32, (SUBLANES, tq), 0)
    ranks = list(ranks)
    for i in range(i0, i1):
        row = jnp.broadcast_to(score[i:i + 1, :], (SUBLANES, tq))
        for j in range(len(ranks)):
            slab = score[SUBLANES * j:SUBLANES * (j + 1), :]
            ge = jnp.where(row >= slab, 1.0, 0.0)
            gt = jnp.where(row > slab, 1.0, 0.0)
            if i < SUBLANES * j:
                ahead = ge
            elif i >= SUBLANES * (j + 1):
                ahead = gt
            else:
                ahead = jnp.where(local > i - SUBLANES * j, ge, gt)
            ranks[j] = ranks[j] + ahead
    return ranks


def _nsa_kernel(q_ref, kcmp_ref, vcmpt_ref, ks_ref, vst_ref, kw_ref, vwt_ref, gate_ref, aggt_ref, hg_ref,
                o_ref, sa_ref, sb_ref, acc_ref, *, tq, kc, wq):
    R = Q_PER_KV
    rows = R * tq
    T = ks_ref.shape[0]
    ns = T // SLC_LEN
    ncp = kcmp_ref.shape[0]
    q0 = pl.program_id(2) * tq
    t_row = q0 + lax.broadcasted_iota(jnp.int32, (1, tq), 1)
    q_pad = q_ref[...]
    q64 = q_pad.reshape(rows, LANES)[:, :HEAD_DIM]

    def all_heads(a):
        return jnp.concatenate([a] * R, axis=1)

    def head(a, r, width):
        return a[:, r * width:(r + 1) * width]

    n_win = tq // wq
    wl = WINDOW + wq

    def window_piece(h):
        t_piece = t_row[:, h * wq:(h + 1) * wq]
        ws = pl.multiple_of(jnp.maximum(q0 + h * wq - WINDOW, 0), wq)
        wpos = ws + lax.broadcasted_iota(jnp.int32, (wl, wq), 0)
        wbias = jnp.where((wpos <= t_piece) & (wpos > t_piece - WINDOW), 0.0, NEG)
        qh = q64.reshape(R, tq, HEAD_DIM)[:, h * wq:(h + 1) * wq, :].reshape(R * wq, HEAD_DIM)
        s = _dot_nt(kw_ref[pl.ds(ws, wl), :], qh) + jnp.concatenate([wbias] * R, axis=1)
        p = jnp.exp2(s - jnp.max(s, axis=0, keepdims=True)).astype(MXU_DTYPE)
        ow = _dot(vwt_ref[:, pl.ds(ws, wl)], p)
        return ow[:HEAD_DIM] / ow[HEAD_DIM:]

    cmp_end = lax.broadcasted_iota(jnp.int32, (ncp, tq), 0) * CMP_STRIDE + (CMP_LEN - 1)
    cbias = jnp.where(cmp_end <= t_row, 0.0, NEG)
    s = _dot_nt(kcmp_ref[...], q64) + all_heads(cbias)
    p = jnp.exp2(s - jnp.max(s, axis=0, keepdims=True))
    l = jnp.sum(p, axis=0, keepdims=True)
    p = p * (all_heads(jnp.where(t_row >= CMP_LEN - 1, 1.0, 0.0)) / l)
    o_cmp = _dot(vcmpt_ref[...], p.astype(MXU_DTYPE))

    psum = head(p, 0, tq) + head(p, 1, tq) + head(p, 2, tq) + head(p, 3, tq)
    p_hi = psum.astype(MXU_DTYPE)
    p_lo = (psum - p_hi.astype(F32)).astype(MXU_DTYPE)
    imp = _dot(aggt_ref[...], p_hi) + _dot(aggt_ref[...], p_lo)
    blk = lax.broadcasted_iota(jnp.int32, (ns, tq), 0)
    tb = (q0 + lax.broadcasted_iota(jnp.int32, (ns, tq), 1)) // SLC_LEN
    forced = (blk == 0) | (blk == tb) | (blk == tb - 1)
    score = jnp.where(forced, BIG, imp)
    score = jnp.where(blk <= tb, score, -BIG)
    ranks = [jnp.zeros((SUBLANES, tq), F32) for _ in range(ns // SUBLANES)]
    ranks = _rank_accumulate(score, ranks, 0, ns // 2)
    o_win_pieces = [window_piece(h) for h in range(n_win // 2)]
    ranks = _rank_accumulate(score, ranks, ns // 2, ns)
    unselected = jnp.where(jnp.concatenate(ranks, axis=0) < min(SLC_TOPK, ns), 0.0, 1.0)
    tag_rows = [jnp.zeros((HEAD_DIM, tq), F32), unselected]
    if ns < LANES - HEAD_DIM:
        tag_rows.append(jnp.zeros((LANES - HEAD_DIM - ns, tq), F32))
    tag = jnp.concatenate(tag_rows, axis=0).T.astype(MXU_DTYPE)
    lane = lax.broadcasted_iota(jnp.int32, (R, tq, LANES), 2)
    q_aug = jnp.where(lane < HEAD_DIM, q_pad, tag[None]).reshape(rows, LANES)

    def scores(c):
        return _dot_nt(ks_ref[pl.ds(pl.multiple_of(c * kc, kc), kc), :], q_aug)

    def causal_bias(c):
        kpos = c * kc + lax.broadcasted_iota(jnp.int32, (kc, tq), 0)
        return all_heads(jnp.where(kpos <= t_row, 0.0, NEG))

    def consume(s, c, m):
        m_new = jnp.maximum(m, jnp.max(s, axis=0, keepdims=True))
        p = jnp.exp2(s - m_new).astype(MXU_DTYPE)
        pv = _dot(vst_ref[:, pl.ds(pl.multiple_of(c * kc, kc), kc)], p)
        acc_ref[...] = jnp.exp2(m - m_new) * acc_ref[...] + pv
        return m_new

    n_pairs = q0 // (2 * kc)
    acc_ref[...] = jnp.zeros((2 * HEAD_DIM, rows), F32)
    sa_ref[...] = scores(0)
    o_win_pieces += [window_piece(h) for h in range(n_win // 2, n_win)]

    def pair_step(j, m):
        sb_ref[...] = scores(2 * j + 1)
        m = consume(sa_ref[...], 2 * j, m)
        sa_ref[...] = scores(2 * j + 2)
        return consume(sb_ref[...], 2 * j + 1, m)

    m = lax.fori_loop(0, n_pairs, pair_step, jnp.full((1, rows), NEG, F32))
    c_last = 2 * n_pairs
    m = consume(sa_ref[...] + causal_bias(c_last), c_last, m)

    @pl.when(q0 + tq > (c_last + 1) * kc)
    def _():
        consume(scores(c_last + 1) + causal_bias(c_last + 1), c_last + 1, m)

    acc = acc_ref[...]
    o_slc = acc[:HEAD_DIM] / acc[HEAD_DIM:]

    g = gate_ref[...]
    outs = []
    for r in range(R):
        c = N_GATES * r
        o_win = jnp.concatenate([head(piece, r, wq) for piece in o_win_pieces], axis=1)
        o = g[c:c + 1] * head(o_cmp, r, tq) + g[c + 1:c + 2] * head(o_slc, r, tq) + g[c + 2:c + 3] * o_win
        ms = jnp.mean(o * o, axis=0, keepdims=True)
        outs.append(o * lax.rsqrt(ms + 1e-6) * hg_ref[r * HEAD_DIM:(r + 1) * HEAD_DIM, :])
    o_ref[...] = jnp.concatenate(outs, axis=0).T.astype(o_ref.dtype)


def _nsa(q, cmp_kv, cmp_kv_t, ks, vst, kw, vwt, gates, agg_t, hg_attn, tq, kc, wq):
    B, _, T, _ = q.shape
    G, R = N_KV_GROUPS, Q_PER_KV
    ncp = cmp_kv.shape[2]
    ns = T // SLC_LEN
    assert T % (2 * kc) == 0 and kc % tq == 0 and tq % (2 * wq) == 0 and WINDOW % wq == 0 and T >= WINDOW + tq
    assert ns <= LANES - HEAD_DIM, "block tags must fit in the upper lanes of a key row"
    whole = lambda b, g, i: (b, g, 0, 0)
    hg_lanes = jnp.broadcast_to(hg_attn.reshape(G, R * HEAD_DIM, 1), (G, R * HEAD_DIM, tq))
    return pl.pallas_call(
        functools.partial(_nsa_kernel, tq=tq, kc=kc, wq=wq),
        grid=(B, G, T // tq),
        in_specs=[
            pl.BlockSpec((None, R, tq, LANES), lambda b, g, i: (b, g, i, 0)),
            pl.BlockSpec((None, None, ncp, HEAD_DIM), lambda b, g, i: (0, b * G + g, 0, 0)),
            pl.BlockSpec((None, None, HEAD_DIM, ncp), lambda b, g, i: (1, b * G + g, 0, 0)),
            pl.BlockSpec((None, None, T, LANES), whole),
            pl.BlockSpec((None, None, 2 * HEAD_DIM, T), whole),
            pl.BlockSpec((None, None, T, HEAD_DIM), whole),
            pl.BlockSpec((None, None, 2 * HEAD_DIM, T), whole),
            pl.BlockSpec((None, None, _GATE_ROWS, tq), lambda b, g, i: (b, g, 0, i)),
            pl.BlockSpec((ns, ncp), lambda b, g, i: (0, 0)),
            pl.BlockSpec((None, R * HEAD_DIM, tq), lambda b, g, i: (g, 0, 0)),
        ],
        out_specs=pl.BlockSpec((None, tq, R * HEAD_DIM), lambda b, g, i: (b, i, g)),
        out_shape=jax.ShapeDtypeStruct((B, T, ATTN_WIDTH), MXU_DTYPE),
        scratch_shapes=[pltpu.VMEM((kc, R * tq), F32), pltpu.VMEM((kc, R * tq), F32),
                        pltpu.VMEM((2 * HEAD_DIM, R * tq), F32)],
        compiler_params=pltpu.CompilerParams(
            dimension_semantics=("arbitrary", "arbitrary", "arbitrary"), vmem_limit_bytes=VMEM_LIMIT),
    )(q, cmp_kv, cmp_kv_t, ks, vst, kw, vwt, gates, agg_t, hg_lanes)


def _mix_ffn_kernel(conv_ref, attn_ref, x_ref, wout_ref, g1_ref, b1_ref,
                    wup_ref, cw_ref, cb_ref, wdn_ref, g_ref, b_ref, o_ref, tail_ref, ubuf, act_ref,
                    *, tiles_per_seq, fc):
    tm = x_ref.shape[0]

    @pl.when(pl.program_id(0) % tiles_per_seq == 0)
    def _():
        tail_ref[...] = jnp.zeros(tail_ref.shape, F32)

    mix = _dot(conv_ref[...], wout_ref[0:CONV_WIDTH, :]) + _dot(attn_ref[...], wout_ref[CONV_WIDTH:, :])
    x = _layer_norm(ALPHA * x_ref[...] + mix, g1_ref[...], b1_ref[...])
    xb = jnp.concatenate([tail_ref[...], x], axis=0).astype(MXU_DTYPE)
    tail_ref[...] = x[tm - SUBLANES:]

    def up_conv(col0, buf):
        buf[...] = _dot(xb, wup_ref[:, col0:col0 + fc])
        return (cw_ref[2:3, col0:col0 + fc] * buf[SUBLANES:SUBLANES + tm, :]
                + cw_ref[1:2, col0:col0 + fc] * buf[SUBLANES - 1:SUBLANES - 1 + tm, :]
                + cw_ref[0:1, col0:col0 + fc] * buf[SUBLANES - 2:SUBLANES - 2 + tm, :]
                + cb_ref[:, col0:col0 + fc])

    for c in range(D_FF // fc):
        val = up_conv(c * fc, ubuf.at[c % 2, 0])
        gate = up_conv(D_FF + c * fc, ubuf.at[c % 2, 1])
        act_ref[:, c * fc:(c + 1) * fc] = (gate * jax.nn.sigmoid(gate) * val).astype(act_ref.dtype)
    o_ref[...] = _layer_norm(ALPHA * x + _dot(act_ref[...], wdn_ref[...]), g_ref[...], b_ref[...])


def _mix_ffn(conv_n, attn_n, x, w_out, g1, b1, w_up, conv_w, conv_b, w_down, g, b, tm, tiles_per_seq, fc):
    N, D = x.shape
    row = lambda i: (i, 0)
    const = lambda i: (0, 0)
    return pl.pallas_call(
        functools.partial(_mix_ffn_kernel, tiles_per_seq=tiles_per_seq, fc=fc),
        grid=(N // tm,),
        in_specs=[
            pl.BlockSpec((tm, CONV_WIDTH), row), pl.BlockSpec((tm, ATTN_WIDTH), row), pl.BlockSpec((tm, D), row),
            pl.BlockSpec((D, D), const, pipeline_mode=pl.Buffered(1)),
            pl.BlockSpec((1, D), const), pl.BlockSpec((1, D), const),
            pl.BlockSpec((D, 2 * D_FF), const, pipeline_mode=pl.Buffered(1)),
            pl.BlockSpec((3, 2 * D_FF), const),
            pl.BlockSpec((1, 2 * D_FF), const),
            pl.BlockSpec((D_FF, D), const, pipeline_mode=pl.Buffered(1)),
            pl.BlockSpec((1, D), const), pl.BlockSpec((1, D), const),
        ],
        out_specs=pl.BlockSpec((tm, D), row),
        out_shape=jax.ShapeDtypeStruct((N, D), F32),
        scratch_shapes=[pltpu.VMEM((SUBLANES, D), F32), pltpu.VMEM((2, 2, tm + SUBLANES, fc), F32),
                        pltpu.VMEM((tm, D_FF), MXU_DTYPE)],
        compiler_params=pltpu.CompilerParams(
            dimension_semantics=("arbitrary",), vmem_limit_bytes=VMEM_LIMIT),
    )(conv_n, attn_n, x, w_out, g1, b1, w_up, conv_w, conv_b, w_down, g, b)


def _rope_tables_kernel(freq_hi_ref, freq_lo_ref, sign_ref, cos_ref, rot_ref):
    tr = cos_ref.shape[0]
    pos = (pl.program_id(0) * tr + lax.broadcasted_iota(jnp.int32, (tr, LANES), 0)).astype(F32)
    ang = pos * freq_hi_ref[...] + pos * freq_lo_ref[...]
    cos_ref[...] = jnp.cos(ang)
    rot_ref[...] = jnp.sin(ang) * sign_ref[...]


def _rope_lane_tables(T, tr):
    half = ROPE_DIM // 2
    inv_freq = ROPE_THETA ** (-(np.arange(half, dtype=np.float64) * 2.0 / ROPE_DIM))
    lane = np.arange(LANES) % HEAD_DIM
    freq = np.where(lane < ROPE_DIM, inv_freq[lane % half], 0.0)
    freq_hi = freq.astype(np.float32)
    freq_lo = (freq - freq_hi.astype(np.float64)).astype(np.float32)
    sign = np.where(lane < half, -1.0, np.where(lane < ROPE_DIM, 1.0, 0.0)).astype(np.float32)
    lanes_spec = pl.BlockSpec((1, LANES), lambda i: (0, 0))
    rows_spec = pl.BlockSpec((tr, LANES), lambda i: (i, 0))
    return pl.pallas_call(
        _rope_tables_kernel,
        grid=(T // tr,),
        in_specs=[lanes_spec, lanes_spec, lanes_spec],
        out_specs=[rows_spec, rows_spec],
        out_shape=[jax.ShapeDtypeStruct((T, LANES), F32)] * 2,
    )(jnp.asarray(freq_hi)[None], jnp.asarray(freq_lo)[None], jnp.asarray(sign)[None])


def _group_mean_matrix():
    grp = np.arange(MXU_COLS) // HEAD_DIM
    return jnp.asarray((grp[:, None] == grp[None, :]) / HEAD_DIM, dtype=MXU_DTYPE)


def _slc_aggregation_t(T):
    nc = T // CMP_STRIDE - CMP_LEN // CMP_STRIDE + 1
    ns = T // SLC_LEN
    sc = np.arange(nc)[None, :] * CMP_STRIDE
    ss = np.arange(ns)[:, None] * SLC_LEN
    ov = np.clip(np.minimum(sc + CMP_LEN, ss + SLC_LEN) - np.maximum(sc, ss), 0, None) / CMP_LEN
    ov = np.pad(ov, ((0, 0), (0, T // CMP_STRIDE - nc)))
    return jnp.asarray(ov, dtype=MXU_DTYPE)


def _split_in_weights(w):
    w = w.astype(MXU_DTYPE)
    plain = jnp.concatenate([w[..., :_W_VS], w[..., _W_KW:_W_VW]], axis=-1)
    half = ROPE_DIM // 2
    partners = []
    for c0, width in ((_OFF_Q, ATTN_WIDTH), (_OFF_KSW, 2 * KV_WIDTH), (_OFF_KC, KV_WIDTH)):
        heads = plain[..., c0:c0 + width].reshape(w.shape[:-1] + (width // HEAD_DIM, HEAD_DIM))
        swapped = jnp.concatenate([heads[..., half:ROPE_DIM], heads[..., :half],
                                   jnp.zeros_like(heads[..., ROPE_DIM:])], axis=-1)
        partners.append(swapped.reshape(w.shape[:-1] + (width,)))
    w_main = jnp.concatenate([plain] + partners, axis=-1)
    per_group = Q_PER_KV * N_GATES
    pad = jnp.zeros(w.shape[:-1] + (_GATE_ROWS - per_group,), w.dtype)
    cols = [w[..., _W_VS:_W_KW], w[..., _W_VW:_W_GATE]]
    for g in range(N_KV_GROUPS):
        cols += [w[..., _W_GATE + g * per_group:_W_GATE + (g + 1) * per_group], pad]
    return w_main, jnp.swapaxes(jnp.concatenate(cols, axis=-1), -1, -2)


_TM_INPROJ, _TM_FFN, _FC_FFN, _TQ, _KC, _WQ = 1024, 512, 256, 512, 512, 128


def kernel(x, w_in, short_conv_w, cmp_pe, cmp_w1, cmp_b1, cmp_w2, cmp_b2, head_norm_g, w_out, ln1_g, ln1_b,
           w_up, ffn_conv_w, ffn_conv_b, w_down, ln2_g, ln2_b):
    B, T, D = x.shape
    assert D == D_MODEL and w_in.shape[0] == DEPTH
    tm = min(_TM_FFN, T)
    tm_in = min(_TM_INPROJ, T)
    cos_t, rot_t = _rope_lane_tables(T, tm_in)
    gmean = _group_mean_matrix()
    agg_t = _slc_aggregation_t(T)
    w_main, w_t = _split_in_weights(w_in)
    cmp_w1_b, cmp_w2_b = cmp_w1.astype(MXU_DTYPE), cmp_w2.astype(MXU_DTYPE)
    w_out_b, w_up_b, w_down_b = w_out.astype(MXU_DTYPE), w_up.astype(MXU_DTYPE), w_down.astype(MXU_DTYPE)
    cmp_pe_flat = cmp_pe.reshape(DEPTH, 2, 1, CMP_LEN * HEAD_DIM)
    for i in range(DEPTH):
        hg = head_norm_g[i]
        conv_n, q, cv, ks, kw, vst, vwt, gates = _inproj(
            x, w_main[i], w_t[i], short_conv_w[i], cos_t, rot_t, hg[None, :CONV_WIDTH], gmean, tm_in)
        cmp_kv, cmp_kv_t = _compress(
            cv.reshape(2, B * N_KV_GROUPS, T, HEAD_DIM), cmp_pe_flat[i], cmp_w1_b[i], cmp_b1[i], cmp_w2_b[i],
            cmp_b2[i])
        attn_n = _nsa(q, cmp_kv, cmp_kv_t, ks, vst, kw, vwt, gates, agg_t, hg[CONV_WIDTH:], tq=_TQ, kc=_KC,
                      wq=_WQ)
        x = _mix_ffn(conv_n.reshape(B * T, CONV_WIDTH), attn_n.reshape(B * T, ATTN_WIDTH), x.reshape(B * T, D),
                     w_out_b[i], ln1_g[i][None], ln1_b[i][None], w_up_b[i], ffn_conv_w[i], ffn_conv_b[i][None],
                     w_down_b[i], ln2_g[i][None], ln2_b[i][None], tm, T // tm, _FC_FFN).reshape(B, T, D)
    return x
```

```python
import functools

import numpy as np
import jax
import jax.numpy as jnp
from jax import lax
from jax.experimental import pallas as pl
from jax.experimental.pallas import tpu as pltpu

D_MODEL = 1024
DEPTH = 2
HEAD_DIM = 64
CONV_WIDTH = 512
ATTN_WIDTH = 512
N_Q_HEADS = 8
N_KV_GROUPS = 2
Q_PER_KV = 4
KV_WIDTH = N_KV_GROUPS * HEAD_DIM
N_GATES = 3
CMP_LEN = 32
CMP_STRIDE = 16
CMP_HIDDEN = 256
SLC_LEN = 64
SLC_TOPK = 16
WINDOW = 512
ROPE_THETA = 500000.0
ROPE_DIM = 16
D_FF = 2816
ALPHA = (2.0 * DEPTH) ** 0.25
NEG = -1e30
BIG = 1e9

F32 = jnp.float32
MXU_DTYPE = jnp.bfloat16
LANES = 128
SUBLANES = 8
MXU_COLS = 256
VMEM_LIMIT = 56 * 1024 * 1024

_W_VS, _W_KW, _W_VW, _W_GATE = 2432, 2560, 2688, 2816
_OFF_B, _OFF_C, _OFF_H, _OFF_Q, _OFF_KC, _OFF_KSW = 0, 512, 1024, 1536, 2048, 2304
_OFF_PQ, _OFF_PKSW, _OFF_PKC = 2560, 3072, 3328
_MAIN_COLS = 3456
LOG2E = 1.4426950408889634
_GATE_ROWS = 16
_ROW_VS, _ROW_VW, _ROW_GATE = 0, KV_WIDTH, 2 * KV_WIDTH
_T_ROWS = 2 * KV_WIDTH + N_KV_GROUPS * _GATE_ROWS

_NT = (((1,), (1,)), ((), ()))


def _dot(a, b):
    return jnp.dot(a, b, preferred_element_type=F32)


def _dot_nt(a, b):
    return lax.dot_general(a, b, _NT, preferred_element_type=F32)


def _layer_norm(y, g, b):
    mu = jnp.mean(y, axis=-1, keepdims=True)
    d = y - mu
    var = jnp.mean(d * d, axis=-1, keepdims=True)
    return d * lax.rsqrt(var + 1e-5) * g + b


def _inproj_kernel(x_ref, w_ref, wt_ref, cw_ref, cos_ref, rot_ref, hg_ref, gmean_ref,
                   conv_ref, q_ref, cv_ref, ks_ref, kw_ref, vst_ref, vwt_ref, gate_ref, zbuf):
    tm = x_ref.shape[0]
    xb = x_ref[...].astype(MXU_DTYPE)

    def mm(c0, width):
        return _dot(xb, w_ref[:, c0:c0 + width])

    @pl.when(pl.program_id(1) == 0)
    def _():
        zbuf[0:SUBLANES, :] = jnp.zeros((SUBLANES, CONV_WIDTH), F32)

    def conv_block(c0):
        cs = slice(c0, c0 + MXU_COLS)
        z = mm(_OFF_C + c0, MXU_COLS) * mm(_OFF_H + c0, MXU_COLS)
        zbuf[SUBLANES:SUBLANES + tm, cs] = z
        conv = (cw_ref[2:3, cs] * z + cw_ref[1:2, cs] * zbuf[SUBLANES - 1:SUBLANES - 1 + tm, cs]
                + cw_ref[0:1, cs] * zbuf[SUBLANES - 2:SUBLANES - 2 + tm, cs])
        zbuf[0:SUBLANES, cs] = zbuf[tm:tm + SUBLANES, cs]
        y = mm(_OFF_B + c0, MXU_COLS) * conv
        y2 = y * y
        y2_hi = y2.astype(MXU_DTYPE)
        y2_lo = (y2 - y2_hi.astype(F32)).astype(MXU_DTYPE)
        ms = _dot(y2_hi, gmean_ref[...]) + _dot(y2_lo, gmean_ref[...])
        conv_ref[:, cs] = (y * lax.rsqrt(ms + 1e-6) * hg_ref[:, cs]).astype(conv_ref.dtype)

    def rope(v, partner):
        return v * cos_ref[...] + partner * rot_ref[...]

    def split_heads(v, ref, lead):
        for g in range(N_KV_GROUPS):
            ref[lead + (g,)] = v[:, g * HEAD_DIM:(g + 1) * HEAD_DIM].astype(ref.dtype)

    lane = lax.broadcasted_iota(jnp.int32, (tm, LANES), 1)
    low = lane < HEAD_DIM

    def head_pair(v):
        return v, pltpu.roll(v, HEAD_DIM, 1)

    def q_block(c):
        r4 = mm(_OFF_Q + c * MXU_COLS, MXU_COLS)
        p4 = mm(_OFF_PQ + c * MXU_COLS, MXU_COLS)
        for i in range(MXU_COLS // LANES):
            sl = slice(i * LANES, (i + 1) * LANES)
            r = rope(r4[:, sl], p4[:, sl]) * (HEAD_DIM ** -0.5 * LOG2E)
            for j, h in enumerate(head_pair(r)):
                q_ref[4 * c + 2 * i + j] = jnp.where(low, h, 0.0).astype(q_ref.dtype)

    conv_block(0)
    q_block(0)
    conv_block(MXU_COLS)
    q_block(1)
    kvc = mm(_OFF_KC, 2 * KV_WIDTH)
    split_heads(rope(kvc[:, :KV_WIDTH], mm(_OFF_PKC, KV_WIDTH)), cv_ref, (0,))
    split_heads(kvc[:, KV_WIDTH:], cv_ref, (1,))
    ksw = mm(_OFF_KSW, 2 * KV_WIDTH)
    pksw = mm(_OFF_PKSW, 2 * KV_WIDTH)
    pos = pl.program_id(1) * tm + lax.broadcasted_iota(jnp.int32, (tm, LANES), 0)
    block_tag = jnp.where(pos // SLC_LEN == lane - HEAD_DIM, NEG, 0.0)
    for g, h in enumerate(head_pair(rope(ksw[:, :KV_WIDTH], pksw[:, :KV_WIDTH]))):
        ks_ref[g] = jnp.where(low, h, block_tag).astype(ks_ref.dtype)
    split_heads(rope(ksw[:, KV_WIDTH:], pksw[:, KV_WIDTH:]), kw_ref, ())

    vt = _dot_nt(wt_ref[...], xb)
    ones = jnp.ones((HEAD_DIM, tm), vst_ref.dtype)
    for g in range(N_KV_GROUPS):
        for ref, row0 in ((vst_ref, _ROW_VS), (vwt_ref, _ROW_VW)):
            ref[g, 0:HEAD_DIM, :] = vt[row0 + g * HEAD_DIM:row0 + (g + 1) * HEAD_DIM].astype(ref.dtype)
            ref[g, HEAD_DIM:2 * HEAD_DIM, :] = ones
        gate_ref[g] = jax.nn.sigmoid(vt[_ROW_GATE + g * _GATE_ROWS:_ROW_GATE + (g + 1) * _GATE_ROWS])


def _inproj(x, w_main, w_t, conv_w, cos_t, rot_t, hg_conv, gmean, tm):
    B, T, D = x.shape
    G = N_KV_GROUPS
    kv_shape = jax.ShapeDtypeStruct((B, G, T, HEAD_DIM), MXU_DTYPE)
    kv_spec = pl.BlockSpec((None, G, tm, HEAD_DIM), lambda b, i: (b, 0, i, 0))
    vt_shape = jax.ShapeDtypeStruct((B, G, 2 * HEAD_DIM, T), MXU_DTYPE)
    vt_spec = pl.BlockSpec((None, G, 2 * HEAD_DIM, tm), lambda b, i: (b, 0, 0, i))
    tab_spec = pl.BlockSpec((tm, LANES), lambda b, i: (i, 0))
    const2 = lambda b, i: (0, 0)
    return pl.pallas_call(
        _inproj_kernel,
        grid=(B, T // tm),
        in_specs=[
            pl.BlockSpec((None, tm, D), lambda b, i: (b, i, 0)),
            pl.BlockSpec((D, _MAIN_COLS), const2),
            pl.BlockSpec((_T_ROWS, D), const2),
            pl.BlockSpec((3, CONV_WIDTH), const2),
            tab_spec, tab_spec,
            pl.BlockSpec((1, CONV_WIDTH), const2),
            pl.BlockSpec((MXU_COLS, MXU_COLS), const2),
        ],
        out_specs=[
            pl.BlockSpec((None, tm, CONV_WIDTH), lambda b, i: (b, i, 0)),
            pl.BlockSpec((None, N_Q_HEADS, tm, LANES), lambda b, i: (b, 0, i, 0)),
            pl.BlockSpec((2, None, G, tm, HEAD_DIM), lambda b, i: (0, b, 0, i, 0)),
            pl.BlockSpec((None, G, tm, LANES), lambda b, i: (b, 0, i, 0)),
            kv_spec, vt_spec, vt_spec,
            pl.BlockSpec((None, G, _GATE_ROWS, tm), lambda b, i: (b, 0, 0, i)),
        ],
        out_shape=[
            jax.ShapeDtypeStruct((B, T, CONV_WIDTH), MXU_DTYPE),
            jax.ShapeDtypeStruct((B, N_Q_HEADS, T, LANES), MXU_DTYPE),
            jax.ShapeDtypeStruct((2, B, G, T, HEAD_DIM), F32),
            jax.ShapeDtypeStruct((B, G, T, LANES), MXU_DTYPE),
            kv_shape, vt_shape, vt_shape,
            jax.ShapeDtypeStruct((B, G, _GATE_ROWS, T), F32),
        ],
        scratch_shapes=[pltpu.VMEM((tm + 2 * SUBLANES, CONV_WIDTH), F32)],
        compiler_params=pltpu.CompilerParams(
            dimension_semantics=("arbitrary", "arbitrary"), vmem_limit_bytes=VMEM_LIMIT),
    )(x, w_main, w_t, conv_w, cos_t, rot_t, hg_conv, gmean)


def _compress_kernel(a_ref, pe_ref, w1_ref, b1_ref, w2_ref, b2_ref, w2t_ref, b2t_ref, out_ref, outt_ref,
                     sub_ref, bbuf):
    nsub = sub_ref.shape[0]
    half = CMP_STRIDE * HEAD_DIM
    for j in range(CMP_STRIDE):
        sub_ref[:, j * HEAD_DIM:(j + 1) * HEAD_DIM] = (
            a_ref[pl.ds(j, nsub, stride=CMP_STRIDE), :].astype(sub_ref.dtype))
    sub = sub_ref[...]
    top = _dot(sub, w1_ref[0:half, :])
    bbuf[0:nsub, :] = _dot(sub, w1_ref[half:2 * half, :])
    bbuf[nsub:nsub + SUBLANES, :] = jnp.zeros((SUBLANES, CMP_HIDDEN), F32)
    pe_rows = jnp.broadcast_to(pe_ref[...], (SUBLANES, 2 * half)).astype(MXU_DTYPE)
    const = _dot(pe_rows, w1_ref[...])[0:1, :] + b1_ref[...]
    h = top + bbuf[1:nsub + 1, :] + const
    act = jax.nn.gelu(h).astype(MXU_DTYPE)
    out_ref[...] = (_dot(act, w2_ref[...]) + b2_ref[...]).astype(out_ref.dtype)
    outt_ref[...] = (_dot_nt(w2t_ref[...], act) + b2t_ref[...]).astype(outt_ref.dtype)


def _compress(a, pe, w1, b1, w2, b2):
    _, BG, T, _ = a.shape
    nsub = T // CMP_STRIDE
    width = CMP_STRIDE * HEAD_DIM
    sel = lambda k, n: (k, 0, 0)
    per = lambda k, n: (k, n, 0, 0)
    return pl.pallas_call(
        _compress_kernel,
        grid=(2, BG),
        in_specs=[
            pl.BlockSpec((None, None, T, HEAD_DIM), per),
            pl.BlockSpec((None, 1, 2 * width), sel),
            pl.BlockSpec((None, 2 * width, CMP_HIDDEN), sel),
            pl.BlockSpec((None, 1, CMP_HIDDEN), sel),
            pl.BlockSpec((None, CMP_HIDDEN, HEAD_DIM), sel),
            pl.BlockSpec((None, 1, HEAD_DIM), sel),
            pl.BlockSpec((None, HEAD_DIM, CMP_HIDDEN), sel),
            pl.BlockSpec((None, HEAD_DIM, 1), sel),
        ],
        out_specs=[pl.BlockSpec((None, None, nsub, HEAD_DIM), per),
                   pl.BlockSpec((None, None, HEAD_DIM, nsub), per)],
        out_shape=[jax.ShapeDtypeStruct((2, BG, nsub, HEAD_DIM), MXU_DTYPE),
                   jax.ShapeDtypeStruct((2, BG, HEAD_DIM, nsub), MXU_DTYPE)],
        scratch_shapes=[pltpu.VMEM((nsub, width), MXU_DTYPE), pltpu.VMEM((nsub + SUBLANES, CMP_HIDDEN), F32)],
        compiler_params=pltpu.CompilerParams(
            dimension_semantics=("arbitrary", "arbitrary"), vmem_limit_bytes=VMEM_LIMIT),
    )(a, pe, w1, b1[:, None, :], w2, b2[:, None, :], jnp.swapaxes(w2, 1, 2), b2[:, :, None])


def _rank_accumulate(score, ranks, i0, i1):
    tq = score.shape[1]
    local = lax.broadcasted_iota(jnp.int32, (SUBLANES, tq), 0)
    ranks = list(ranks)
    for i in range(i0, i1):
        row = jnp.broadcast_to(score[i:i + 1, :], (SUBLANES, tq))
        for j in range(len(ranks)):
            slab = score[SUBLANES * j:SUBLANES * (j + 1), :]
            ge = jnp.where(row >= slab, 1.0, 0.0)
            gt = jnp.where(row > slab, 1.0, 0.0)
            if i < SUBLANES * j:
                ahead = ge
            elif i >= SUBLANES * (j + 1):
                ahead = gt
            else:
                ahead = jnp.where(local > i - SUBLANES * j, ge, gt)
            ranks[j] = ranks[j] + ahead
    return ranks


def _nsa_kernel(q_ref, kcmp_ref, vcmpt_ref, ks_ref, vst_ref, kw_ref, vwt_ref, gate_ref, aggt_ref, hg_ref,
                o_ref, sa_ref, sb_ref, acc_ref, *, tq, kc, wq):
    R = Q_PER_KV
    rows = R * tq
    T = ks_ref.shape[0]
    ns = T // SLC_LEN
    ncp = kcmp_ref.shape[0]
    q0 = pl.program_id(2) * tq
    t_row = q0 + lax.broadcasted_iota(jnp.int32, (1, tq), 1)
    q_pad = q_ref[...]
    q64 = q_pad.reshape(rows, LANES)[:, :HEAD_DIM]

    def all_heads(a):
        return jnp.concatenate([a] * R, axis=1)

    def head(a, r, width):
        return a[:, r * width:(r + 1) * width]

    n_win = tq // wq
    wl = WINDOW + wq

    def window_piece(h):
        t_piece = t_row[:, h * wq:(h + 1) * wq]
        ws = pl.multiple_of(jnp.maximum(q0 + h * wq - WINDOW, 0), wq)
        wpos = ws + lax.broadcasted_iota(jnp.int32, (wl, wq), 0)
        wbias = jnp.where((wpos <= t_piece) & (wpos > t_piece - WINDOW), 0.0, NEG)
        qh = q64.reshape(R, tq, HEAD_DIM)[:, h * wq:(h + 1) * wq, :].reshape(R * wq, HEAD_DIM)
        s = _dot_nt(kw_ref[pl.ds(ws, wl), :], qh) + jnp.concatenate([wbias] * R, axis=1)
        p = jnp.exp2(s - jnp.max(s, axis=0, keepdims=True)).astype(MXU_DTYPE)
        ow = _dot(vwt_ref[:, pl.ds(ws, wl)], p)
        return ow[:HEAD_DIM] / ow[HEAD_DIM:]

    cmp_end = lax.broadcasted_iota(jnp.int32, (ncp, tq), 0) * CMP_STRIDE + (CMP_LEN - 1)
    cbias = jnp.where(cmp_end <= t_row, 0.0, NEG)
    s = _dot_nt(kcmp_ref[...], q64) + all_heads(cbias)
    p = jnp.exp2(s - jnp.max(s, axis=0, keepdims=True))
    l = jnp.sum(p, axis=0, keepdims=True)
    p = p * (all_heads(jnp.where(t_row >= CMP_LEN - 1, 1.0, 0.0)) / l)
    o_cmp = _dot(vcmpt_ref[...], p.astype(MXU_DTYPE))

    psum = head(p, 0, tq) + head(p, 1, tq) + head(p, 2, tq) + head(p, 3, tq)
    p_hi = psum.astype(MXU_DTYPE)
    p_lo = (psum - p_hi.astype(F32)).astype(MXU_DTYPE)
    imp = _dot(aggt_ref[...], p_hi) + _dot(aggt_ref[...], p_lo)
    blk = lax.broadcasted_iota(jnp.int32, (ns, tq), 0)
    tb = (q0 + lax.broadcasted_iota(jnp.int32, (ns, tq), 1)) // SLC_LEN
    forced = (blk == 0) | (blk == tb) | (blk == tb - 1)
    score = jnp.where(forced, BIG, imp)
    score = jnp.where(blk <= tb, score, -BIG)
    ranks = [jnp.zeros((SUBLANES, tq), F32) for _ in range(ns // SUBLANES)]
    ranks = _rank_accumulate(score, ranks, 0, ns // 2)
    o_win_pieces = [window_piece(h) for h in range(n_win // 2)]
    ranks = _rank_accumulate(score, ranks, ns // 2, ns)
    unselected = jnp.where(jnp.concatenate(ranks, axis=0) < min(SLC_TOPK, ns), 0.0, 1.0)
    tag_rows = [jnp.zeros((HEAD_DIM, tq), F32), unselected]
    if ns < LANES - HEAD_DIM:
        tag_rows.append(jnp.zeros((LANES - HEAD_DIM - ns, tq), F32))
    tag = jnp.concatenate(tag_rows, axis=0).T.astype(MXU_DTYPE)
    lane = lax.broadcasted_iota(jnp.int32, (R, tq, LANES), 2)
    q_aug = jnp.where(lane < HEAD_DIM, q_pad, tag[None]).reshape(rows, LANES)

    def scores(c):
        return _dot_nt(ks_ref[pl.ds(pl.multiple_of(c * kc, kc), kc), :], q_aug)

    def causal_bias(c):
        kpos = c * kc + lax.broadcasted_iota(jnp.int32, (kc, tq), 0)
        return all_heads(jnp.where(kpos <= t_row, 0.0, NEG))

    def consume(s, c, m):
        m_new = jnp.maximum(m, jnp.max(s, axis=0, keepdims=True))
        p = jnp.exp2(s - m_new).astype(MXU_DTYPE)
        pv = _dot(vst_ref[:, pl.ds(pl.multiple_of(c * kc, kc), kc)], p)
        acc_ref[...] = jnp.exp2(m - m_new) * acc_ref[...] + pv
        return m_new

    n_pairs = q0 // (2 * kc)
    acc_ref[...] = jnp.zeros((2 * HEAD_DIM, rows), F32)
    sa_ref[...] = scores(0)
    o_win_pieces += [window_piece(h) for h in range(n_win // 2, n_win)]

    def pair_step(j, m):
        sb_ref[...] = scores(2 * j + 1)
        m = consume(sa_ref[...], 2 * j, m)
        sa_ref[...] = scores(2 * j + 2)
        return consume(sb_ref[...], 2 * j + 1, m)

    m = lax.fori_loop(0, n_pairs, pair_step, jnp.full((1, rows), NEG, F32))
    c_last = 2 * n_pairs
    m = consume(sa_ref[...] + causal_bias(c_last), c_last, m)

    @pl.when(q0 + tq > (c_last + 1) * kc)
    def _():
        consume(scores(c_last + 1) + causal_bias(c_last + 1), c_last + 1, m)

    acc = acc_ref[...]
    o_slc = acc[:HEAD_DIM] / acc[HEAD_DIM:]

    g = gate_ref[...]
    outs = []
    for r in range(R):
        c = N_GATES * r
        o_win = jnp.concatenate([head(piece, r, wq) for piece in o_win_pieces], axis=1)
        o = g[c:c + 1] * head(o_cmp, r, tq) + g[c + 1:c + 2] * head(o_slc, r, tq) + g[c + 2:c + 3] * o_win
        ms = jnp.mean(o * o, axis=0, keepdims=True)
        outs.append(o * lax.rsqrt(ms + 1e-6) * hg_ref[r * HEAD_DIM:(r + 1) * HEAD_DIM, :])
    o_ref[...] = jnp.concatenate(outs, axis=0).T.astype(o_ref.dtype)


def _nsa(q, cmp_kv, cmp_kv_t, ks, vst, kw, vwt, gates, agg_t, hg_attn, tq, kc, wq):
    B, _, T, _ = q.shape
    G, R = N_KV_GROUPS, Q_PER_KV
    ncp = cmp_kv.shape[2]
    ns = T // SLC_LEN
    assert T % (2 * kc) == 0 and kc % tq == 0 and tq % (2 * wq) == 0 and WINDOW % wq == 0 and T >= WINDOW + tq
    assert ns <= LANES - HEAD_DIM, "block tags must fit in the upper lanes of a key row"
    whole = lambda b, g, i: (b, g, 0, 0)
    hg_lanes = jnp.broadcast_to(hg_attn.reshape(G, R * HEAD_DIM, 1), (G, R * HEAD_DIM, tq))
    return pl.pallas_call(
        functools.partial(_nsa_kernel, tq=tq, kc=kc, wq=wq),
        grid=(B, G, T // tq),
        in_specs=[
            pl.BlockSpec((None, R, tq, LANES), lambda b, g, i: (b, g, i, 0)),
            pl.BlockSpec((None, None, ncp, HEAD_DIM), lambda b, g, i: (0, b * G + g, 0, 0)),
            pl.BlockSpec((None, None, HEAD_DIM, ncp), lambda b, g, i: (1, b * G + g, 0, 0)),
            pl.BlockSpec((None, None, T, LANES), whole),
            pl.BlockSpec((None, None, 2 * HEAD_DIM, T), whole),
            pl.BlockSpec((None, None, T, HEAD_DIM), whole),
            pl.BlockSpec((None, None, 2 * HEAD_DIM, T), whole),
            pl.BlockSpec((None, None, _GATE_ROWS, tq), lambda b, g, i: (b, g, 0, i)),
            pl.BlockSpec((ns, ncp), lambda b, g, i: (0, 0)),
            pl.BlockSpec((None, R * HEAD_DIM, tq), lambda b, g, i: (g, 0, 0)),
        ],
        out_specs=pl.BlockSpec((None, tq, R * HEAD_DIM), lambda b, g, i: (b, i, g)),
        out_shape=jax.ShapeDtypeStruct((B, T, ATTN_WIDTH), MXU_DTYPE),
        scratch_shapes=[pltpu.VMEM((kc, R * tq), F32), pltpu.VMEM((kc, R * tq), F32),
                        pltpu.VMEM((2 * HEAD_DIM, R * tq), F32)],
        compiler_params=pltpu.CompilerParams(
            dimension_semantics=("arbitrary", "arbitrary", "arbitrary"), vmem_limit_bytes=VMEM_LIMIT),
    )(q, cmp_kv, cmp_kv_t, ks, vst, kw, vwt, gates, agg_t, hg_lanes)


def _mix_ffn_kernel(conv_ref, attn_ref, x_ref, wout_ref, g1_ref, b1_ref,
                    wup_ref, cw_ref, cb_ref, wdn_ref, g_ref, b_ref, o_ref, tail_ref, ubuf, act_ref,
                    *, tiles_per_seq, fc):
    tm = x_ref.shape[0]

    @pl.when(pl.program_id(0) % tiles_per_seq == 0)
    def _():
        tail_ref[...] = jnp.zeros(tail_ref.shape, F32)

    mix = _dot(conv_ref[...], wout_ref[0:CONV_WIDTH, :]) + _dot(attn_ref[...], wout_ref[CONV_WIDTH:, :])
    x = _layer_norm(ALPHA * x_ref[...] + mix, g1_ref[...], b1_ref[...])
    xb = jnp.concatenate([tail_ref[...], x], axis=0).astype(MXU_DTYPE)
    tail_ref[...] = x[tm - SUBLANES:]

    def up_conv(col0, buf):
        buf[...] = _dot(xb, wup_ref[:, col0:col0 + fc])
        return (cw_ref[2:3, col0:col0 + fc] * buf[SUBLANES:SUBLANES + tm, :]
                + cw_ref[1:2, col0:col0 + fc] * buf[SUBLANES - 1:SUBLANES - 1 + tm, :]
                + cw_ref[0:1, col0:col0 + fc] * buf[SUBLANES - 2:SUBLANES - 2 + tm, :]
                + cb_ref[:, col0:col0 + fc])

    for c in range(D_FF // fc):
        val = up_conv(c * fc, ubuf.at[c % 2, 0])
        gate = up_conv(D_FF + c * fc, ubuf.at[c % 2, 1])
        act_ref[:, c * fc:(c + 1) * fc] = (gate * jax.nn.sigmoid(gate) * val).astype(act_ref.dtype)
    o_ref[...] = _layer_norm(ALPHA * x + _dot(act_ref[...], wdn_ref[...]), g_ref[...], b_ref[...])


def _mix_ffn(conv_n, attn_n, x, w_out, g1, b1, w_up, conv_w, conv_b, w_down, g, b, tm, tiles_per_seq, fc):
    N, D = x.shape
    row = lambda i: (i, 0)
    const = lambda i: (0, 0)
    return pl.pallas_call(
        functools.partial(_mix_ffn_kernel, tiles_per_seq=tiles_per_seq, fc=fc),
        grid=(N // tm,),
        in_specs=[
            pl.BlockSpec((tm, CONV_WIDTH), row), pl.BlockSpec((tm, ATTN_WIDTH), row), pl.BlockSpec((tm, D), row),
            pl.BlockSpec((D, D), const, pipeline_mode=pl.Buffered(1)),
            pl.BlockSpec((1, D), const), pl.BlockSpec((1, D), const),
            pl.BlockSpec((D, 2 * D_FF), const, pipeline_mode=pl.Buffered(1)),
            pl.BlockSpec((3, 2 * D_FF), const),
            pl.BlockSpec((1, 2 * D_FF), const),
            pl.BlockSpec((D_FF, D), const, pipeline_mode=pl.Buffered(1)),
            pl.BlockSpec((1, D), const), pl.BlockSpec((1, D), const),
        ],
        out_specs=pl.BlockSpec((tm, D), row),
        out_shape=jax.ShapeDtypeStruct((N, D), F32),
        scratch_shapes=[pltpu.VMEM((SUBLANES, D), F32), pltpu.VMEM((2, 2, tm + SUBLANES, fc), F32),
                        pltpu.VMEM((tm, D_FF), MXU_DTYPE)],
        compiler_params=pltpu.CompilerParams(
            dimension_semantics=("arbitrary",), vmem_limit_bytes=VMEM_LIMIT),
    )(conv_n, attn_n, x, w_out, g1, b1, w_up, conv_w, conv_b, w_down, g, b)


def _rope_tables_kernel(freq_hi_ref, freq_lo_ref, sign_ref, cos_ref, rot_ref):
    tr = cos_ref.shape[0]
    pos = (pl.program_id(0) * tr + lax.broadcasted_iota(jnp.int32, (tr, LANES), 0)).astype(F32)
    ang = pos * freq_hi_ref[...] + pos * freq_lo_ref[...]
    cos_ref[...] = jnp.cos(ang)
    rot_ref[...] = jnp.sin(ang) * sign_ref[...]


def _rope_lane_tables(T, tr):
    half = ROPE_DIM // 2
    inv_freq = ROPE_THETA ** (-(np.arange(half, dtype=np.float64) * 2.0 / ROPE_DIM))
    lane = np.arange(LANES) % HEAD_DIM
    freq = np.where(lane < ROPE_DIM, inv_freq[lane % half], 0.0)
    freq_hi = freq.astype(np.float32)
    freq_lo = (freq - freq_hi.astype(np.float64)).astype(np.float32)
    sign = np.where(lane < half, -1.0, np.where(lane < ROPE_DIM, 1.0, 0.0)).astype(np.float32)
    lanes_spec = pl.BlockSpec((1, LANES), lambda i: (0, 0))
    rows_spec = pl.BlockSpec((tr, LANES), lambda i: (i, 0))
    return pl.pallas_call(
        _rope_tables_kernel,
        grid=(T // tr,),
        in_specs=[lanes_spec, lanes_spec, lanes_spec],
        out_specs=[rows_spec, rows_spec],
        out_shape=[jax.ShapeDtypeStruct((T, LANES), F32)] * 2,
    )(jnp.asarray(freq_hi)[None], jnp.asarray(freq_lo)[None], jnp.asarray(sign)[None])


def _group_mean_matrix():
    grp = np.arange(MXU_COLS) // HEAD_DIM
    return jnp.asarray((grp[:, None] == grp[None, :]) / HEAD_DIM, dtype=MXU_DTYPE)


def _slc_aggregation_t(T):
    nc = T // CMP_STRIDE - CMP_LEN // CMP_STRIDE + 1
    ns = T // SLC_LEN
    sc = np.arange(nc)[None, :] * CMP_STRIDE
    ss = np.arange(ns)[:, None] * SLC_LEN
    ov = np.clip(np.minimum(sc + CMP_LEN, ss + SLC_LEN) - np.maximum(sc, ss), 0, None) / CMP_LEN
    ov = np.pad(ov, ((0, 0), (0, T // CMP_STRIDE - nc)))
    return jnp.asarray(ov, dtype=MXU_DTYPE)


def _split_in_weights(w):
    w = w.astype(MXU_DTYPE)
    plain = jnp.concatenate([w[..., :_W_VS], w[..., _W_KW:_W_VW]], axis=-1)
    half = ROPE_DIM // 2
    partners = []
    for c0, width in ((_OFF_Q, ATTN_WIDTH), (_OFF_KSW, 2 * KV_WIDTH), (_OFF_KC, KV_WIDTH)):
        heads = plain[..., c0:c0 + width].reshape(w.shape[:-1] + (width // HEAD_DIM, HEAD_DIM))
        swapped = jnp.concatenate([heads[..., half:ROPE_DIM], heads[..., :half],
                                   jnp.zeros_like(heads[..., ROPE_DIM:])], axis=-1)
        partners.append(swapped.reshape(w.shape[:-1] + (width,)))
    w_main = jnp.concatenate([plain] + partners, axis=-1)
    per_group = Q_PER_KV * N_GATES
    pad = jnp.zeros(w.shape[:-1] + (_GATE_ROWS - per_group,), w.dtype)
    cols = [w[..., _W_VS:_W_KW], w[..., _W_VW:_W_GATE]]
    for g in range(N_KV_GROUPS):
        cols += [w[..., _W_GATE + g * per_group:_W_GATE + (g + 1) * per_group], pad]
    return w_main, jnp.swapaxes(jnp.concatenate(cols, axis=-1), -1, -2)


_TM_INPROJ, _TM_FFN, _FC_FFN, _TQ, _KC, _WQ = 1024, 512, 256, 512, 512, 128


def kernel(x, w_in, short_conv_w, cmp_pe, cmp_w1, cmp_b1, cmp_w2, cmp_b2, head_norm_g, w_out, ln1_g, ln1_b,
           w_up, ffn_conv_w, ffn_conv_b, w_down, ln2_g, ln2_b):
    B, T, D = x.shape
    assert D == D_MODEL and w_in.shape[0] == DEPTH
    tm = min(_TM_FFN, T)
    tm_in = min(_TM_INPROJ, T)
    cos_t, rot_t = _rope_lane_tables(T, tm_in)
    gmean = _group_mean_matrix()
    agg_t = _slc_aggregation_t(T)
    w_main, w_t = _split_in_weights(w_in)
    cmp_pe_flat = cmp_pe.reshape(DEPTH, 2, 1, CMP_LEN * HEAD_DIM)
    to_mxu = lambda w: w.astype(MXU_DTYPE)
    for i in range(DEPTH):
        hg = head_norm_g[i]
        conv_n, q, cv, ks, kw, vst, vwt, gates = _inproj(
            x, w_main[i], w_t[i], short_conv_w[i], cos_t, rot_t, hg[None, :CONV_WIDTH], gmean, tm_in)
        cmp_kv, cmp_kv_t = _compress(
            cv.reshape(2, B * N_KV_GROUPS, T, HEAD_DIM), cmp_pe_flat[i], to_mxu(cmp_w1[i]), cmp_b1[i],
            to_mxu(cmp_w2[i]), cmp_b2[i])
        attn_n = _nsa(q, cmp_kv, cmp_kv_t, ks, vst, kw, vwt, gates, agg_t, hg[CONV_WIDTH:], tq=_TQ, kc=_KC,
                      wq=_WQ)
        x = _mix_ffn(conv_n.reshape(B * T, CONV_WIDTH), attn_n.reshape(B * T, ATTN_WIDTH), x.reshape(B * T, D),
                     to_mxu(w_out[i]), ln1_g[i][None], ln1_b[i][None], to_mxu(w_up[i]), ffn_conv_w[i],
                     ffn_conv_b[i][None], to_mxu(w_down[i]), ln2_g[i][None], ln2_b[i][None], tm, T // tm,
                     _FC_FFN).reshape(B, T, D)
    return x
```

```python
import functools

import numpy as np
import jax
import jax.numpy as jnp
from jax import lax
from jax.experimental import pallas as pl
from jax.experimental.pallas import tpu as pltpu

D_MODEL = 1024
DEPTH = 2
HEAD_DIM = 64
CONV_WIDTH = 512
ATTN_WIDTH = 512
N_Q_HEADS = 8
N_KV_GROUPS = 2
Q_PER_KV = 4
KV_WIDTH = N_KV_GROUPS * HEAD_DIM
N_GATES = 3
CMP_LEN = 32
CMP_STRIDE = 16
CMP_HIDDEN = 256
SLC_LEN = 64
SLC_TOPK = 16
WINDOW = 512
ROPE_THETA = 500000.0
ROPE_DIM = 16
D_FF = 2816
ALPHA = (2.0 * DEPTH) ** 0.25
NEG = -1e30
BIG = 1e9

F32 = jnp.float32
MXU_DTYPE = jnp.bfloat16
LANES = 128
SUBLANES = 8
MXU_COLS = 256
VMEM_LIMIT = 56 * 1024 * 1024

_W_VS, _W_KW, _W_VW, _W_GATE = 2432, 2560, 2688, 2816
_OFF_B, _OFF_C, _OFF_H, _OFF_Q, _OFF_KC, _OFF_KSW = 0, 512, 1024, 1536, 2048, 2304
_OFF_PQ, _OFF_PKSW, _OFF_PKC = 2560, 3072, 3328
_MAIN_COLS = 3456
LOG2E = 1.4426950408889634
_GATE_ROWS = 16
_ROW_VS, _ROW_VW, _ROW_GATE = 0, KV_WIDTH, 2 * KV_WIDTH
_T_ROWS = 2 * KV_WIDTH + N_KV_GROUPS * _GATE_ROWS

_NT = (((1,), (1,)), ((), ()))


def _dot(a, b):
    return jnp.dot(a, b, preferred_element_type=F32)


def _dot_nt(a, b):
    return lax.dot_general(a, b, _NT, preferred_element_type=F32)


def _layer_norm(y, g, b):
    mu = jnp.mean(y, axis=-1, keepdims=True)
    d = y - mu
    var = jnp.mean(d * d, axis=-1, keepdims=True)
    return d * lax.rsqrt(var + 1e-5) * g + b


def _inproj_kernel(x_ref, w_ref, wt_ref, cw_ref, cos_ref, rot_ref, hg_ref, gmean_ref,
                   conv_ref, q_ref, cv_ref, ks_ref, kw_ref, vst_ref, vwt_ref, gate_ref, zbuf):
    tm = x_ref.shape[0]
    xb = x_ref[...].astype(MXU_DTYPE)

    def mm(c0, width):
        return _dot(xb, w_ref[:, c0:c0 + width])

    @pl.when(pl.program_id(1) == 0)
    def _():
        zbuf[0:SUBLANES, :] = jnp.zeros((SUBLANES, CONV_WIDTH), F32)

    def conv_block(c0):
        cs = slice(c0, c0 + MXU_COLS)
        z = mm(_OFF_C + c0, MXU_COLS) * mm(_OFF_H + c0, MXU_COLS)
        zbuf[SUBLANES:SUBLANES + tm, cs] = z
        conv = (cw_ref[2:3, cs] * z + cw_ref[1:2, cs] * zbuf[SUBLANES - 1:SUBLANES - 1 + tm, cs]
                + cw_ref[0:1, cs] * zbuf[SUBLANES - 2:SUBLANES - 2 + tm, cs])
        zbuf[0:SUBLANES, cs] = zbuf[tm:tm + SUBLANES, cs]
        y = mm(_OFF_B + c0, MXU_COLS) * conv
        y2 = y * y
        y2_hi = y2.astype(MXU_DTYPE)
        y2_lo = (y2 - y2_hi.astype(F32)).astype(MXU_DTYPE)
        ms = _dot(y2_hi, gmean_ref[...]) + _dot(y2_lo, gmean_ref[...])
        conv_ref[:, cs] = (y * lax.rsqrt(ms + 1e-6) * hg_ref[:, cs]).astype(conv_ref.dtype)

    def rope(v, partner):
        return v * cos_ref[...] + partner * rot_ref[...]

    def split_heads(v, ref, lead):
        for g in range(N_KV_GROUPS):
            ref[lead + (g,)] = v[:, g * HEAD_DIM:(g + 1) * HEAD_DIM].astype(ref.dtype)

    lane = lax.broadcasted_iota(jnp.int32, (tm, LANES), 1)
    low = lane < HEAD_DIM

    def head_pair(v):
        return v, pltpu.roll(v, HEAD_DIM, 1)

    def q_block(c):
        r4 = mm(_OFF_Q + c * MXU_COLS, MXU_COLS)
        p4 = mm(_OFF_PQ + c * MXU_COLS, MXU_COLS)
        for i in range(MXU_COLS // LANES):
            sl = slice(i * LANES, (i + 1) * LANES)
            r = rope(r4[:, sl], p4[:, sl]) * (HEAD_DIM ** -0.5 * LOG2E)
            for j, h in enumerate(head_pair(r)):
                q_ref[4 * c + 2 * i + j] = jnp.where(low, h, 0.0).astype(q_ref.dtype)

    conv_block(0)
    q_block(0)
    conv_block(MXU_COLS)
    q_block(1)
    kvc = mm(_OFF_KC, 2 * KV_WIDTH)
    split_heads(rope(kvc[:, :KV_WIDTH], mm(_OFF_PKC, KV_WIDTH)), cv_ref, (0,))
    split_heads(kvc[:, KV_WIDTH:], cv_ref, (1,))
    ksw = mm(_OFF_KSW, 2 * KV_WIDTH)
    pksw = mm(_OFF_PKSW, 2 * KV_WIDTH)
    pos = pl.program_id(1) * tm + lax.broadcasted_iota(jnp.int32, (tm, LANES), 0)
    block_tag = jnp.where(pos // SLC_LEN == lane - HEAD_DIM, NEG, 0.0)
    for g, h in enumerate(head_pair(rope(ksw[:, :KV_WIDTH], pksw[:, :KV_WIDTH]))):
        ks_ref[g] = jnp.where(low, h, block_tag).astype(ks_ref.dtype)
    split_heads(rope(ksw[:, KV_WIDTH:], pksw[:, KV_WIDTH:]), kw_ref, ())

    vt = _dot_nt(wt_ref[...], xb)
    ones = jnp.ones((HEAD_DIM, tm), vst_ref.dtype)
    for g in range(N_KV_GROUPS):
        for ref, row0 in ((vst_ref, _ROW_VS), (vwt_ref, _ROW_VW)):
            ref[g, 0:HEAD_DIM, :] = vt[row0 + g * HEAD_DIM:row0 + (g + 1) * HEAD_DIM].astype(ref.dtype)
            ref[g, HEAD_DIM:2 * HEAD_DIM, :] = ones
        gate_ref[g] = jax.nn.sigmoid(vt[_ROW_GATE + g * _GATE_ROWS:_ROW_GATE + (g + 1) * _GATE_ROWS])


def _inproj(x, w_main, w_t, conv_w, cos_t, rot_t, hg_conv, gmean, tm):
    B, T, D = x.shape
    G = N_KV_GROUPS
    kv_shape = jax.ShapeDtypeStruct((B, G, T, HEAD_DIM), MXU_DTYPE)
    kv_spec = pl.BlockSpec((None, G, tm, HEAD_DIM), lambda b, i: (b, 0, i, 0))
    vt_shape = jax.ShapeDtypeStruct((B, G, 2 * HEAD_DIM, T), MXU_DTYPE)
    vt_spec = pl.BlockSpec((None, G, 2 * HEAD_DIM, tm), lambda b, i: (b, 0, 0, i))
    tab_spec = pl.BlockSpec((tm, LANES), lambda b, i: (i, 0))
    const2 = lambda b, i: (0, 0)
    return pl.pallas_call(
        _inproj_kernel,
        grid=(B, T // tm),
        in_specs=[
            pl.BlockSpec((None, tm, D), lambda b, i: (b, i, 0)),
            pl.BlockSpec((D, _MAIN_COLS), const2),
            pl.BlockSpec((_T_ROWS, D), const2),
            pl.BlockSpec((3, CONV_WIDTH), const2),
            tab_spec, tab_spec,
            pl.BlockSpec((1, CONV_WIDTH), const2),
            pl.BlockSpec((MXU_COLS, MXU_COLS), const2),
        ],
        out_specs=[
            pl.BlockSpec((None, tm, CONV_WIDTH), lambda b, i: (b, i, 0)),
            pl.BlockSpec((None, N_Q_HEADS, tm, LANES), lambda b, i: (b, 0, i, 0)),
            pl.BlockSpec((2, None, G, tm, HEAD_DIM), lambda b, i: (0, b, 0, i, 0)),
            pl.BlockSpec((None, G, tm, LANES), lambda b, i: (b, 0, i, 0)),
            kv_spec, vt_spec, vt_spec,
            pl.BlockSpec((None, G, _GATE_ROWS, tm), lambda b, i: (b, 0, 0, i)),
        ],
        out_shape=[
            jax.ShapeDtypeStruct((B, T, CONV_WIDTH), MXU_DTYPE),
            jax.ShapeDtypeStruct((B, N_Q_HEADS, T, LANES), MXU_DTYPE),
            jax.ShapeDtypeStruct((2, B, G, T, HEAD_DIM), F32),
            jax.ShapeDtypeStruct((B, G, T, LANES), MXU_DTYPE),
            kv_shape, vt_shape, vt_shape,
            jax.ShapeDtypeStruct((B, G, _GATE_ROWS, T), F32),
        ],
        scratch_shapes=[pltpu.VMEM((tm + 2 * SUBLANES, CONV_WIDTH), F32)],
        compiler_params=pltpu.CompilerParams(
            dimension_semantics=("arbitrary", "arbitrary"), vmem_limit_bytes=VMEM_LIMIT),
    )(x, w_main, w_t, conv_w, cos_t, rot_t, hg_conv, gmean)


def _compress_kernel(a_ref, pe_ref, w1_ref, b1_ref, w2_ref, b2_ref, w2t_ref, b2t_ref, out_ref, outt_ref,
                     sub_ref, bbuf):
    nsub = sub_ref.shape[0]
    half = CMP_STRIDE * HEAD_DIM
    for j in range(CMP_STRIDE):
        sub_ref[:, j * HEAD_DIM:(j + 1) * HEAD_DIM] = (
            a_ref[pl.ds(j, nsub, stride=CMP_STRIDE), :].astype(sub_ref.dtype))
    sub = sub_ref[...]
    top = _dot(sub, w1_ref[0:half, :])
    bbuf[0:nsub, :] = _dot(sub, w1_ref[half:2 * half, :])
    bbuf[nsub:nsub + SUBLANES, :] = jnp.zeros((SUBLANES, CMP_HIDDEN), F32)
    pe_rows = jnp.broadcast_to(pe_ref[...], (SUBLANES, 2 * half)).astype(MXU_DTYPE)
    const = _dot(pe_rows, w1_ref[...])[0:1, :] + b1_ref[...]
    h = top + bbuf[1:nsub + 1, :] + const
    act = jax.nn.gelu(h).astype(MXU_DTYPE)
    out_ref[...] = (_dot(act, w2_ref[...]) + b2_ref[...]).astype(out_ref.dtype)
    outt_ref[...] = (_dot_nt(w2t_ref[...], act) + b2t_ref[...]).astype(outt_ref.dtype)


def _compress(a, pe, w1, b1, w2, b2):
    _, BG, T, _ = a.shape
    nsub = T // CMP_STRIDE
    width = CMP_STRIDE * HEAD_DIM
    sel = lambda k, n: (k, 0, 0)
    per = lambda k, n: (k, n, 0, 0)
    return pl.pallas_call(
        _compress_kernel,
        grid=(2, BG),
        in_specs=[
            pl.BlockSpec((None, None, T, HEAD_DIM), per),
            pl.BlockSpec((None, 1, 2 * width), sel),
            pl.BlockSpec((None, 2 * width, CMP_HIDDEN), sel),
            pl.BlockSpec((None, 1, CMP_HIDDEN), sel),
            pl.BlockSpec((None, CMP_HIDDEN, HEAD_DIM), sel),
            pl.BlockSpec((None, 1, HEAD_DIM), sel),
            pl.BlockSpec((None, HEAD_DIM, CMP_HIDDEN), sel),
            pl.BlockSpec((None, HEAD_DIM, 1), sel),
        ],
        out_specs=[pl.BlockSpec((None, None, nsub, HEAD_DIM), per),
                   pl.BlockSpec((None, None, HEAD_DIM, nsub), per)],
        out_shape=[jax.ShapeDtypeStruct((2, BG, nsub, HEAD_DIM), MXU_DTYPE),
                   jax.ShapeDtypeStruct((2, BG, HEAD_DIM, nsub), MXU_DTYPE)],
        scratch_shapes=[pltpu.VMEM((nsub, width), MXU_DTYPE), pltpu.VMEM((nsub + SUBLANES, CMP_HIDDEN), F32)],
        compiler_params=pltpu.CompilerParams(
            dimension_semantics=("arbitrary", "arbitrary"), vmem_limit_bytes=VMEM_LIMIT),
    )(a, pe, w1, b1[:, None, :], w2, b2[:, None, :], jnp.swapaxes(w2, 1, 2), b2[:, :, None])


def _rank_accumulate(score, ranks, i0, i1):
    tq = score.shape[1]
    local = lax.broadcasted_iota(jnp.int32, (SUBLANES, tq), 0)
    ranks = list(ranks)
    for i in range(i0, i1):
        row = jnp.broadcast_to(score[i:i + 1, :], (SUBLANES, tq))
        for j in range(len(ranks)):
            slab = score[SUBLANES * j:SUBLANES * (j + 1), :]
            ge = jnp.where(row >= slab, 1.0, 0.0)
            gt = jnp.where(row > slab, 1.0, 0.0)
            if i < SUBLANES * j:
                ahead = ge
            elif i >= SUBLANES * (j + 1):
                ahead = gt
            else:
                ahead = jnp.where(local > i - SUBLANES * j, ge, gt)
            ranks[j] = ranks[j] + ahead
    return ranks


def _nsa_kernel(q_ref, kcmp_ref, vcmpt_ref, ks_ref, vst_ref, kw_ref, vwt_ref, gate_ref, aggt_ref, hg_ref,
                o_ref, sa_ref, sb_ref, acc_ref, *, tq, kc, wq):
    R = Q_PER_KV
    rows = R * tq
    T = ks_ref.shape[0]
    ns = T // SLC_LEN
    ncp = kcmp_ref.shape[0]
    q0 = pl.program_id(2) * tq
    t_row = q0 + lax.broadcasted_iota(jnp.int32, (1, tq), 1)
    q_pad = q_ref[...]
    q64 = q_pad.reshape(rows, LANES)[:, :HEAD_DIM]

    def all_heads(a):
        return jnp.concatenate([a] * R, axis=1)

    def head(a, r, width):
        return a[:, r * width:(r + 1) * width]

    n_win = tq // wq
    wl = WINDOW + wq

    def window_piece(h):
        t_piece = t_row[:, h * wq:(h + 1) * wq]
        ws = pl.multiple_of(jnp.maximum(q0 + h * wq - WINDOW, 0), wq)
        wpos = ws + lax.broadcasted_iota(jnp.int32, (wl, wq), 0)
        wbias = jnp.where((wpos <= t_piece) & (wpos > t_piece - WINDOW), 0.0, NEG)
        qh = q64.reshape(R, tq, HEAD_DIM)[:, h * wq:(h + 1) * wq, :].reshape(R * wq, HEAD_DIM)
        s = _dot_nt(kw_ref[pl.ds(ws, wl), :], qh) + jnp.concatenate([wbias] * R, axis=1)
        p = jnp.exp2(s - jnp.max(s, axis=0, keepdims=True)).astype(MXU_DTYPE)
        ow = _dot(vwt_ref[:, pl.ds(ws, wl)], p)
        return ow[:HEAD_DIM] / ow[HEAD_DIM:]

    cmp_end = lax.broadcasted_iota(jnp.int32, (ncp, tq), 0) * CMP_STRIDE + (CMP_LEN - 1)
    cbias = jnp.where(cmp_end <= t_row, 0.0, NEG)
    s = _dot_nt(kcmp_ref[...], q64) + all_heads(cbias)
    p = jnp.exp2(s - jnp.max(s, axis=0, keepdims=True))
    l = jnp.sum(p, axis=0, keepdims=True)
    p = p * (all_heads(jnp.where(t_row >= CMP_LEN - 1, 1.0, 0.0)) / l)
    o_cmp = _dot(vcmpt_ref[...], p.astype(MXU_DTYPE))

    psum = head(p, 0, tq) + head(p, 1, tq) + head(p, 2, tq) + head(p, 3, tq)
    p_hi = psum.astype(MXU_DTYPE)
    p_lo = (psum - p_hi.astype(F32)).astype(MXU_DTYPE)
    imp = _dot(aggt_ref[...], p_hi) + _dot(aggt_ref[...], p_lo)
    blk = lax.broadcasted_iota(jnp.int32, (ns, tq), 0)
    tb = (q0 + lax.broadcasted_iota(jnp.int32, (ns, tq), 1)) // SLC_LEN
    forced = (blk == 0) | (blk == tb) | (blk == tb - 1)
    score = jnp.where(forced, BIG, imp)
    score = jnp.where(blk <= tb, score, -BIG)
    ranks = [jnp.zeros((SUBLANES, tq), F32) for _ in range(ns // SUBLANES)]
    ranks = _rank_accumulate(score, ranks, 0, ns // 2)
    o_win_pieces = [window_piece(h) for h in range(n_win // 2)]
    ranks = _rank_accumulate(score, ranks, ns // 2, ns)
    unselected = jnp.where(jnp.concatenate(ranks, axis=0) < min(SLC_TOPK, ns), 0.0, 1.0)
    tag_rows = [jnp.zeros((HEAD_DIM, tq), F32), unselected]
    if ns < LANES - HEAD_DIM:
        tag_rows.append(jnp.zeros((LANES - HEAD_DIM - ns, tq), F32))
    tag = jnp.concatenate(tag_rows, axis=0).T.astype(MXU_DTYPE)
    lane = lax.broadcasted_iota(jnp.int32, (R, tq, LANES), 2)
    q_aug = jnp.where(lane < HEAD_DIM, q_pad, tag[None]).reshape(rows, LANES)

    def scores(c):
        return _dot_nt(ks_ref[pl.ds(pl.multiple_of(c * kc, kc), kc), :], q_aug)

    def causal_bias(c):
        kpos = c * kc + lax.broadcasted_iota(jnp.int32, (kc, tq), 0)
        return all_heads(jnp.where(kpos <= t_row, 0.0, NEG))

    def consume(s, c, m):
        m_new = jnp.maximum(m, jnp.max(s, axis=0, keepdims=True))
        p = jnp.exp2(s - m_new).astype(MXU_DTYPE)
        pv = _dot(vst_ref[:, pl.ds(pl.multiple_of(c * kc, kc), kc)], p)
        acc_ref[...] = jnp.exp2(m - m_new) * acc_ref[...] + pv
        return m_new

    n_pairs = q0 // (2 * kc)
    acc_ref[...] = jnp.zeros((2 * HEAD_DIM, rows), F32)
    sa_ref[...] = scores(0)
    o_win_pieces += [window_piece(h) for h in range(n_win // 2, n_win)]

    def pair_step(j, m):
        sb_ref[...] = scores(2 * j + 1)
        m = consume(sa_ref[...], 2 * j, m)
        sa_ref[...] = scores(2 * j + 2)
        return consume(sb_ref[...], 2 * j + 1, m)

    m = lax.fori_loop(0, n_pairs, pair_step, jnp.full((1, rows), NEG, F32))
    c_last = 2 * n_pairs
    m = consume(sa_ref[...] + causal_bias(c_last), c_last, m)

    @pl.when(q0 + tq > (c_last + 1) * kc)
    def _():
        consume(scores(c_last + 1) + causal_bias(c_last + 1), c_last + 1, m)

    acc = acc_ref[...]
    o_slc = acc[:HEAD_DIM] / acc[HEAD_DIM:]

    g = gate_ref[...]
    outs = []
    for r in range(R):
        c = N_GATES * r
        o_win = jnp.concatenate([head(piece, r, wq) for piece in o_win_pieces], axis=1)
        o = g[c:c + 1] * head(o_cmp, r, tq) + g[c + 1:c + 2] * head(o_slc, r, tq) + g[c + 2:c + 3] * o_win
        ms = jnp.mean(o * o, axis=0, keepdims=True)
        outs.append(o * lax.rsqrt(ms + 1e-6) * hg_ref[r * HEAD_DIM:(r + 1) * HEAD_DIM, :])
    o_ref[...] = jnp.concatenate(outs, axis=0).T.astype(o_ref.dtype)


def _nsa(q, cmp_kv, cmp_kv_t, ks, vst, kw, vwt, gates, agg_t, hg_attn, tq, kc, wq):
    B, _, T, _ = q.shape
    G, R = N_KV_GROUPS, Q_PER_KV
    ncp = cmp_kv.shape[2]
    ns = T // SLC_LEN
    assert T % (2 * kc) == 0 and kc % tq == 0 and tq % (2 * wq) == 0 and WINDOW % wq == 0 and T >= WINDOW + tq
    assert ns <= LANES - HEAD_DIM, "block tags must fit in the upper lanes of a key row"
    whole = lambda b, g, i: (b, g, 0, 0)
    hg_lanes = jnp.broadcast_to(hg_attn.reshape(G, R * HEAD_DIM, 1), (G, R * HEAD_DIM, tq))
    return pl.pallas_call(
        functools.partial(_nsa_kernel, tq=tq, kc=kc, wq=wq),
        grid=(B, G, T // tq),
        in_specs=[
            pl.BlockSpec((None, R, tq, LANES), lambda b, g, i: (b, g, i, 0)),
            pl.BlockSpec((None, None, ncp, HEAD_DIM), lambda b, g, i: (0, b * G + g, 0, 0)),
            pl.BlockSpec((None, None, HEAD_DIM, ncp), lambda b, g, i: (1, b * G + g, 0, 0)),
            pl.BlockSpec((None, None, T, LANES), whole),
            pl.BlockSpec((None, None, 2 * HEAD_DIM, T), whole),
            pl.BlockSpec((None, None, T, HEAD_DIM), whole),
            pl.BlockSpec((None, None, 2 * HEAD_DIM, T), whole),
            pl.BlockSpec((None, None, _GATE_ROWS, tq), lambda b, g, i: (b, g, 0, i)),
            pl.BlockSpec((ns, ncp), lambda b, g, i: (0, 0)),
            pl.BlockSpec((None, R * HEAD_DIM, tq), lambda b, g, i: (g, 0, 0)),
        ],
        out_specs=pl.BlockSpec((None, tq, R * HEAD_DIM), lambda b, g, i: (b, i, g)),
        out_shape=jax.ShapeDtypeStruct((B, T, ATTN_WIDTH), MXU_DTYPE),
        scratch_shapes=[pltpu.VMEM((kc, R * tq), F32), pltpu.VMEM((kc, R * tq), F32),
                        pltpu.VMEM((2 * HEAD_DIM, R * tq), F32)],
        compiler_params=pltpu.CompilerParams(
            dimension_semantics=("arbitrary", "arbitrary", "arbitrary"), vmem_limit_bytes=VMEM_LIMIT),
    )(q, cmp_kv, cmp_kv_t, ks, vst, kw, vwt, gates, agg_t, hg_lanes)


def _mix_ffn_kernel(conv_ref, attn_ref, x_ref, wout_ref, g1_ref, b1_ref,
                    wup_ref, cw_ref, cb_ref, wdn_ref, g_ref, b_ref, o_ref, tail_ref, ubuf, act_ref,
                    *, tiles_per_seq, fc):
    tm = x_ref.shape[0]

    @pl.when(pl.program_id(0) % tiles_per_seq == 0)
    def _():
        tail_ref[...] = jnp.zeros(tail_ref.shape, F32)

    mix = _dot(conv_ref[...], wout_ref[0:CONV_WIDTH, :]) + _dot(attn_ref[...], wout_ref[CONV_WIDTH:, :])
    x = _layer_norm(ALPHA * x_ref[...] + mix, g1_ref[...], b1_ref[...])
    xb = jnp.concatenate([tail_ref[...], x], axis=0).astype(MXU_DTYPE)
    tail_ref[...] = x[tm - SUBLANES:]

    def up_conv(col0, buf):
        buf[...] = _dot(xb, wup_ref[:, col0:col0 + fc])
        return (cw_ref[2:3, col0:col0 + fc] * buf[SUBLANES:SUBLANES + tm, :]
                + cw_ref[1:2, col0:col0 + fc] * buf[SUBLANES - 1:SUBLANES - 1 + tm, :]
                + cw_ref[0:1, col0:col0 + fc] * buf[SUBLANES - 2:SUBLANES - 2 + tm, :]
                + cb_ref[:, col0:col0 + fc])

    for c in range(D_FF // fc):
        val = up_conv(c * fc, ubuf.at[c % 2, 0])
        gate = up_conv(D_FF + c * fc, ubuf.at[c % 2, 1])
        act_ref[:, c * fc:(c + 1) * fc] = (gate * jax.nn.sigmoid(gate) * val).astype(act_ref.dtype)
    o_ref[...] = _layer_norm(ALPHA * x + _dot(act_ref[...], wdn_ref[...]), g_ref[...], b_ref[...])


def _mix_ffn(conv_n, attn_n, x, w_out, g1, b1, w_up, conv_w, conv_b, w_down, g, b, tm, tiles_per_seq, fc):
    N, D = x.shape
    row = lambda i: (i, 0)
    const = lambda i: (0, 0)
    return pl.pallas_call(
        functools.partial(_mix_ffn_kernel, tiles_per_seq=tiles_per_seq, fc=fc),
        grid=(N // tm,),
        in_specs=[
            pl.BlockSpec((tm, CONV_WIDTH), row), pl.BlockSpec((tm, ATTN_WIDTH), row), pl.BlockSpec((tm, D), row),
            pl.BlockSpec((D, D), const, pipeline_mode=pl.Buffered(1)),
            pl.BlockSpec((1, D), const), pl.BlockSpec((1, D), const),
            pl.BlockSpec((D, 2 * D_FF), const, pipeline_mode=pl.Buffered(1)),
            pl.BlockSpec((3, 2 * D_FF), const),
            pl.BlockSpec((1, 2 * D_FF), const),
            pl.BlockSpec((D_FF, D), const, pipeline_mode=pl.Buffered(1)),
            pl.BlockSpec((1, D), const), pl.BlockSpec((1, D), const),
        ],
        out_specs=pl.BlockSpec((tm, D), row),
        out_shape=jax.ShapeDtypeStruct((N, D), F32),
        scratch_shapes=[pltpu.VMEM((SUBLANES, D), F32), pltpu.VMEM((2, 2, tm + SUBLANES, fc), F32),
                        pltpu.VMEM((tm, D_FF), MXU_DTYPE)],
        compiler_params=pltpu.CompilerParams(
            dimension_semantics=("arbitrary",), vmem_limit_bytes=VMEM_LIMIT,
            allow_input_fusion=[False, False, False, True, False, False, True, False, False, True, False, False]),
    )(conv_n, attn_n, x, w_out, g1, b1, w_up, conv_w, conv_b, w_down, g, b)


def _rope_tables_kernel(freq_hi_ref, freq_lo_ref, sign_ref, cos_ref, rot_ref):
    tr = cos_ref.shape[0]
    pos = (pl.program_id(0) * tr + lax.broadcasted_iota(jnp.int32, (tr, LANES), 0)).astype(F32)
    ang = pos * freq_hi_ref[...] + pos * freq_lo_ref[...]
    cos_ref[...] = jnp.cos(ang)
    rot_ref[...] = jnp.sin(ang) * sign_ref[...]


def _rope_lane_tables(T, tr):
    half = ROPE_DIM // 2
    inv_freq = ROPE_THETA ** (-(np.arange(half, dtype=np.float64) * 2.0 / ROPE_DIM))
    lane = np.arange(LANES) % HEAD_DIM
    freq = np.where(lane < ROPE_DIM, inv_freq[lane % half], 0.0)
    freq_hi = freq.astype(np.float32)
    freq_lo = (freq - freq_hi.astype(np.float64)).astype(np.float32)
    sign = np.where(lane < half, -1.0, np.where(lane < ROPE_DIM, 1.0, 0.0)).astype(np.float32)
    lanes_spec = pl.BlockSpec((1, LANES), lambda i: (0, 0))
    rows_spec = pl.BlockSpec((tr, LANES), lambda i: (i, 0))
    return pl.pallas_call(
        _rope_tables_kernel,
        grid=(T // tr,),
        in_specs=[lanes_spec, lanes_spec, lanes_spec],
        out_specs=[rows_spec, rows_spec],
        out_shape=[jax.ShapeDtypeStruct((T, LANES), F32)] * 2,
    )(jnp.asarray(freq_hi)[None], jnp.asarray(freq_lo)[None], jnp.asarray(sign)[None])


def _group_mean_matrix():
    grp = np.arange(MXU_COLS) // HEAD_DIM
    return jnp.asarray((grp[:, None] == grp[None, :]) / HEAD_DIM, dtype=MXU_DTYPE)


def _slc_aggregation_t(T):
    nc = T // CMP_STRIDE - CMP_LEN // CMP_STRIDE + 1
    ns = T // SLC_LEN
    sc = np.arange(nc)[None, :] * CMP_STRIDE
    ss = np.arange(ns)[:, None] * SLC_LEN
    ov = np.clip(np.minimum(sc + CMP_LEN, ss + SLC_LEN) - np.maximum(sc, ss), 0, None) / CMP_LEN
    ov = np.pad(ov, ((0, 0), (0, T // CMP_STRIDE - nc)))
    return jnp.asarray(ov, dtype=MXU_DTYPE)


def _split_in_weights(w):
    w = w.astype(MXU_DTYPE)
    plain = jnp.concatenate([w[..., :_W_VS], w[..., _W_KW:_W_VW]], axis=-1)
    half = ROPE_DIM // 2
    partners = []
    for c0, width in ((_OFF_Q, ATTN_WIDTH), (_OFF_KSW, 2 * KV_WIDTH), (_OFF_KC, KV_WIDTH)):
        heads = plain[..., c0:c0 + width].reshape(w.shape[:-1] + (width // HEAD_DIM, HEAD_DIM))
        swapped = jnp.concatenate([heads[..., half:ROPE_DIM], heads[..., :half],
                                   jnp.zeros_like(heads[..., ROPE_DIM:])], axis=-1)
        partners.append(swapped.reshape(w.shape[:-1] + (width,)))
    w_main = jnp.concatenate([plain] + partners, axis=-1)
    per_group = Q_PER_KV * N_GATES
    pad = jnp.zeros(w.shape[:-1] + (_GATE_ROWS - per_group,), w.dtype)
    cols = [w[..., _W_VS:_W_KW], w[..., _W_VW:_W_GATE]]
    for g in range(N_KV_GROUPS):
        cols += [w[..., _W_GATE + g * per_group:_W_GATE + (g + 1) * per_group], pad]
    return w_main, jnp.swapaxes(jnp.concatenate(cols, axis=-1), -1, -2)


_TM_INPROJ, _TM_FFN, _FC_FFN, _TQ, _KC, _WQ = 1024, 512, 256, 512, 512, 128


def kernel(x, w_in, short_conv_w, cmp_pe, cmp_w1, cmp_b1, cmp_w2, cmp_b2, head_norm_g, w_out, ln1_g, ln1_b,
           w_up, ffn_conv_w, ffn_conv_b, w_down, ln2_g, ln2_b):
    B, T, D = x.shape
    assert D == D_MODEL and w_in.shape[0] == DEPTH
    tm = min(_TM_FFN, T)
    tm_in = min(_TM_INPROJ, T)
    cos_t, rot_t = _rope_lane_tables(T, tm_in)
    gmean = _group_mean_matrix()
    agg_t = _slc_aggregation_t(T)
    w_main, w_t = _split_in_weights(w_in)
    cmp_w1_b, cmp_w2_b = cmp_w1.astype(MXU_DTYPE), cmp_w2.astype(MXU_DTYPE)
    w_out_b, w_up_b, w_down_b = w_out.astype(MXU_DTYPE), w_up.astype(MXU_DTYPE), w_down.astype(MXU_DTYPE)
    cmp_pe_flat = cmp_pe.reshape(DEPTH, 2, 1, CMP_LEN * HEAD_DIM)
    for i in range(DEPTH):
        hg = head_norm_g[i]
        conv_n, q, cv, ks, kw, vst, vwt, gates = _inproj(
            x, w_main[i], w_t[i], short_conv_w[i], cos_t, rot_t, hg[None, :CONV_WIDTH], gmean, tm_in)
        cmp_kv, cmp_kv_t = _compress(
            cv.reshape(2, B * N_KV_GROUPS, T, HEAD_DIM), cmp_pe_flat[i], cmp_w1_b[i], cmp_b1[i], cmp_w2_b[i],
            cmp_b2[i])
        attn_n = _nsa(q, cmp_kv, cmp_kv_t, ks, vst, kw, vwt, gates, agg_t, hg[CONV_WIDTH:], tq=_TQ, kc=_KC,
                      wq=_WQ)
        x = _mix_ffn(conv_n.reshape(B * T, CONV_WIDTH), attn_n.reshape(B * T, ATTN_WIDTH), x.reshape(B * T, D),
                     w_out_b[i], ln1_g[i][None], ln1_b[i][None], w_up_b[i], ffn_conv_w[i], ffn_conv_b[i][None],
                     w_down_b[i], ln2_g[i][None], ln2_b[i][None], tm, T // tm, _FC_FFN).reshape(B, T, D)
    return x
```
